```python
import jax
import jax.numpy as jnp
from jax import lax
import numpy as np

D_MODEL = 1024
BATCH = 2
SEQ = 8192
DEPTH = 4
DEC_BATCH = 128
DEC_SEQ = 4
PAST_LEN = 8192
PAGE_SIZE = 128

A_HEADS = 8
A_KV_HEADS = 2
A_GROUP = A_HEADS // A_KV_HEADS
A_HEAD_DIM = 64
WINDOW = 128
A_ROT_DIM = A_HEAD_DIM // 4
ROPE_THETA_A = 500000.0
B_HEADS = 4
B_DK = 128
B_DV = 128
RET_CHUNK = 128
ROPE_THETA_B = 10000.0
EPS = 1e-6

A_WIDTH = A_HEADS * A_HEAD_DIM
A_KV_WIDTH = A_KV_HEADS * A_HEAD_DIM
B_QK_WIDTH = B_HEADS * B_DK
B_V_WIDTH = B_HEADS * B_DV
MIX_WIDTH = A_WIDTH + B_V_WIDTH
SPLITS = (A_WIDTH, A_KV_WIDTH, A_KV_WIDTH, A_WIDTH, B_QK_WIDTH, B_QK_WIDTH, B_V_WIDTH, B_V_WIDTH)
IN_WIDTH = A_WIDTH + 2 * A_KV_WIDTH + A_WIDTH + 2 * B_QK_WIDTH + 2 * B_V_WIDTH

kernel_name = 'hymba_swa_sink_retention_step'


def rms_norm(x, g=None):
    xf = x.astype(jnp.float32)
    y = xf * lax.rsqrt(jnp.mean(xf * xf, axis=-1, keepdims=True) + EPS)
    if g is not None:
        y = y * g.astype(jnp.float32)
    return y.astype(x.dtype)


def rotary(x, pos, rot_dim, theta):
    half = rot_dim // 2
    inv_freq = theta ** (-jnp.arange(half, dtype=jnp.float32) / half)
    ang = pos.astype(jnp.float32)[:, None] * inv_freq[None, :]
    cos = jnp.cos(ang)[:, None, :]
    sin = jnp.sin(ang)[:, None, :]
    xr = x[..., :rot_dim].astype(jnp.float32)
    x1, x2 = xr[..., :half], xr[..., half:]
    rot = jnp.concatenate([x1 * cos - x2 * sin, x2 * cos + x1 * sin], axis=-1)
    return jnp.concatenate([rot.astype(x.dtype), x[..., rot_dim:]], axis=-1)


def project(x, pos, w_in, norm_g, q_norm_g, k_norm_g):
    N, T, _ = x.shape
    h = rms_norm(x, norm_g)
    p = jnp.einsum('ntd,de->nte', h, w_in)
    offs = np.cumsum(SPLITS)[:-1].tolist()
    qa, ka, va, ga, qb, kb, vb, gb = jnp.split(p, offs, axis=-1)
    qa = rotary(rms_norm(qa.reshape(N, T, A_HEADS, A_HEAD_DIM), q_norm_g), pos, A_ROT_DIM, ROPE_THETA_A)
    ka = rotary(rms_norm(ka.reshape(N, T, A_KV_HEADS, A_HEAD_DIM), k_norm_g), pos, A_ROT_DIM, ROPE_THETA_A)
    va = va.reshape(N, T, A_KV_HEADS, A_HEAD_DIM)
    qb = rotary(qb.reshape(N, T, B_HEADS, B_DK), pos, B_DK, ROPE_THETA_B)
    kb = rotary(kb.reshape(N, T, B_HEADS, B_DK), pos, B_DK, ROPE_THETA_B) * (B_DK ** -0.5)
    vb = vb.reshape(N, T, B_HEADS, B_DV)
    return qa, ka, va, ga, qb, kb, vb, gb


def attend_with_sinks(q, kk, vv, mask, sinks):
    s = jnp.einsum('...qgrd,...kgd->...grqk', q, kk, preferred_element_type=jnp.float32) * (A_HEAD_DIM ** -0.5)
    s = jnp.where(mask, s, -jnp.inf)
    sk = sinks.astype(jnp.float32).reshape(A_KV_HEADS, A_GROUP)[:, :, None, None]
    m = jnp.maximum(jnp.max(s, axis=-1, keepdims=True), sk)
    p = jnp.exp(s - m)
    denom = jnp.sum(p, axis=-1, keepdims=True) + jnp.exp(sk - m)
    p = (p / denom).astype(vv.dtype)
    return jnp.einsum('...grqk,...kgd->...qgrd', p, vv)


def swa_prompt(qa, ka, va, sinks):
    N, T = qa.shape[:2]
    nb = T // WINDOW
    qblk = qa.reshape(N, nb, WINDOW, A_KV_HEADS, A_GROUP, A_HEAD_DIM)
    kblk = ka.reshape(N, nb, WINDOW, A_KV_HEADS, A_HEAD_DIM)
    vblk = va.reshape(N, nb, WINDOW, A_KV_HEADS, A_HEAD_DIM)
    kk = jnp.concatenate([jnp.concatenate([jnp.zeros_like(kblk[:, :1]), kblk[:, :-1]], axis=1), kblk], axis=2)
    vv = jnp.concatenate([jnp.concatenate([jnp.zeros_like(vblk[:, :1]), vblk[:, :-1]], axis=1), vblk], axis=2)
    qi = jnp.arange(WINDOW)[:, None] + WINDOW
    kj = jnp.arange(2 * WINDOW)[None, :]
    diff = qi - kj
    band = (diff >= 0) & (diff < WINDOW)
    has_prev = (jnp.arange(nb) > 0)[:, None, None] | (kj >= WINDOW)[None]
    mask = (band[None] & has_prev)[:, None, None]
    o = attend_with_sinks(qblk, kk, vv, mask, sinks)
    return o.reshape(N, T, A_WIDTH)


def swa_sample(qa, ka, va, cache_k, cache_v, sinks):
    N, T = qa.shape[:2]
    w_buf = cache_k.shape[1]
    kk = jnp.concatenate([cache_k.astype(ka.dtype), ka], axis=1)
    vv = jnp.concatenate([cache_v.astype(va.dtype), va], axis=1)
    diff = jnp.arange(T)[:, None] + w_buf - jnp.arange(w_buf + T)[None, :]
    mask = (diff >= 0) & (diff < WINDOW)
    o = attend_with_sinks(qa.reshape(N, T, A_KV_HEADS, A_GROUP, A_HEAD_DIM), kk, vv, mask, sinks)
    return o.reshape(N, T, A_WIDTH), kk[:, -w_buf:], vv[:, -w_buf:]


def retention_log_decay():
    return jnp.log(1.0 - 2.0 ** (-5.0 - jnp.arange(B_HEADS, dtype=jnp.float32)))


def retention_chunk(S, q, k, v):
    C = q.shape[1]
    lg = retention_log_decay()
    idx = jnp.arange(C, dtype=jnp.float32)
    diff = idx[:, None] - idx[None, :]
    decay = jnp.where(diff >= 0, jnp.exp(lg[:, None, None] * jnp.maximum(diff, 0.0)), 0.0)
    qf = q.astype(jnp.float32)
    kf = k.astype(jnp.float32)
    vf = v.astype(jnp.float32)
    scores = jnp.einsum('nihd,njhd->nhij', qf, kf) * decay
    inner = jnp.einsum('nhij,njhv->nihv', scores, vf)
    q_dec = jnp.exp(lg[None, :] * (idx[:, None] + 1.0))
    cross = jnp.einsum('nihd,nhdv->nihv', qf, S) * q_dec[None, :, :, None]
    k_dec = jnp.exp(lg[None, :] * (C - 1.0 - idx[:, None]))
    S_new = jnp.exp(lg * C)[None, :, None, None] * S + jnp.einsum('njhd,njhv->nhdv', kf * k_dec[None, :, :, None], vf)
    return S_new, inner + cross


def retention_prompt(qb, kb, vb):
    N, T = qb.shape[:2]
    C = min(RET_CHUNK, T)
    nc = T // C

    def to_chunks(a):
        return a.reshape(N, nc, C, *a.shape[2:]).swapaxes(0, 1)

    S0 = jnp.zeros((N, B_HEADS, B_DK, B_DV), jnp.float32)
    S, o = lax.scan(lambda S, xs: retention_chunk(S, *xs), S0, (to_chunks(qb), to_chunks(kb), to_chunks(vb)))
    return o.swapaxes(0, 1).reshape(N, T, B_HEADS, B_DV), S


def merge(x, attn, ret, ga, gb, w_out):
    N, T = x.shape[:2]
    ret = rms_norm(ret).reshape(N, T, B_V_WIDTH).astype(x.dtype)
    mixed = jnp.concatenate([jax.nn.silu(ga) * attn, jax.nn.silu(gb) * ret], axis=-1)
    return x + jnp.einsum('nte,ed->ntd', mixed, w_out)


def setup_inputs(seed: int = 0) -> dict:
    key = jax.random.key(seed)
    ks = jax.random.split(key, 11)
    w_buf = min(WINDOW, PAST_LEN)
    f32 = jnp.float32
    return {
        'x_prompt': jax.random.normal(ks[0], (BATCH, SEQ, D_MODEL), f32),
        'x_sample': jax.random.normal(ks[1], (DEC_BATCH, DEC_SEQ, D_MODEL), f32),
        'cache_swa_k': jax.random.normal(ks[2], (DEPTH, DEC_BATCH, w_buf, A_KV_HEADS, A_HEAD_DIM), f32),
        'cache_swa_v': jax.random.normal(ks[3], (DEPTH, DEC_BATCH, w_buf, A_KV_HEADS, A_HEAD_DIM), f32),
        'state_ret': 0.5 * jax.random.normal(ks[4], (DEPTH, DEC_BATCH, B_HEADS, B_DK, B_DV), f32),
        'w_in': jax.random.normal(ks[5], (DEPTH, D_MODEL, IN_WIDTH), f32) * (D_MODEL ** -0.5),
        'w_out': jax.random.normal(ks[6], (DEPTH, MIX_WIDTH, D_MODEL), f32) * (MIX_WIDTH ** -0.5),
        'norm_g': 1.0 + 0.02 * jax.random.normal(ks[7], (DEPTH, D_MODEL), f32),
        'q_norm_g': 1.0 + 0.02 * jax.random.normal(ks[8], (DEPTH, A_HEAD_DIM), f32),
        'k_norm_g': 1.0 + 0.02 * jax.random.normal(ks[9], (DEPTH, A_HEAD_DIM), f32),
        'sinks': 0.5 * jax.random.normal(ks[10], (DEPTH, A_HEADS), f32),
    }


def reference(x_prompt, x_sample, cache_swa_k, cache_swa_v, state_ret, w_in, w_out, norm_g, q_norm_g, k_norm_g, sinks):
    pos_p = jnp.arange(x_prompt.shape[1], dtype=jnp.int32)
    pos_s = PAST_LEN + jnp.arange(x_sample.shape[1], dtype=jnp.int32)
    w_p = min(WINDOW, x_prompt.shape[1])
    xp, xs = x_prompt, x_sample
    kp_l, vp_l, sp_l, ks_l, vs_l, ss_l = [], [], [], [], [], []
    for l in range(DEPTH):
        qa, ka, va, ga, qb, kb, vb, gb = project(xp, pos_p, w_in[l], norm_g[l], q_norm_g[l], k_norm_g[l])
        attn = swa_prompt(qa, ka, va, sinks[l])
        ret, S = retention_prompt(qb, kb, vb)
        xp = merge(xp, attn, ret, ga, gb, w_out[l])
        kp_l.append(ka[:, -w_p:])
        vp_l.append(va[:, -w_p:])
        sp_l.append(S.astype(xp.dtype))
        qa, ka, va, ga, qb, kb, vb, gb = project(xs, pos_s, w_in[l], norm_g[l], q_norm_g[l], k_norm_g[l])
        attn, k_buf, v_buf = swa_sample(qa, ka, va, cache_swa_k[l], cache_swa_v[l], sinks[l])
        S, ret = retention_chunk(state_ret[l].astype(jnp.float32), qb, kb, vb)
        xs = merge(xs, attn, ret, ga, gb, w_out[l])
        ks_l.append(k_buf)
        vs_l.append(v_buf)
        ss_l.append(S.astype(xs.dtype))
    return (xp, xs, jnp.stack(kp_l), jnp.stack(vp_l), jnp.stack(sp_l), jnp.stack(ks_l), jnp.stack(vs_l), jnp.stack(ss_l))
```

```python
import functools

import jax
import jax.numpy as jnp
from jax import lax
from jax.experimental import pallas as pl
from jax.experimental.pallas import tpu as pltpu

D_MODEL = 1024
DEPTH = 4
PAST_LEN = 8192
A_HEADS = 8
A_KV_HEADS = 2
A_GROUP = A_HEADS // A_KV_HEADS
A_HEAD_DIM = 64
WINDOW = 128
A_ROT_DIM = A_HEAD_DIM // 4
ROPE_THETA_A = 500000.0
B_HEADS = 4
B_DK = 128
B_DV = 128
RET_CHUNK = 128
ROPE_THETA_B = 10000.0
EPS = 1e-6

A_WIDTH = A_HEADS * A_HEAD_DIM
A_KV_WIDTH = A_KV_HEADS * A_HEAD_DIM
B_QK_WIDTH = B_HEADS * B_DK
B_V_WIDTH = B_HEADS * B_DV
MIX_WIDTH = A_WIDTH + B_V_WIDTH
OFF_QA = 0
OFF_KA = OFF_QA + A_WIDTH
OFF_VA = OFF_KA + A_KV_WIDTH
OFF_GA = OFF_VA + A_KV_WIDTH
OFF_QB = OFF_GA + A_WIDTH
OFF_KB = OFF_QB + B_QK_WIDTH
OFF_VB = OFF_KB + B_QK_WIDTH
OFF_GB = OFF_VB + B_V_WIDTH
IN_WIDTH = OFF_GB + B_V_WIDTH

LANES = 128
HEADS_PER_TILE = LANES // A_HEAD_DIM
A_TILES = A_WIDTH // LANES
NEG = -1e30

PROMPT_BLOCK = 512
SAMPLE_SEQS = 8
VMEM_LIMIT = 56 * 1024 * 1024

F32 = jnp.float32
BF16 = jnp.bfloat16


def _dot(a, b):
    return jnp.dot(a, b, preferred_element_type=F32)


def _dot_nt(a, b):
    return lax.dot_general(a, b, (((1,), (1,)), ((), ())), preferred_element_type=F32)


def _dot_tn(a, b):
    return lax.dot_general(a, b, (((0,), (0,)), ((), ())), preferred_element_type=F32)


def _silu(x):
    return x * (1.0 / (1.0 + jnp.exp(-x)))


def _row_rms(x):
    n = x.shape[-1]
    return x * lax.rsqrt(jnp.sum(x * x, axis=-1, keepdims=True) * (1.0 / n) + EPS)


def _pair_norm_rot(blk, g_row, ca, sa1, sa2, lo_mask):
    sq = blk * blk
    tot = jnp.sum(sq, axis=-1, keepdims=True)
    lo = jnp.sum(jnp.where(lo_mask, sq, 0.0), axis=-1, keepdims=True)
    hi = tot - lo
    inv_lo = lax.rsqrt(lo * (1.0 / A_HEAD_DIM) + EPS)
    inv_hi = lax.rsqrt(hi * (1.0 / A_HEAD_DIM) + EPS)
    xn = blk * jnp.where(lo_mask, inv_lo, inv_hi) * g_row
    half = A_ROT_DIM // 2
    return xn * ca + pltpu.roll(xn, LANES - half, 1) * sa1 + pltpu.roll(xn, half, 1) * sa2


def _rot_b(blk, cb, sb):
    return blk * cb + pltpu.roll(blk, B_DK // 2, 1) * sb


def _softmax_parts(parts, sink):
    mx = sink
    for s in parts:
        mx = jnp.maximum(mx, jnp.max(s, axis=-1, keepdims=True))
    ps = [jnp.exp(s - mx) for s in parts]
    den = jnp.exp(sink - mx)
    for p in ps:
        den = den + jnp.sum(p, axis=-1, keepdims=True)
    rinv = 1.0 / den
    return [p * rinv for p in ps]


def _prompt_kernel(sinks_ref, gam_ref, x_ref, win_ref, wout_ref, ng_ref, gq_ref, gk_ref,
                   ca_ref, sa1_ref, sa2_ref, cb_ref, sb_ref, dmat_ref, qdec_ref, kdec_ref,
                   y_ref, klast_ref, vlast_ref, S_ref,
                   q_s, kz_s, vz_s, qb_s, kb_s, kbd_s, vb_s, g_s, mix_s, *, tm):
    t = pl.program_id(1)
    nchunk = tm // RET_CHUNK
    C = RET_CHUNK

    @pl.when(t == 0)
    def _():
        S_ref[...] = jnp.zeros_like(S_ref)
        kz_s[:, 0:C, :] = jnp.zeros((2 * A_KV_HEADS, C, LANES), BF16)
        vz_s[:, 0:C, :] = jnp.zeros((2 * A_KV_HEADS, C, LANES), BF16)

    x = x_ref[0]
    h = (_row_rms(x) * ng_ref[...]).astype(BF16)

    lo_mask = lax.broadcasted_iota(jnp.int32, (tm, LANES), 1) < A_HEAD_DIM
    ca, sa1, sa2 = ca_ref[...], sa1_ref[...], sa2_ref[...]
    cb, sb = cb_ref[...], sb_ref[...]

    for m in range(A_TILES):
        blk = _dot(h, win_ref[:, OFF_QA + m * LANES:OFF_QA + (m + 1) * LANES])
        q_s[:, m * LANES:(m + 1) * LANES] = _pair_norm_rot(blk, gq_ref[...], ca, sa1, sa2, lo_mask).astype(BF16)

    kr = _pair_norm_rot(_dot(h, win_ref[:, OFF_KA:OFF_KA + LANES]), gk_ref[...], ca, sa1, sa2, lo_mask)
    va = _dot(h, win_ref[:, OFF_VA:OFF_VA + LANES])
    klast_ref[0] = kr[tm - WINDOW:, :]
    vlast_ref[0] = va[tm - WINDOW:, :]
    for src, dst in ((kr, kz_s), (va, vz_s)):
        sw = pltpu.roll(src, A_HEAD_DIM, 1)
        dst[0, C:, :] = jnp.where(lo_mask, src, 0.0).astype(BF16)
        dst[1, C:, :] = jnp.where(lo_mask, 0.0, sw).astype(BF16)
        dst[2, C:, :] = jnp.where(lo_mask, sw, 0.0).astype(BF16)
        dst[3, C:, :] = jnp.where(lo_mask, 0.0, src).astype(BF16)

    for m in range(A_TILES):
        g_s[:, m * LANES:(m + 1) * LANES] = _silu(_dot(h, win_ref[:, OFF_GA + m * LANES:OFF_GA + (m + 1) * LANES]))
    for hh in range(B_HEADS):
        sl = slice(hh * LANES, (hh + 1) * LANES)
        qb_s[:, sl] = _rot_b(_dot(h, win_ref[:, OFF_QB + hh * LANES:OFF_QB + (hh + 1) * LANES]), cb, sb).astype(BF16)
        kb = _rot_b(_dot(h, win_ref[:, OFF_KB + hh * LANES:OFF_KB + (hh + 1) * LANES]), cb, sb) * (B_DK ** -0.5)
        kb_s[:, sl] = kb.astype(BF16)
        kbd_s[:, sl] = (kb.reshape(nchunk, C, LANES) * kdec_ref[hh][None]).reshape(tm, LANES).astype(BF16)
        vb_s[:, sl] = _dot(h, win_ref[:, OFF_VB + hh * LANES:OFF_VB + (hh + 1) * LANES]).astype(BF16)
        g_s[:, A_WIDTH + hh * LANES:A_WIDTH + (hh + 1) * LANES] = _silu(
            _dot(h, win_ref[:, OFF_GB + hh * LANES:OFF_GB + (hh + 1) * LANES]))

    qi = lax.broadcasted_iota(jnp.int32, (C, 2 * C), 0)
    kj = lax.broadcasted_iota(jnp.int32, (C, 2 * C), 1)
    band = (kj > qi) & (kj <= qi + C)

    def chunk(c, carry):
        r0 = pl.multiple_of(c * C, C)
        rows = pl.ds(r0, C)
        win_rows = pl.ds(r0, 2 * C)
        kmin = jnp.where(jnp.logical_and(t == 0, c == 0), C, 0)
        mask = band & (kj >= kmin)

        for m in range(A_TILES):
            g = m // (A_GROUP // HEADS_PER_TILE)
            sl = slice(m * LANES, (m + 1) * LANES)
            qp = q_s[rows, sl]
            ps = []
            for half in range(HEADS_PER_TILE):
                s = jnp.where(mask, _dot_nt(qp, kz_s[2 * g + half, win_rows, :]), NEG)
                (p,) = _softmax_parts([s], sinks_ref[m * HEADS_PER_TILE + half])
                ps.append(p.astype(BF16))
            o = _dot(ps[0], vz_s[2 * g, win_rows, :]) + _dot(ps[1], vz_s[2 * g + 1, win_rows, :])
            mix_s[rows, sl] = (g_s[rows, sl] * o).astype(BF16)

        for hh in range(B_HEADS):
            sl = slice(hh * LANES, (hh + 1) * LANES)
            q = qb_s[rows, sl]
            k = kb_s[rows, sl]
            v = vb_s[rows, sl]
            kd = kbd_s[rows, sl]
            sc = _dot_nt(q, k) * dmat_ref[hh]
            S = S_ref[0, hh]
            ret = _dot(sc.astype(BF16), v) + _dot(q, S.astype(BF16)) * qdec_ref[hh]
            S_ref[0, hh] = gam_ref[hh] * S + _dot_tn(kd, v)
            osl = slice(A_WIDTH + hh * LANES, A_WIDTH + (hh + 1) * LANES)
            mix_s[rows, osl] = (g_s[rows, osl] * _row_rms(ret)).astype(BF16)
        return carry

    lax.fori_loop(0, nchunk, chunk, 0)

    y_ref[0] = x + _dot(mix_s[...], wout_ref[...])
    kz_s[:, 0:C, :] = kz_s[:, tm:tm + C, :]
    vz_s[:, 0:C, :] = vz_s[:, tm:tm + C, :]


def _const_spec(shape):
    nd = len(shape)
    return pl.BlockSpec(shape, lambda *_: (0,) * nd, pipeline_mode=pl.Buffered(1))


def _prompt_layer(x, win, wout, ng, gq, gk, sinks, gam, rot_a, rot_b, dmat, qdec, kdec):
    N, T, D = x.shape
    tm = PROMPT_BLOCK
    assert T % tm == 0 and tm % RET_CHUNK == 0 and WINDOW == RET_CHUNK
    nt = T // tm
    smem = pl.BlockSpec(memory_space=pltpu.SMEM)
    tab = pl.BlockSpec((tm, LANES), lambda n, t: (t, 0))
    kv_out = pl.BlockSpec((1, WINDOW, LANES), lambda n, t: (n, 0, 0))
    return pl.pallas_call(
        functools.partial(_prompt_kernel, tm=tm),
        grid=(N, nt),
        in_specs=[smem, smem,
                  pl.BlockSpec((1, tm, D), lambda n, t: (n, t, 0)),
                  _const_spec((D, IN_WIDTH)), _const_spec((MIX_WIDTH, D)),
                  _const_spec((1, D)), _const_spec((1, LANES)), _const_spec((1, LANES)),
                  tab, tab, tab, tab, tab,
                  _const_spec((B_HEADS, RET_CHUNK, RET_CHUNK)), _const_spec((B_HEADS, RET_CHUNK, LANES)),
                  _const_spec((B_HEADS, RET_CHUNK, LANES))],
        out_specs=[pl.BlockSpec((1, tm, D), lambda n, t: (n, t, 0)), kv_out, kv_out,
                   pl.BlockSpec((1, B_HEADS, B_DK, B_DV), lambda n, t: (n, 0, 0, 0))],
        out_shape=[jax.ShapeDtypeStruct((N, T, D), F32),
                   jax.ShapeDtypeStruct((N, WINDOW, LANES), F32),
                   jax.ShapeDtypeStruct((N, WINDOW, LANES), F32),
                   jax.ShapeDtypeStruct((N, B_HEADS, B_DK, B_DV), F32)],
        scratch_shapes=[pltpu.VMEM((tm, A_WIDTH), BF16),
                        pltpu.VMEM((2 * A_KV_HEADS, RET_CHUNK + tm, LANES), BF16),
                        pltpu.VMEM((2 * A_KV_HEADS, RET_CHUNK + tm, LANES), BF16),
                        pltpu.VMEM((tm, B_QK_WIDTH), BF16),
                        pltpu.VMEM((tm, B_QK_WIDTH), BF16),
                        pltpu.VMEM((tm, B_QK_WIDTH), BF16),
                        pltpu.VMEM((tm, B_V_WIDTH), BF16),
                        pltpu.VMEM((tm, MIX_WIDTH), F32),
                        pltpu.VMEM((tm, MIX_WIDTH), BF16)],
        compiler_params=pltpu.CompilerParams(dimension_semantics=("arbitrary", "arbitrary"),
                                             vmem_limit_bytes=VMEM_LIMIT),
        name="prompt_layer",
    )(sinks, gam, x, win, wout, ng, gq, gk, *rot_a, *rot_b, dmat, qdec, kdec)


def _sample_kernel(gam_ref, x_ref, win_ref, wout_ref, ng_ref, gq_ref, gk_ref,
                   ca_ref, sa1_ref, sa2_ref, cb_ref, sb_ref, sink_ref, d8_ref, qdec_ref, kdec_ref,
                   ck_ref, cv_ref, st_ref,
                   y_ref, ko_ref, vo_ref, so_ref,
                   qz_s, kn_s, vn_s, qb_s, kb_s, kbd_s, vb_s, g_s, mix_s, *, nseq, dec):
    step = pl.program_id(0)
    ntok = x_ref.shape[0]
    pair_rows = 2 * dec
    npair = nseq // 2

    @pl.when(step == 0)
    def _():
        x = x_ref[...]
        h = (_row_rms(x) * ng_ref[...]).astype(BF16)
        lo_mask = lax.broadcasted_iota(jnp.int32, (ntok, LANES), 1) < A_HEAD_DIM
        ca, sa1, sa2 = ca_ref[...], sa1_ref[...], sa2_ref[...]
        cb, sb = cb_ref[...], sb_ref[...]
        for m in range(A_TILES):
            blk = _dot(h, win_ref[:, OFF_QA + m * LANES:OFF_QA + (m + 1) * LANES])
            qr = _pair_norm_rot(blk, gq_ref[...], ca, sa1, sa2, lo_mask)
            sw = pltpu.roll(qr, A_HEAD_DIM, 1)
            if m < A_TILES // 2:
                qz_s[2 * m] = jnp.where(lo_mask, qr, 0.0)
                qz_s[2 * m + 1] = jnp.where(lo_mask, sw, 0.0)
            else:
                qz_s[2 * m] = jnp.where(lo_mask, 0.0, sw)
                qz_s[2 * m + 1] = jnp.where(lo_mask, 0.0, qr)
        kn_s[...] = _pair_norm_rot(_dot(h, win_ref[:, OFF_KA:OFF_KA + LANES]), gk_ref[...], ca, sa1, sa2, lo_mask)
        vn_s[...] = _dot(h, win_ref[:, OFF_VA:OFF_VA + LANES])
        for m in range(A_TILES):
            g_s[:, m * LANES:(m + 1) * LANES] = _silu(
                _dot(h, win_ref[:, OFF_GA + m * LANES:OFF_GA + (m + 1) * LANES]))
        for hh in range(B_HEADS):
            sl = slice(hh * LANES, (hh + 1) * LANES)
            qb_s[:, sl] = _rot_b(_dot(h, win_ref[:, OFF_QB + hh * LANES:OFF_QB + (hh + 1) * LANES]), cb, sb)
            kb = _rot_b(_dot(h, win_ref[:, OFF_KB + hh * LANES:OFF_KB + (hh + 1) * LANES]), cb, sb) * (B_DK ** -0.5)
            kb_s[:, sl] = kb
            kbd_s[:, sl] = kb * kdec_ref[hh]
            vb_s[:, sl] = _dot(h, win_ref[:, OFF_VB + hh * LANES:OFF_VB + (hh + 1) * LANES])
            g_s[:, A_WIDTH + hh * LANES:A_WIDTH + (hh + 1) * LANES] = _silu(
                _dot(h, win_ref[:, OFF_GB + hh * LANES:OFF_GB + (hh + 1) * LANES]))

    nq = A_HEADS * pair_rows
    row = lax.broadcasted_iota(jnp.int32, (nq, LANES), 0)
    key = lax.broadcasted_iota(jnp.int32, (nq, LANES), 1)
    tok = row % dec
    first_seq = (row % pair_rows) < dec
    cache_mask = key > tok
    rown = lax.broadcasted_iota(jnp.int32, (nq, pair_rows), 0)
    coln = lax.broadcasted_iota(jnp.int32, (nq, pair_rows), 1)
    new_mask = ((coln // dec) == ((rown % pair_rows) // dec)) & ((coln % dec) <= (rown % dec))
    lo8 = lax.broadcasted_iota(jnp.int32, (pair_rows, LANES), 1) < A_HEAD_DIM
    r8 = lax.broadcasted_iota(jnp.int32, (pair_rows, LANES), 0)
    sink = sink_ref[...]

    def pair(p, carry):
        b0 = 2 * p
        base = pl.multiple_of((step * npair + p) * pair_rows, pair_rows)
        rows = pl.ds(base, pair_rows)

        qz = qz_s[:, rows, :].reshape(nq, LANES).astype(BF16)
        kn = kn_s[rows, :]
        vn = vn_s[rows, :]
        k0 = ck_ref[b0]
        k1 = ck_ref[b0 + 1]
        v0 = cv_ref[b0]
        v1 = cv_ref[b0 + 1]
        s_c = jnp.where(first_seq, _dot_nt(qz, k0.astype(BF16)), _dot_nt(qz, k1.astype(BF16)))
        s_c = jnp.where(cache_mask, s_c, NEG)
        s_n = jnp.where(new_mask, _dot_nt(qz, kn.astype(BF16)), NEG)
        p_c, p_n = _softmax_parts([s_c, s_n], sink)
        p_c = p_c.astype(BF16)
        o = jnp.where(first_seq, _dot(p_c, v0.astype(BF16)), _dot(p_c, v1.astype(BF16)))
        o = o + _dot(p_n.astype(BF16), vn.astype(BF16))
        for m in range(A_TILES):
            oa = o[(2 * m) * pair_rows:(2 * m + 1) * pair_rows, :]
            ob = o[(2 * m + 1) * pair_rows:(2 * m + 2) * pair_rows, :]
            if m < A_TILES // 2:
                blk = jnp.where(lo8, oa, pltpu.roll(ob, A_HEAD_DIM, 1))
            else:
                blk = jnp.where(lo8, pltpu.roll(oa, A_HEAD_DIM, 1), ob)
            sl = slice(m * LANES, (m + 1) * LANES)
            mix_s[rows, sl] = g_s[rows, sl] * blk

        for j, (kc, vc) in enumerate(((k0, v0), (k1, v1))):
            ko_ref[b0 + j, 0:WINDOW - dec, :] = kc[dec:, :]
            vo_ref[b0 + j, 0:WINDOW - dec, :] = vc[dec:, :]
            ko_ref[b0 + j, WINDOW - dec:, :] = kn[j * dec:(j + 1) * dec, :]
            vo_ref[b0 + j, WINDOW - dec:, :] = vn[j * dec:(j + 1) * dec, :]

        for hh in range(B_HEADS):
            sl = slice(hh * LANES, (hh + 1) * LANES)
            q = qb_s[rows, sl].astype(BF16)
            k = kb_s[rows, sl].astype(BF16)
            kd = kbd_s[rows, sl]
            v = vb_s[rows, sl].astype(BF16)
            sc = _dot_nt(q, k) * d8_ref[hh]
            inner = _dot(sc.astype(BF16), v)
            crosses = []
            for j in range(2):
                S = st_ref[b0 + j, hh]
                crosses.append(_dot(q, S.astype(BF16)))
                own = (r8 // dec) == j
                kdj = jnp.where(own, kd, 0.0).astype(BF16)
                so_ref[b0 + j, hh] = gam_ref[hh] * S + _dot_tn(kdj, v)
            cross = jnp.where(r8 < dec, crosses[0], crosses[1]) * qdec_ref[hh]
            osl = slice(A_WIDTH + hh * LANES, A_WIDTH + (hh + 1) * LANES)
            mix_s[rows, osl] = g_s[rows, osl] * _row_rms(inner + cross)
        return carry

    lax.fori_loop(0, npair, pair, 0)

    @pl.when(step == pl.num_programs(0) - 1)
    def _():
        y_ref[...] = x_ref[...] + _dot(mix_s[...].astype(BF16), wout_ref[...])


def _sample_layer(x, win, wout, ng, gq, gk, sink_col, gam, rot_a, rot_b, d8, qdec, kdec, ck, cv, st, dec):
    ntok, D = x.shape
    nb = ck.shape[0]
    nseq = SAMPLE_SEQS
    assert nb % nseq == 0 and nseq % 2 == 0 and ntok == nb * dec and 2 * dec == 8
    pair_rows = 2 * dec
    nq = A_HEADS * pair_rows
    smem = pl.BlockSpec(memory_space=pltpu.SMEM)
    kv = pl.BlockSpec((nseq, WINDOW, LANES), lambda i: (i, 0, 0))
    stt = pl.BlockSpec((nseq, B_HEADS, B_DK, B_DV), lambda i: (i, 0, 0, 0))
    tab = _const_spec((ntok, LANES))
    return pl.pallas_call(
        functools.partial(_sample_kernel, nseq=nseq, dec=dec),
        grid=(nb // nseq,),
        in_specs=[smem, _const_spec((ntok, D)), _const_spec((D, IN_WIDTH)), _const_spec((MIX_WIDTH, D)),
                  _const_spec((1, D)), _const_spec((1, LANES)), _const_spec((1, LANES)),
                  tab, tab, tab, tab, tab,
                  _const_spec((nq, 1)), _const_spec((B_HEADS, pair_rows, pair_rows)),
                  _const_spec((B_HEADS, pair_rows, LANES)), _const_spec((B_HEADS, ntok, LANES)),
                  kv, kv, stt],
        out_specs=[pl.BlockSpec((ntok, D), lambda i: (0, 0)), kv, kv, stt],
        out_shape=[jax.ShapeDtypeStruct((ntok, D), F32),
                   jax.ShapeDtypeStruct(ck.shape, F32),
                   jax.ShapeDtypeStruct(cv.shape, F32),
                   jax.ShapeDtypeStruct(st.shape, F32)],
        scratch_shapes=[pltpu.VMEM((A_HEADS, ntok, LANES), F32),
                        pltpu.VMEM((ntok, LANES), F32),
                        pltpu.VMEM((ntok, LANES), F32),
                        pltpu.VMEM((ntok, B_QK_WIDTH), F32),
                        pltpu.VMEM((ntok, B_QK_WIDTH), F32),
                        pltpu.VMEM((ntok, B_QK_WIDTH), F32),
                        pltpu.VMEM((ntok, B_V_WIDTH), F32),
                        pltpu.VMEM((ntok, MIX_WIDTH), F32),
                        pltpu.VMEM((ntok, MIX_WIDTH), F32)],
        compiler_params=pltpu.CompilerParams(dimension_semantics=("arbitrary",),
                                             vmem_limit_bytes=VMEM_LIMIT),
        name="sample_layer",
    )(gam, x, win, wout, ng, gq, gk, *rot_a, *rot_b, sink_col, d8, qdec, kdec, ck, cv, st)


def _rot_tables_a(pos):
    half = A_ROT_DIM // 2
    inv_freq = ROPE_THETA_A ** (-jnp.arange(half, dtype=F32) / half)
    ang = pos.astype(F32)[:, None] * inv_freq[None, :]
    cos, sin = jnp.cos(ang), jnp.sin(ang)
    n = pos.shape[0]
    zh = jnp.zeros((n, half), F32)
    rest = A_HEAD_DIM - A_ROT_DIM
    c = jnp.concatenate([cos, cos, jnp.ones((n, rest), F32)], axis=-1)
    s1 = jnp.concatenate([-sin, zh, jnp.zeros((n, rest), F32)], axis=-1)
    s2 = jnp.concatenate([zh, sin, jnp.zeros((n, rest), F32)], axis=-1)
    return tuple(jnp.tile(a, (1, HEADS_PER_TILE)) for a in (c, s1, s2))


def _rot_tables_b(pos):
    half = B_DK // 2
    inv_freq = ROPE_THETA_B ** (-jnp.arange(half, dtype=F32) / half)
    ang = pos.astype(F32)[:, None] * inv_freq[None, :]
    cos, sin = jnp.cos(ang), jnp.sin(ang)
    return jnp.concatenate([cos, cos], axis=-1), jnp.concatenate([-sin, sin], axis=-1)


def _log_decay():
    return jnp.log(1.0 - 2.0 ** (-5.0 - jnp.arange(B_HEADS, dtype=F32)))


def _decay_tables(C):
    lg = _log_decay()
    idx = jnp.arange(C, dtype=F32)
    diff = idx[:, None] - idx[None, :]
    dmat = jnp.where(diff >= 0, jnp.exp(lg[:, None, None] * jnp.maximum(diff, 0.0)), 0.0)
    q_dec = jnp.exp(lg[None, :] * (idx[:, None] + 1.0))
    k_dec = jnp.exp(lg[None, :] * (C - 1.0 - idx[:, None]))
    gam = jnp.exp(lg * C)
    return dmat, q_dec, k_dec, gam


def kernel(x_prompt, x_sample, cache_swa_k, cache_swa_v, state_ret, w_in, w_out, norm_g, q_norm_g, k_norm_g, sinks):
    N, T, D = x_prompt.shape
    nb, dec, _ = x_sample.shape
    w_buf = cache_swa_k.shape[2]
    assert w_buf == WINDOW

    pos_p = jnp.arange(T, dtype=jnp.int32)
    pos_s = PAST_LEN + jnp.arange(dec, dtype=jnp.int32)
    rot_a_p, rot_b_p = _rot_tables_a(pos_p), _rot_tables_b(pos_p)
    rot_a_s = tuple(jnp.tile(a, (nb, 1)) for a in _rot_tables_a(pos_s))
    rot_b_s = tuple(jnp.tile(a, (nb, 1)) for a in _rot_tables_b(pos_s))

    dmat, q_dec, k_dec, gam_p = _decay_tables(RET_CHUNK)
    qdec_p = jnp.broadcast_to(q_dec.T[:, :, None], (B_HEADS, RET_CHUNK, LANES))
    kdec_p = jnp.broadcast_to(k_dec.T[:, :, None], (B_HEADS, RET_CHUNK, LANES))
    dmat4, q_dec4, k_dec4, gam_s = _decay_tables(dec)
    d8 = jnp.kron(jnp.eye(2, dtype=F32)[None], jnp.ones((1, dec, dec), F32)) * jnp.tile(dmat4, (1, 2, 2))
    qdec_s = jnp.broadcast_to(jnp.tile(q_dec4.T, (1, 2))[:, :, None], (B_HEADS, 2 * dec, LANES))
    kdec_s = jnp.broadcast_to(jnp.tile(k_dec4.T, (1, nb))[:, :, None], (B_HEADS, nb * dec, LANES))

    w_in_b = w_in.astype(BF16)
    w_out_b = w_out.astype(BF16)
    ck = cache_swa_k.reshape(DEPTH, nb, w_buf, LANES)
    cv = cache_swa_v.reshape(DEPTH, nb, w_buf, LANES)

    xp = x_prompt
    xs = x_sample.reshape(nb * dec, D)
    kp_l, vp_l, sp_l, ks_l, vs_l, ss_l = [], [], [], [], [], []
    for l in range(DEPTH):
        ng = norm_g[l][None, :]
        gq = jnp.tile(q_norm_g[l], HEADS_PER_TILE)[None, :] * (A_HEAD_DIM ** -0.5)
        gk = jnp.tile(k_norm_g[l], HEADS_PER_TILE)[None, :]
        xp, kl, vl, S = _prompt_layer(xp, w_in_b[l], w_out_b[l], ng, gq, gk, sinks[l], gam_p,
                                      rot_a_p, rot_b_p, dmat, qdec_p, kdec_p)
        kp_l.append(kl)
        vp_l.append(vl)
        sp_l.append(S)
        sink_col = jnp.repeat(sinks[l], 2 * dec)[:, None]
        xs, ko, vo, so = _sample_layer(xs, w_in_b[l], w_out_b[l], ng, gq, gk, sink_col, gam_s,
                                       rot_a_s, rot_b_s, d8, qdec_s, kdec_s, ck[l], cv[l], state_ret[l], dec)
        ks_l.append(ko)
        vs_l.append(vo)
        ss_l.append(so)

    kv_p = (DEPTH, N, WINDOW, A_KV_HEADS, A_HEAD_DIM)
    kv_s = (DEPTH, nb, w_buf, A_KV_HEADS, A_HEAD_DIM)
    return (xp, xs.reshape(nb, dec, D),
            jnp.stack(kp_l).reshape(kv_p), jnp.stack(vp_l).reshape(kv_p), jnp.stack(sp_l),
            jnp.stack(ks_l).reshape(kv_s), jnp.stack(vs_l).reshape(kv_s), jnp.stack(ss_l))
```

```python
import functools

import jax
import jax.numpy as jnp
from jax import lax
from jax.experimental import pallas as pl
from jax.experimental.pallas import tpu as pltpu

D_MODEL = 1024
DEPTH = 4
PAST_LEN = 8192
A_HEADS = 8
A_KV_HEADS = 2
A_GROUP = A_HEADS // A_KV_HEADS
A_HEAD_DIM = 64
WINDOW = 128
A_ROT_DIM = A_HEAD_DIM // 4
ROPE_THETA_A = 500000.0
B_HEADS = 4
B_DK = 128
B_DV = 128
RET_CHUNK = 128
ROPE_THETA_B = 10000.0
EPS = 1e-6

A_WIDTH = A_HEADS * A_HEAD_DIM
A_KV_WIDTH = A_KV_HEADS * A_HEAD_DIM
B_QK_WIDTH = B_HEADS * B_DK
B_V_WIDTH = B_HEADS * B_DV
MIX_WIDTH = A_WIDTH + B_V_WIDTH
OFF_QA = 0
OFF_KA = OFF_QA + A_WIDTH
OFF_VA = OFF_KA + A_KV_WIDTH
OFF_GA = OFF_VA + A_KV_WIDTH
OFF_QB = OFF_GA + A_WIDTH
OFF_KB = OFF_QB + B_QK_WIDTH
OFF_VB = OFF_KB + B_QK_WIDTH
OFF_GB = OFF_VB + B_V_WIDTH
IN_WIDTH = OFF_GB + B_V_WIDTH

LANES = 128
HEADS_PER_TILE = LANES // A_HEAD_DIM
A_TILES = A_WIDTH // LANES
NEG = -1e30
LOG2E = 1.4426950408889634

PROMPT_BLOCK = 512
SAMPLE_SEQS = 8
VMEM_LIMIT = 56 * 1024 * 1024

F32 = jnp.float32
BF16 = jnp.bfloat16


def _dot(a, b):
    return jnp.dot(a, b, preferred_element_type=F32)


def _dot_nt(a, b):
    return lax.dot_general(a, b, (((1,), (1,)), ((), ())), preferred_element_type=F32)


def _dot_tn(a, b):
    return lax.dot_general(a, b, (((0,), (0,)), ((), ())), preferred_element_type=F32)


def _silu(x):
    return x * (1.0 / (1.0 + jnp.exp(-x)))


def _row_rms(x):
    n = x.shape[-1]
    return x * lax.rsqrt(jnp.sum(x * x, axis=-1, keepdims=True) * (1.0 / n) + EPS)


def _pair_norm_rot(blk, g_row, ca, sa1, sa2, lo_mask):
    sq = blk * blk
    tot = jnp.sum(sq, axis=-1, keepdims=True)
    lo = jnp.sum(jnp.where(lo_mask, sq, 0.0), axis=-1, keepdims=True)
    hi = tot - lo
    inv_lo = lax.rsqrt(lo * (1.0 / A_HEAD_DIM) + EPS)
    inv_hi = lax.rsqrt(hi * (1.0 / A_HEAD_DIM) + EPS)
    xn = blk * jnp.where(lo_mask, inv_lo, inv_hi) * g_row
    half = A_ROT_DIM // 2
    return xn * ca + pltpu.roll(xn, LANES - half, 1) * sa1 + pltpu.roll(xn, half, 1) * sa2


def _rot_b(blk, cb, sb):
    return blk * cb + pltpu.roll(blk, B_DK // 2, 1) * sb


def _softmax_parts(parts, sink):
    mx = sink
    for s in parts:
        mx = jnp.maximum(mx, jnp.max(s, axis=-1, keepdims=True))
    ps = [jnp.exp2(s - mx) for s in parts]
    den = jnp.exp2(sink - mx)
    for p in ps:
        den = den + jnp.sum(p, axis=-1, keepdims=True)
    rinv = 1.0 / den
    return [p * rinv for p in ps]


def _prompt_kernel(sinks_ref, gam_ref, x_ref, win_ref, wout_ref, ng_ref, gq_ref, gk_ref,
                   ca_ref, sa1_ref, sa2_ref, cb_ref, sb_ref, dmat_ref, qdec_ref, kdec_ref,
                   y_ref, klast_ref, vlast_ref, S_ref,
                   q_s, kz_s, vz_s, qb_s, kb_s, kbd_s, vb_s, g_s, mix_s, bias_s, s_s, p_s, es_s, *, tm):
    n = pl.program_id(0)
    t = pl.program_id(1)
    nchunk = tm // RET_CHUNK
    C = RET_CHUNK
    lo_c = lax.broadcasted_iota(jnp.int32, (C, LANES), 1) < A_HEAD_DIM

    @pl.when(jnp.logical_and(n == 0, t == 0))
    def _():
        qi = lax.broadcasted_iota(jnp.int32, (C, 2 * C), 0)
        kj = lax.broadcasted_iota(jnp.int32, (C, 2 * C), 1)
        bias_s[...] = jnp.where((kj > qi) & (kj <= qi + C), 0.0, NEG)

    @pl.when(t == 0)
    def _():
        S_ref[...] = jnp.zeros_like(S_ref)
        for i in range(2 * A_KV_HEADS):
            kz_s[i, 0:C, :] = jnp.zeros((C, LANES), BF16)
            head_half, ones_half = (0.0, 1.0) if i % 2 == 0 else (1.0, 0.0)
            vz_s[i, 0:C, :] = jnp.where(lo_c, head_half, ones_half).astype(BF16)

    x = x_ref[0]
    h = (_row_rms(x) * ng_ref[...]).astype(BF16)

    lo_mask = lax.broadcasted_iota(jnp.int32, (tm, LANES), 1) < A_HEAD_DIM
    ca, sa1, sa2 = ca_ref[...], sa1_ref[...], sa2_ref[...]
    cb, sb = cb_ref[...], sb_ref[...]

    def tile(a, i):
        return a[:, i * LANES:(i + 1) * LANES]

    qa = _dot(h, win_ref[:, OFF_QA:OFF_QA + A_WIDTH])
    for m in range(A_TILES):
        q_s[:, m * LANES:(m + 1) * LANES] = _pair_norm_rot(tile(qa, m), gq_ref[...], ca, sa1, sa2, lo_mask).astype(BF16)

    kv = _dot(h, win_ref[:, OFF_KA:OFF_KA + 2 * A_KV_WIDTH])
    kr = _pair_norm_rot(tile(kv, 0), gk_ref[...], ca, sa1, sa2, lo_mask)
    va = tile(kv, 1)
    klast_ref[0] = kr[tm - WINDOW:, :]
    vlast_ref[0] = va[tm - WINDOW:, :]
    for src, dst, fill in ((kr, kz_s, 0.0), (va, vz_s, 1.0)):
        sw = pltpu.roll(src, A_HEAD_DIM, 1)
        dst[0, C:, :] = jnp.where(lo_mask, src, fill).astype(BF16)
        dst[1, C:, :] = jnp.where(lo_mask, fill, sw).astype(BF16)
        dst[2, C:, :] = jnp.where(lo_mask, sw, fill).astype(BF16)
        dst[3, C:, :] = jnp.where(lo_mask, fill, src).astype(BF16)

    ga = _dot(h, win_ref[:, OFF_GA:OFF_GA + A_WIDTH])
    for m in range(A_TILES):
        g_s[:, m * LANES:(m + 1) * LANES] = _silu(tile(ga, m))
    qb = _dot(h, win_ref[:, OFF_QB:OFF_QB + B_QK_WIDTH])
    for hh in range(B_HEADS):
        qb_s[:, hh * LANES:(hh + 1) * LANES] = _rot_b(tile(qb, hh), cb, sb).astype(BF16)
    kb = _dot(h, win_ref[:, OFF_KB:OFF_KB + B_QK_WIDTH])
    for hh in range(B_HEADS):
        sl = slice(hh * LANES, (hh + 1) * LANES)
        kbh = _rot_b(tile(kb, hh), cb, sb) * (B_DK ** -0.5)
        kb_s[:, sl] = kbh.astype(BF16)
        kbd_s[:, sl] = (kbh.reshape(nchunk, C, LANES) * kdec_ref[hh][None]).reshape(tm, LANES).astype(BF16)
    vb_s[...] = _dot(h, win_ref[:, OFF_VB:OFF_VB + B_V_WIDTH]).astype(BF16)
    gb = _dot(h, win_ref[:, OFF_GB:OFF_GB + B_V_WIDTH])
    for hh in range(B_HEADS):
        g_s[:, A_WIDTH + hh * LANES:A_WIDTH + (hh + 1) * LANES] = _silu(tile(gb, hh))

    first_bias = jnp.where(t == 0, NEG, 0.0)

    def scores(c):
        buf = c % 2
        rows = slice(c * C, (c + 1) * C)
        win_rows = slice(c * C, (c + 2) * C)
        for m in range(A_TILES):
            g = m // (A_GROUP // HEADS_PER_TILE)
            qp = q_s[rows, m * LANES:(m + 1) * LANES]
            for half in range(HEADS_PER_TILE):
                s = _dot_nt(qp, kz_s[2 * g + half, win_rows, :]) + bias_s[...]
                if c == 0:
                    s = jnp.concatenate([s[:, :C] + first_bias, s[:, C:]], axis=1)
                s_s[buf, m * HEADS_PER_TILE + half] = s

    def softmax(c):
        buf = c % 2
        for m in range(A_TILES):
            es = []
            for half in range(HEADS_PER_TILE):
                hd = m * HEADS_PER_TILE + half
                s = s_s[buf, hd]
                sink = sinks_ref[hd]
                mx = jnp.maximum(jnp.max(s, axis=-1, keepdims=True), sink)
                p_s[buf, hd] = jnp.exp2(s - mx).astype(BF16)
                es.append(jnp.exp2(sink - mx))
            es_s[buf, m] = jnp.where(lo_c, es[0], es[1])

    def attend(c):
        buf = c % 2
        rows = slice(c * C, (c + 1) * C)
        win_rows = slice(c * C, (c + 2) * C)
        for m in range(A_TILES):
            g = m // (A_GROUP // HEADS_PER_TILE)
            sl = slice(m * LANES, (m + 1) * LANES)
            oe = _dot(p_s[buf, 2 * m], vz_s[2 * g, win_rows, :])
            oo = _dot(p_s[buf, 2 * m + 1], vz_s[2 * g + 1, win_rows, :])
            pv = jnp.where(lo_c, oe, oo)
            den = pltpu.roll(jnp.where(lo_c, oo, oe), A_HEAD_DIM, 1) + es_s[buf, m]
            mix_s[rows, sl] = (g_s[rows, sl] * pv * (1.0 / den)).astype(BF16)

    def retain(c):
        rows = slice(c * C, (c + 1) * C)
        for hh in range(B_HEADS):
            sl = slice(hh * LANES, (hh + 1) * LANES)
            q = qb_s[rows, sl]
            k = kb_s[rows, sl]
            v = vb_s[rows, sl]
            kd = kbd_s[rows, sl]
            sc = _dot_nt(q, k) * dmat_ref[hh]
            S = S_ref[0, hh]
            ret = _dot(sc.astype(BF16), v) + _dot(q, S.astype(BF16)) * qdec_ref[hh]
            S_ref[0, hh] = gam_ref[hh] * S + _dot_tn(kd, v)
            osl = slice(A_WIDTH + hh * LANES, A_WIDTH + (hh + 1) * LANES)
            mix_s[rows, osl] = (g_s[rows, osl] * _row_rms(ret)).astype(BF16)

    scores(0)
    for c in range(nchunk):
        if c + 1 < nchunk:
            scores(c + 1)
        softmax(c)
        retain(c)
        attend(c)

    y_ref[0] = x + _dot(mix_s[...], wout_ref[...])
    for i in range(2 * A_KV_HEADS):
        kz_s[i, 0:C, :] = kz_s[i, tm:tm + C, :]
        vz_s[i, 0:C, :] = vz_s[i, tm:tm + C, :]


def _const_spec(shape):
    nd = len(shape)
    return pl.BlockSpec(shape, lambda *_: (0,) * nd, pipeline_mode=pl.Buffered(1))


def _layer_spec(shape, l):
    nd = len(shape)
    return pl.BlockSpec((None,) + tuple(shape), lambda *_: (l,) + (0,) * nd, pipeline_mode=pl.Buffered(1))


def _prompt_layer(l, x, win, wout, ng, gq, gk, sinks, gam, rot_a, rot_b, dmat, qdec, kdec):
    N, T, D = x.shape
    tm = PROMPT_BLOCK
    assert T % tm == 0 and tm % RET_CHUNK == 0 and WINDOW == RET_CHUNK
    nt = T // tm
    smem = pl.BlockSpec(memory_space=pltpu.SMEM)
    tab = pl.BlockSpec((tm, LANES), lambda n, t: (t, 0))
    kv_out = pl.BlockSpec((1, WINDOW, LANES), lambda n, t: (n, 0, 0))
    return pl.pallas_call(
        functools.partial(_prompt_kernel, tm=tm),
        grid=(N, nt),
        in_specs=[smem, smem,
                  pl.BlockSpec((1, tm, D), lambda n, t: (n, t, 0)),
                  _layer_spec((D, IN_WIDTH), l), _layer_spec((MIX_WIDTH, D), l),
                  _layer_spec((1, D), l), _layer_spec((1, LANES), l), _layer_spec((1, LANES), l),
                  tab, tab, tab, tab, tab,
                  _const_spec((B_HEADS, RET_CHUNK, RET_CHUNK)), _const_spec((B_HEADS, RET_CHUNK, LANES)),
                  _const_spec((B_HEADS, RET_CHUNK, LANES))],
        out_specs=[pl.BlockSpec((1, tm, D), lambda n, t: (n, t, 0)), kv_out, kv_out,
                   pl.BlockSpec((1, B_HEADS, B_DK, B_DV), lambda n, t: (n, 0, 0, 0))],
        out_shape=[jax.ShapeDtypeStruct((N, T, D), F32),
                   jax.ShapeDtypeStruct((N, WINDOW, LANES), F32),
                   jax.ShapeDtypeStruct((N, WINDOW, LANES), F32),
                   jax.ShapeDtypeStruct((N, B_HEADS, B_DK, B_DV), F32)],
        scratch_shapes=[pltpu.VMEM((tm, A_WIDTH), BF16),
                        pltpu.VMEM((2 * A_KV_HEADS, RET_CHUNK + tm, LANES), BF16),
                        pltpu.VMEM((2 * A_KV_HEADS, RET_CHUNK + tm, LANES), BF16),
                        pltpu.VMEM((tm, B_QK_WIDTH), BF16),
                        pltpu.VMEM((tm, B_QK_WIDTH), BF16),
                        pltpu.VMEM((tm, B_QK_WIDTH), BF16),
                        pltpu.VMEM((tm, B_V_WIDTH), BF16),
                        pltpu.VMEM((tm, MIX_WIDTH), F32),
                        pltpu.VMEM((tm, MIX_WIDTH), BF16),
                        pltpu.VMEM((RET_CHUNK, 2 * RET_CHUNK), F32),
                        pltpu.VMEM((2, A_HEADS, RET_CHUNK, 2 * RET_CHUNK), F32),
                        pltpu.VMEM((2, A_HEADS, RET_CHUNK, 2 * RET_CHUNK), BF16),
                        pltpu.VMEM((2, A_TILES, RET_CHUNK, LANES), F32)],
        compiler_params=pltpu.CompilerParams(dimension_semantics=("arbitrary", "arbitrary"),
                                             vmem_limit_bytes=VMEM_LIMIT),
        name="prompt_layer",
    )(sinks, gam, x, win, wout, ng, gq, gk, *rot_a, *rot_b, dmat, qdec, kdec)


def _sample_kernel(gam_ref, x_ref, win_ref, wout_ref, ng_ref, gq_ref, gk_ref,
                   ca_ref, sa1_ref, sa2_ref, cb_ref, sb_ref, sink_ref, d8_ref, qdec_ref, kdec_ref,
                   ck_ref, cv_ref, st_ref,
                   y_ref, ko_ref, vo_ref, so_ref,
                   qz_s, kn_s, vn_s, qb_s, kb_s, kbd_s, vb_s, g_s, mix_s, *, nseq, dec):
    step = pl.program_id(0)
    ntok = x_ref.shape[0]
    pair_rows = 2 * dec
    npair = nseq // 2

    @pl.when(step == 0)
    def _():
        x = x_ref[...]
        h = (_row_rms(x) * ng_ref[...]).astype(BF16)
        lo_mask = lax.broadcasted_iota(jnp.int32, (ntok, LANES), 1) < A_HEAD_DIM
        ca, sa1, sa2 = ca_ref[...], sa1_ref[...], sa2_ref[...]
        cb, sb = cb_ref[...], sb_ref[...]
        for m in range(A_TILES):
            blk = _dot(h, win_ref[:, OFF_QA + m * LANES:OFF_QA + (m + 1) * LANES])
            qr = _pair_norm_rot(blk, gq_ref[...], ca, sa1, sa2, lo_mask)
            sw = pltpu.roll(qr, A_HEAD_DIM, 1)
            if m < A_TILES // 2:
                qz_s[2 * m] = jnp.where(lo_mask, qr, 0.0)
                qz_s[2 * m + 1] = jnp.where(lo_mask, sw, 0.0)
            else:
                qz_s[2 * m] = jnp.where(lo_mask, 0.0, sw)
                qz_s[2 * m + 1] = jnp.where(lo_mask, 0.0, qr)
        kn_s[...] = _pair_norm_rot(_dot(h, win_ref[:, OFF_KA:OFF_KA + LANES]), gk_ref[...], ca, sa1, sa2, lo_mask)
        vn_s[...] = _dot(h, win_ref[:, OFF_VA:OFF_VA + LANES])
        for m in range(A_TILES):
            g_s[:, m * LANES:(m + 1) * LANES] = _silu(
                _dot(h, win_ref[:, OFF_GA + m * LANES:OFF_GA + (m + 1) * LANES]))
        for hh in range(B_HEADS):
            sl = slice(hh * LANES, (hh + 1) * LANES)
            qb_s[:, sl] = _rot_b(_dot(h, win_ref[:, OFF_QB + hh * LANES:OFF_QB + (hh + 1) * LANES]), cb, sb)
            kb = _rot_b(_dot(h, win_ref[:, OFF_KB + hh * LANES:OFF_KB + (hh + 1) * LANES]), cb, sb) * (B_DK ** -0.5)
            kb_s[:, sl] = kb
            kbd_s[:, sl] = kb * kdec_ref[hh]
            vb_s[:, sl] = _dot(h, win_ref[:, OFF_VB + hh * LANES:OFF_VB + (hh + 1) * LANES])
            g_s[:, A_WIDTH + hh * LANES:A_WIDTH + (hh + 1) * LANES] = _silu(
                _dot(h, win_ref[:, OFF_GB + hh * LANES:OFF_GB + (hh + 1) * LANES]))

    nq = A_HEADS * pair_rows
    row = lax.broadcasted_iota(jnp.int32, (nq, LANES), 0)
    key = lax.broadcasted_iota(jnp.int32, (nq, LANES), 1)
    tok = row % dec
    first_seq = (row % pair_rows) < dec
    cache_mask = key > tok
    rown = lax.broadcasted_iota(jnp.int32, (nq, pair_rows), 0)
    coln = lax.broadcasted_iota(jnp.int32, (nq, pair_rows), 1)
    new_mask = ((coln // dec) == ((rown % pair_rows) // dec)) & ((coln % dec) <= (rown % dec))
    lo8 = lax.broadcasted_iota(jnp.int32, (pair_rows, LANES), 1) < A_HEAD_DIM
    r8 = lax.broadcasted_iota(jnp.int32, (pair_rows, LANES), 0)
    sink = sink_ref[...]

    def pair(p, carry):
        b0 = 2 * p
        base = pl.multiple_of((step * npair + p) * pair_rows, pair_rows)
        rows = pl.ds(base, pair_rows)

        qz = qz_s[:, rows, :].reshape(nq, LANES).astype(BF16)
        kn = kn_s[rows, :]
        vn = vn_s[rows, :]
        k0 = ck_ref[b0]
        k1 = ck_ref[b0 + 1]
        v0 = cv_ref[b0]
        v1 = cv_ref[b0 + 1]
        s_c = jnp.where(first_seq, _dot_nt(qz, k0.astype(BF16)), _dot_nt(qz, k1.astype(BF16)))
        s_c = jnp.where(cache_mask, s_c, NEG)
        s_n = jnp.where(new_mask, _dot_nt(qz, kn.astype(BF16)), NEG)
        p_c, p_n = _softmax_parts([s_c, s_n], sink)
        p_c = p_c.astype(BF16)
        o = jnp.where(first_seq, _dot(p_c, v0.astype(BF16)), _dot(p_c, v1.astype(BF16)))
        o = o + _dot(p_n.astype(BF16), vn.astype(BF16))
        for m in range(A_TILES):
            oa = o[(2 * m) * pair_rows:(2 * m + 1) * pair_rows, :]
            ob = o[(2 * m + 1) * pair_rows:(2 * m + 2) * pair_rows, :]
            if m < A_TILES // 2:
                blk = jnp.where(lo8, oa, pltpu.roll(ob, A_HEAD_DIM, 1))
            else:
                blk = jnp.where(lo8, pltpu.roll(oa, A_HEAD_DIM, 1), ob)
            sl = slice(m * LANES, (m + 1) * LANES)
            mix_s[rows, sl] = g_s[rows, sl] * blk

        for j, (kc, vc) in enumerate(((k0, v0), (k1, v1))):
            ko_ref[b0 + j, 0:WINDOW - dec, :] = kc[dec:, :]
            vo_ref[b0 + j, 0:WINDOW - dec, :] = vc[dec:, :]
            ko_ref[b0 + j, WINDOW - dec:, :] = kn[j * dec:(j + 1) * dec, :]
            vo_ref[b0 + j, WINDOW - dec:, :] = vn[j * dec:(j + 1) * dec, :]

        for hh in range(B_HEADS):
            sl = slice(hh * LANES, (hh + 1) * LANES)
            q = qb_s[rows, sl].astype(BF16)
            k = kb_s[rows, sl].astype(BF16)
            kd = kbd_s[rows, sl]
            v = vb_s[rows, sl].astype(BF16)
            sc = _dot_nt(q, k) * d8_ref[hh]
            inner = _dot(sc.astype(BF16), v)
            crosses = []
            for j in range(2):
                S = st_ref[b0 + j, hh]
                crosses.append(_dot(q, S.astype(BF16)))
                own = (r8 // dec) == j
                kdj = jnp.where(own, kd, 0.0).astype(BF16)
                so_ref[b0 + j, hh] = gam_ref[hh] * S + _dot_tn(kdj, v)
            cross = jnp.where(r8 < dec, crosses[0], crosses[1]) * qdec_ref[hh]
            osl = slice(A_WIDTH + hh * LANES, A_WIDTH + (hh + 1) * LANES)
            mix_s[rows, osl] = g_s[rows, osl] * _row_rms(inner + cross)
        return carry

    lax.fori_loop(0, npair, pair, 0)

    @pl.when(step == pl.num_programs(0) - 1)
    def _():
        y_ref[...] = x_ref[...] + _dot(mix_s[...].astype(BF16), wout_ref[...])


def _sample_layer(l, x, win, wout, ng, gq, gk, sink_col, gam, rot_a, rot_b, d8, qdec, kdec, ck, cv, st, dec):
    ntok, D = x.shape
    nb = ck.shape[0]
    nseq = SAMPLE_SEQS
    assert nb % nseq == 0 and nseq % 2 == 0 and ntok == nb * dec and 2 * dec == 8
    pair_rows = 2 * dec
    nq = A_HEADS * pair_rows
    smem = pl.BlockSpec(memory_space=pltpu.SMEM)
    kv = pl.BlockSpec((nseq, WINDOW, LANES), lambda i: (i, 0, 0))
    stt = pl.BlockSpec((nseq, B_HEADS, B_DK, B_DV), lambda i: (i, 0, 0, 0))
    tab = _const_spec((ntok, LANES))
    return pl.pallas_call(
        functools.partial(_sample_kernel, nseq=nseq, dec=dec),
        grid=(nb // nseq,),
        in_specs=[smem, _const_spec((ntok, D)), _layer_spec((D, IN_WIDTH), l), _layer_spec((MIX_WIDTH, D), l),
                  _layer_spec((1, D), l), _layer_spec((1, LANES), l), _layer_spec((1, LANES), l),
                  tab, tab, tab, tab, tab,
                  _const_spec((nq, 1)), _const_spec((B_HEADS, pair_rows, pair_rows)),
                  _const_spec((B_HEADS, pair_rows, LANES)), _const_spec((B_HEADS, ntok, LANES)),
                  kv, kv, stt],
        out_specs=[pl.BlockSpec((ntok, D), lambda i: (0, 0)), kv, kv, stt],
        out_shape=[jax.ShapeDtypeStruct((ntok, D), F32),
                   jax.ShapeDtypeStruct(ck.shape, F32),
                   jax.ShapeDtypeStruct(cv.shape, F32),
                   jax.ShapeDtypeStruct(st.shape, F32)],
        scratch_shapes=[pltpu.VMEM((A_HEADS, ntok, LANES), F32),
                        pltpu.VMEM((ntok, LANES), F32),
                        pltpu.VMEM((ntok, LANES), F32),
                        pltpu.VMEM((ntok, B_QK_WIDTH), F32),
                        pltpu.VMEM((ntok, B_QK_WIDTH), F32),
                        pltpu.VMEM((ntok, B_QK_WIDTH), F32),
                        pltpu.VMEM((ntok, B_V_WIDTH), F32),
                        pltpu.VMEM((ntok, MIX_WIDTH), F32),
                        pltpu.VMEM((ntok, MIX_WIDTH), F32)],
        compiler_params=pltpu.CompilerParams(dimension_semantics=("arbitrary",),
                                             vmem_limit_bytes=VMEM_LIMIT),
        name="sample_layer",
    )(gam, x, win, wout, ng, gq, gk, *rot_a, *rot_b, sink_col, d8, qdec, kdec, ck, cv, st)


def _rot_tables_a(pos):
    half = A_ROT_DIM // 2
    inv_freq = ROPE_THETA_A ** (-jnp.arange(half, dtype=F32) / half)
    ang = pos.astype(F32)[:, None] * inv_freq[None, :]
    cos, sin = jnp.cos(ang), jnp.sin(ang)
    n = pos.shape[0]
    zh = jnp.zeros((n, half), F32)
    rest = A_HEAD_DIM - A_ROT_DIM
    c = jnp.concatenate([cos, cos, jnp.ones((n, rest), F32)], axis=-1)
    s1 = jnp.concatenate([-sin, zh, jnp.zeros((n, rest), F32)], axis=-1)
    s2 = jnp.concatenate([zh, sin, jnp.zeros((n, rest), F32)], axis=-1)
    return tuple(jnp.tile(a, (1, HEADS_PER_TILE)) for a in (c, s1, s2))


def _rot_tables_b(pos):
    half = B_DK // 2
    inv_freq = ROPE_THETA_B ** (-jnp.arange(half, dtype=F32) / half)
    ang = pos.astype(F32)[:, None] * inv_freq[None, :]
    cos, sin = jnp.cos(ang), jnp.sin(ang)
    return jnp.concatenate([cos, cos], axis=-1), jnp.concatenate([-sin, sin], axis=-1)


def _log_decay():
    return jnp.log(1.0 - 2.0 ** (-5.0 - jnp.arange(B_HEADS, dtype=F32)))


def _decay_tables(C):
    lg = _log_decay()
    idx = jnp.arange(C, dtype=F32)
    diff = idx[:, None] - idx[None, :]
    dmat = jnp.where(diff >= 0, jnp.exp(lg[:, None, None] * jnp.maximum(diff, 0.0)), 0.0)
    q_dec = jnp.exp(lg[None, :] * (idx[:, None] + 1.0))
    k_dec = jnp.exp(lg[None, :] * (C - 1.0 - idx[:, None]))
    gam = jnp.exp(lg * C)
    return dmat, q_dec, k_dec, gam


def kernel(x_prompt, x_sample, cache_swa_k, cache_swa_v, state_ret, w_in, w_out, norm_g, q_norm_g, k_norm_g, sinks):
    N, T, D = x_prompt.shape
    nb, dec, _ = x_sample.shape
    w_buf = cache_swa_k.shape[2]
    assert w_buf == WINDOW

    pos_p = jnp.arange(T, dtype=jnp.int32)
    pos_s = PAST_LEN + jnp.arange(dec, dtype=jnp.int32)
    rot_a_p, rot_b_p = _rot_tables_a(pos_p), _rot_tables_b(pos_p)
    rot_a_s = tuple(jnp.tile(a, (nb, 1)) for a in _rot_tables_a(pos_s))
    rot_b_s = tuple(jnp.tile(a, (nb, 1)) for a in _rot_tables_b(pos_s))

    dmat, q_dec, k_dec, gam_p = _decay_tables(RET_CHUNK)
    qdec_p = jnp.broadcast_to(q_dec.T[:, :, None], (B_HEADS, RET_CHUNK, LANES))
    kdec_p = jnp.broadcast_to(k_dec.T[:, :, None], (B_HEADS, RET_CHUNK, LANES))
    dmat4, q_dec4, k_dec4, gam_s = _decay_tables(dec)
    d8 = jnp.kron(jnp.eye(2, dtype=F32)[None], jnp.ones((1, dec, dec), F32)) * jnp.tile(dmat4, (1, 2, 2))
    qdec_s = jnp.broadcast_to(jnp.tile(q_dec4.T, (1, 2))[:, :, None], (B_HEADS, 2 * dec, LANES))
    kdec_s = jnp.broadcast_to(jnp.tile(k_dec4.T, (1, nb))[:, :, None], (B_HEADS, nb * dec, LANES))

    w_in_b = w_in.astype(BF16)
    w_out_b = w_out.astype(BF16)
    ck = cache_swa_k.reshape(DEPTH, nb, w_buf, LANES)
    cv = cache_swa_v.reshape(DEPTH, nb, w_buf, LANES)

    ng = norm_g[:, None, :]
    gq = jnp.tile(q_norm_g, (1, HEADS_PER_TILE))[:, None, :] * (A_HEAD_DIM ** -0.5 * LOG2E)
    gk = jnp.tile(k_norm_g, (1, HEADS_PER_TILE))[:, None, :]
    sinks2 = sinks * LOG2E

    xp = x_prompt
    xs = x_sample.reshape(nb * dec, D)
    kp_l, vp_l, sp_l, ks_l, vs_l, ss_l = [], [], [], [], [], []
    for l in range(DEPTH):
        xp, kl, vl, S = _prompt_layer(l, xp, w_in_b, w_out_b, ng, gq, gk, sinks2[l], gam_p,
                                      rot_a_p, rot_b_p, dmat, qdec_p, kdec_p)
        kp_l.append(kl)
        vp_l.append(vl)
        sp_l.append(S)
        sink_col = jnp.repeat(sinks2[l], 2 * dec)[:, None]
        xs, ko, vo, so = _sample_layer(l, xs, w_in_b, w_out_b, ng, gq, gk, sink_col, gam_s,
                                       rot_a_s, rot_b_s, d8, qdec_s, kdec_s, ck[l], cv[l], state_ret[l], dec)
        ks_l.append(ko)
        vs_l.append(vo)
        ss_l.append(so)

    kv_p = (DEPTH, N, WINDOW, A_KV_HEADS, A_HEAD_DIM)
    kv_s = (DEPTH, nb, w_buf, A_KV_HEADS, A_HEAD_DIM)
    return (xp, xs.reshape(nb, dec, D),
            jnp.stack(kp_l).reshape(kv_p), jnp.stack(vp_l).reshape(kv_p), jnp.stack(sp_l),
            jnp.stack(ks_l).reshape(kv_s), jnp.stack(vs_l).reshape(kv_s), jnp.stack(ss_l))
```

```python
import functools

import jax
import jax.numpy as jnp
from jax import lax
from jax.experimental import pallas as pl
from jax.experimental.pallas import tpu as pltpu

D_MODEL = 1024
DEPTH = 4
PAST_LEN = 8192
A_HEADS = 8
A_KV_HEADS = 2
A_GROUP = A_HEADS // A_KV_HEADS
A_HEAD_DIM = 64
WINDOW = 128
A_ROT_DIM = A_HEAD_DIM // 4
ROPE_THETA_A = 500000.0
B_HEADS = 4
B_DK = 128
B_DV = 128
RET_CHUNK = 128
ROPE_THETA_B = 10000.0
EPS = 1e-6

A_WIDTH = A_HEADS * A_HEAD_DIM
A_KV_WIDTH = A_KV_HEADS * A_HEAD_DIM
B_QK_WIDTH = B_HEADS * B_DK
B_V_WIDTH = B_HEADS * B_DV
MIX_WIDTH = A_WIDTH + B_V_WIDTH
OFF_QA = 0
OFF_KA = OFF_QA + A_WIDTH
OFF_VA = OFF_KA + A_KV_WIDTH
OFF_GA = OFF_VA + A_KV_WIDTH
OFF_QB = OFF_GA + A_WIDTH
OFF_KB = OFF_QB + B_QK_WIDTH
OFF_VB = OFF_KB + B_QK_WIDTH
OFF_GB = OFF_VB + B_V_WIDTH
IN_WIDTH = OFF_GB + B_V_WIDTH

LANES = 128
HEADS_PER_TILE = LANES // A_HEAD_DIM
A_TILES = A_WIDTH // LANES
NEG = -1e30
LOG2E = 1.4426950408889634

PROMPT_BLOCK = 512
SAMPLE_SEQS = 8
VMEM_LIMIT = 56 * 1024 * 1024

F32 = jnp.float32
BF16 = jnp.bfloat16


def _dot(a, b):
    return jnp.dot(a, b, preferred_element_type=F32)


def _dot_nt(a, b):
    return lax.dot_general(a, b, (((1,), (1,)), ((), ())), preferred_element_type=F32)


def _dot_tn(a, b):
    return lax.dot_general(a, b, (((0,), (0,)), ((), ())), preferred_element_type=F32)


def _silu(x):
    return x * (1.0 / (1.0 + jnp.exp(-x)))


def _row_rms(x):
    n = x.shape[-1]
    return x * lax.rsqrt(jnp.sum(x * x, axis=-1, keepdims=True) * (1.0 / n) + EPS)


def _pair_norm_rot(blk, g_row, ca, sa1, sa2, lo_mask):
    sq = blk * blk
    tot = jnp.sum(sq, axis=-1, keepdims=True)
    lo = jnp.sum(jnp.where(lo_mask, sq, 0.0), axis=-1, keepdims=True)
    hi = tot - lo
    inv_lo = lax.rsqrt(lo * (1.0 / A_HEAD_DIM) + EPS)
    inv_hi = lax.rsqrt(hi * (1.0 / A_HEAD_DIM) + EPS)
    xn = blk * jnp.where(lo_mask, inv_lo, inv_hi) * g_row
    half = A_ROT_DIM // 2
    return xn * ca + pltpu.roll(xn, LANES - half, 1) * sa1 + pltpu.roll(xn, half, 1) * sa2


def _rot_b(blk, cb, sb):
    return blk * cb + pltpu.roll(blk, B_DK // 2, 1) * sb


def _softmax_parts(parts, sink):
    mx = sink
    for s in parts:
        mx = jnp.maximum(mx, jnp.max(s, axis=-1, keepdims=True))
    ps = [jnp.exp2(s - mx) for s in parts]
    den = jnp.exp2(sink - mx)
    for p in ps:
        den = den + jnp.sum(p, axis=-1, keepdims=True)
    rinv = 1.0 / den
    return [p * rinv for p in ps]


def _prompt_kernel(sinks_ref, gam_ref, x_ref, win_ref, wout_ref, ng_ref, gq_ref, gk_ref,
                   ca_ref, sa1_ref, sa2_ref, cb_ref, sb_ref, dmat_ref, qdec_ref, kdec_ref,
                   y_ref, klast_ref, vlast_ref, S_ref,
                   q_s, kz_s, vz_s, qb_s, kb_s, kbd_s, vb_s, g_s, mix_s, bias_s, s_s, p_s, es_s, *, tm):
    n = pl.program_id(0)
    t = pl.program_id(1)
    nchunk = tm // RET_CHUNK
    C = RET_CHUNK
    lo_c = lax.broadcasted_iota(jnp.int32, (C, LANES), 1) < A_HEAD_DIM

    @pl.when(jnp.logical_and(n == 0, t == 0))
    def _():
        qi = lax.broadcasted_iota(jnp.int32, (C, 2 * C), 0)
        kj = lax.broadcasted_iota(jnp.int32, (C, 2 * C), 1)
        bias_s[...] = jnp.where((kj > qi) & (kj <= qi + C), 0.0, NEG)

    @pl.when(t == 0)
    def _():
        S_ref[...] = jnp.zeros_like(S_ref)
        for i in range(2 * A_KV_HEADS):
            kz_s[i, 0:C, :] = jnp.zeros((C, LANES), BF16)
            head_half, ones_half = (0.0, 1.0) if i % 2 == 0 else (1.0, 0.0)
            vz_s[i, 0:C, :] = jnp.where(lo_c, head_half, ones_half).astype(BF16)

    x = x_ref[0]
    h = (_row_rms(x) * ng_ref[...]).astype(BF16)

    lo_mask = lax.broadcasted_iota(jnp.int32, (tm, LANES), 1) < A_HEAD_DIM
    ca, sa1, sa2 = ca_ref[...], sa1_ref[...], sa2_ref[...]
    cb, sb = cb_ref[...], sb_ref[...]

    def tile(a, i):
        return a[:, i * LANES:(i + 1) * LANES]

    qa = _dot(h, win_ref[:, OFF_QA:OFF_QA + A_WIDTH])
    for m in range(A_TILES):
        q_s[:, m * LANES:(m + 1) * LANES] = _pair_norm_rot(tile(qa, m), gq_ref[...], ca, sa1, sa2, lo_mask).astype(BF16)

    kv = _dot(h, win_ref[:, OFF_KA:OFF_KA + 2 * A_KV_WIDTH])
    kr = _pair_norm_rot(tile(kv, 0), gk_ref[...], ca, sa1, sa2, lo_mask)
    va = tile(kv, 1)
    klast_ref[0] = kr[tm - WINDOW:, :]
    vlast_ref[0] = va[tm - WINDOW:, :]
    for src, dst, fill in ((kr, kz_s, 0.0), (va, vz_s, 1.0)):
        sw = pltpu.roll(src, A_HEAD_DIM, 1)
        dst[0, C:, :] = jnp.where(lo_mask, src, fill).astype(BF16)
        dst[1, C:, :] = jnp.where(lo_mask, fill, sw).astype(BF16)
        dst[2, C:, :] = jnp.where(lo_mask, sw, fill).astype(BF16)
        dst[3, C:, :] = jnp.where(lo_mask, fill, src).astype(BF16)

    ga = _dot(h, win_ref[:, OFF_GA:OFF_GA + A_WIDTH])
    for m in range(A_TILES):
        g_s[:, m * LANES:(m + 1) * LANES] = _silu(tile(ga, m))
    qb = _dot(h, win_ref[:, OFF_QB:OFF_QB + B_QK_WIDTH])
    for hh in range(B_HEADS):
        qb_s[:, hh * LANES:(hh + 1) * LANES] = _rot_b(tile(qb, hh), cb, sb).astype(BF16)
    kb = _dot(h, win_ref[:, OFF_KB:OFF_KB + B_QK_WIDTH])
    for hh in range(B_HEADS):
        sl = slice(hh * LANES, (hh + 1) * LANES)
        kbh = _rot_b(tile(kb, hh), cb, sb) * (B_DK ** -0.5)
        kb_s[:, sl] = kbh.astype(BF16)
        kbd_s[:, sl] = (kbh.reshape(nchunk, C, LANES) * kdec_ref[hh][None]).reshape(tm, LANES).astype(BF16)
    vb_s[...] = _dot(h, win_ref[:, OFF_VB:OFF_VB + B_V_WIDTH]).astype(BF16)
    gb = _dot(h, win_ref[:, OFF_GB:OFF_GB + B_V_WIDTH])
    for hh in range(B_HEADS):
        g_s[:, A_WIDTH + hh * LANES:A_WIDTH + (hh + 1) * LANES] = _silu(tile(gb, hh))

    first_bias = jnp.where(t == 0, NEG, 0.0)

    def scores(c):
        buf = c % 2
        rows = slice(c * C, (c + 1) * C)
        win_rows = slice(c * C, (c + 2) * C)
        for m in range(A_TILES):
            g = m // (A_GROUP // HEADS_PER_TILE)
            qp = q_s[rows, m * LANES:(m + 1) * LANES]
            for half in range(HEADS_PER_TILE):
                s = _dot_nt(qp, kz_s[2 * g + half, win_rows, :]) + bias_s[...]
                if c == 0:
                    s = jnp.concatenate([s[:, :C] + first_bias, s[:, C:]], axis=1)
                s_s[buf, m * HEADS_PER_TILE + half] = s

    def softmax(c):
        buf = c % 2
        for m in range(A_TILES):
            es = []
            for half in range(HEADS_PER_TILE):
                hd = m * HEADS_PER_TILE + half
                s = s_s[buf, hd]
                sink = sinks_ref[hd]
                mx = jnp.maximum(jnp.max(s, axis=-1, keepdims=True), sink)
                p_s[buf, hd] = jnp.exp2(s - mx).astype(BF16)
                es.append(jnp.exp2(sink - mx))
            es_s[buf, m] = jnp.where(lo_c, es[0], es[1])

    def attend(c):
        buf = c % 2
        rows = slice(c * C, (c + 1) * C)
        win_rows = slice(c * C, (c + 2) * C)
        for m in range(A_TILES):
            g = m // (A_GROUP // HEADS_PER_TILE)
            sl = slice(m * LANES, (m + 1) * LANES)
            oe = _dot(p_s[buf, 2 * m], vz_s[2 * g, win_rows, :])
            oo = _dot(p_s[buf, 2 * m + 1], vz_s[2 * g + 1, win_rows, :])
            pv = jnp.where(lo_c, oe, oo)
            den = pltpu.roll(jnp.where(lo_c, oo, oe), A_HEAD_DIM, 1) + es_s[buf, m]
            mix_s[rows, sl] = (g_s[rows, sl] * pv * (1.0 / den)).astype(BF16)

    def retain(c):
        rows = slice(c * C, (c + 1) * C)
        for hh in range(B_HEADS):
            sl = slice(hh * LANES, (hh + 1) * LANES)
            q = qb_s[rows, sl]
            k = kb_s[rows, sl]
            v = vb_s[rows, sl]
            kd = kbd_s[rows, sl]
            sc = _dot_nt(q, k) * dmat_ref[hh]
            S = S_ref[0, hh]
            ret = _dot(sc.astype(BF16), v) + _dot(q, S.astype(BF16)) * qdec_ref[hh]
            S_ref[0, hh] = gam_ref[hh] * S + _dot_tn(kd, v)
            osl = slice(A_WIDTH + hh * LANES, A_WIDTH + (hh + 1) * LANES)
            mix_s[rows, osl] = (g_s[rows, osl] * _row_rms(ret)).astype(BF16)

    scores(0)
    for c in range(nchunk):
        if c + 1 < nchunk:
            scores(c + 1)
        softmax(c)
        retain(c)
        attend(c)

    y_ref[0] = x + _dot(mix_s[...], wout_ref[...])
    for i in range(2 * A_KV_HEADS):
        kz_s[i, 0:C, :] = kz_s[i, tm:tm + C, :]
        vz_s[i, 0:C, :] = vz_s[i, tm:tm + C, :]


def _const_spec(shape):
    nd = len(shape)
    return pl.BlockSpec(shape, lambda *_: (0,) * nd, pipeline_mode=pl.Buffered(1))


def _layer_spec(shape, l):
    nd = len(shape)
    return pl.BlockSpec((None,) + tuple(shape), lambda *_: (l,) + (0,) * nd, pipeline_mode=pl.Buffered(1))


def _prompt_layer(l, x, win, wout, ng, gq, gk, sinks, gam, rot_a, rot_b, dmat, qdec, kdec):
    N, T, D = x.shape
    tm = PROMPT_BLOCK
    assert T % tm == 0 and tm % RET_CHUNK == 0 and WINDOW == RET_CHUNK
    nt = T // tm
    smem = pl.BlockSpec(memory_space=pltpu.SMEM)
    tab = pl.BlockSpec((tm, LANES), lambda n, t: (t, 0))
    kv_out = pl.BlockSpec((1, WINDOW, LANES), lambda n, t: (n, 0, 0))
    return pl.pallas_call(
        functools.partial(_prompt_kernel, tm=tm),
        grid=(N, nt),
        in_specs=[smem, smem,
                  pl.BlockSpec((1, tm, D), lambda n, t: (n, t, 0)),
                  _layer_spec((D, IN_WIDTH), l), _layer_spec((MIX_WIDTH, D), l),
                  _layer_spec((1, D), l), _layer_spec((1, LANES), l), _layer_spec((1, LANES), l),
                  tab, tab, tab, tab, tab,
                  _const_spec((B_HEADS, RET_CHUNK, RET_CHUNK)), _const_spec((B_HEADS, RET_CHUNK, LANES)),
                  _const_spec((B_HEADS, RET_CHUNK, LANES))],
        out_specs=[pl.BlockSpec((1, tm, D), lambda n, t: (n, t, 0)), kv_out, kv_out,
                   pl.BlockSpec((1, B_HEADS, B_DK, B_DV), lambda n, t: (n, 0, 0, 0))],
        out_shape=[jax.ShapeDtypeStruct((N, T, D), F32),
                   jax.ShapeDtypeStruct((N, WINDOW, LANES), F32),
                   jax.ShapeDtypeStruct((N, WINDOW, LANES), F32),
                   jax.ShapeDtypeStruct((N, B_HEADS, B_DK, B_DV), F32)],
        scratch_shapes=[pltpu.VMEM((tm, A_WIDTH), BF16),
                        pltpu.VMEM((2 * A_KV_HEADS, RET_CHUNK + tm, LANES), BF16),
                        pltpu.VMEM((2 * A_KV_HEADS, RET_CHUNK + tm, LANES), BF16),
                        pltpu.VMEM((tm, B_QK_WIDTH), BF16),
                        pltpu.VMEM((tm, B_QK_WIDTH), BF16),
                        pltpu.VMEM((tm, B_QK_WIDTH), BF16),
                        pltpu.VMEM((tm, B_V_WIDTH), BF16),
                        pltpu.VMEM((tm, MIX_WIDTH), F32),
                        pltpu.VMEM((tm, MIX_WIDTH), BF16),
                        pltpu.VMEM((RET_CHUNK, 2 * RET_CHUNK), F32),
                        pltpu.VMEM((2, A_HEADS, RET_CHUNK, 2 * RET_CHUNK), F32),
                        pltpu.VMEM((2, A_HEADS, RET_CHUNK, 2 * RET_CHUNK), BF16),
                        pltpu.VMEM((2, A_TILES, RET_CHUNK, LANES), F32)],
        compiler_params=pltpu.CompilerParams(dimension_semantics=("arbitrary", "arbitrary"),
                                             vmem_limit_bytes=VMEM_LIMIT),
        name="prompt_layer",
    )(sinks, gam, x, win, wout, ng, gq, gk, *rot_a, *rot_b, dmat, qdec, kdec)


def _sample_kernel(gam_ref, x_ref, win_ref, wout_ref, ng_ref, gq_ref, gk_ref,
                   ca_ref, sa1_ref, sa2_ref, cb_ref, sb_ref, sink_ref, d8_ref, qdec_ref, kdec_ref,
                   ck_ref, cv_ref, st_ref,
                   y_ref, ko_ref, vo_ref, so_ref,
                   xs_s, qz_s, kn_s, vn_s, knt_s, vnt_s, qb_s, kb_s, vb_s, g_s, mix_s, *, nseq, dec):
    layer = pl.program_id(0)
    step = pl.program_id(1)
    ntok = x_ref.shape[0]
    pair_rows = 2 * dec
    npair = nseq // 2

    @pl.when(jnp.logical_and(layer == 0, step == 0))
    def _():
        xs_s[...] = x_ref[...]

    @pl.when(step == 0)
    def _():
        h = (_row_rms(xs_s[...]) * ng_ref[...]).astype(BF16)
        lo_mask = lax.broadcasted_iota(jnp.int32, (ntok, LANES), 1) < A_HEAD_DIM
        ca, sa1, sa2 = ca_ref[...], sa1_ref[...], sa2_ref[...]
        cb, sb = cb_ref[...], sb_ref[...]

        def tile(a, i):
            return a[:, i * LANES:(i + 1) * LANES]

        qa = _dot(h, win_ref[:, OFF_QA:OFF_QA + A_WIDTH])
        for m in range(A_TILES):
            qr = _pair_norm_rot(tile(qa, m), gq_ref[...], ca, sa1, sa2, lo_mask)
            sw = pltpu.roll(qr, A_HEAD_DIM, 1)
            if m < A_TILES // 2:
                qz_s[2 * m] = jnp.where(lo_mask, qr, 0.0)
                qz_s[2 * m + 1] = jnp.where(lo_mask, sw, 0.0)
            else:
                qz_s[2 * m] = jnp.where(lo_mask, 0.0, sw)
                qz_s[2 * m + 1] = jnp.where(lo_mask, 0.0, qr)
        kv = _dot(h, win_ref[:, OFF_KA:OFF_KA + 2 * A_KV_WIDTH])
        kn = _pair_norm_rot(tile(kv, 0), gk_ref[...], ca, sa1, sa2, lo_mask)
        vn = tile(kv, 1)
        kn_s[...] = kn
        vn_s[...] = vn
        for tt in range(ntok // LANES):
            knt_s[tt] = kn[tt * LANES:(tt + 1) * LANES, :].T
            vnt_s[tt] = vn[tt * LANES:(tt + 1) * LANES, :].T
        ga = _dot(h, win_ref[:, OFF_GA:OFF_GA + A_WIDTH])
        for m in range(A_TILES):
            g_s[:, m * LANES:(m + 1) * LANES] = _silu(tile(ga, m))
        qb = _dot(h, win_ref[:, OFF_QB:OFF_QB + B_QK_WIDTH])
        for hh in range(B_HEADS):
            qb_s[:, hh * LANES:(hh + 1) * LANES] = _rot_b(tile(qb, hh), cb, sb)
        kb = _dot(h, win_ref[:, OFF_KB:OFF_KB + B_QK_WIDTH])
        for hh in range(B_HEADS):
            kb_s[:, hh * LANES:(hh + 1) * LANES] = _rot_b(tile(kb, hh), cb, sb) * (B_DK ** -0.5)
        vb_s[...] = _dot(h, win_ref[:, OFF_VB:OFF_VB + B_V_WIDTH])
        gb = _dot(h, win_ref[:, OFF_GB:OFF_GB + B_V_WIDTH])
        for hh in range(B_HEADS):
            g_s[:, A_WIDTH + hh * LANES:A_WIDTH + (hh + 1) * LANES] = _silu(tile(gb, hh))

    nq = A_HEADS * pair_rows
    row = lax.broadcasted_iota(jnp.int32, (nq, LANES), 0)
    slot = lax.broadcasted_iota(jnp.int32, (nq, LANES), 1)
    first_seq = (row % pair_rows) < dec
    cache_mask = slot > (row % dec)
    rown = lax.broadcasted_iota(jnp.int32, (nq, pair_rows), 0)
    coln = lax.broadcasted_iota(jnp.int32, (nq, pair_rows), 1)
    new_mask = ((coln // dec) == ((rown % pair_rows) // dec)) & ((coln % dec) <= (rown % dec))
    lo8 = lax.broadcasted_iota(jnp.int32, (pair_rows, LANES), 1) < A_HEAD_DIM
    r8 = lax.broadcasted_iota(jnp.int32, (pair_rows, LANES), 0)
    keep = lax.broadcasted_iota(jnp.int32, (LANES, WINDOW), 1) < WINDOW - dec
    sink = sink_ref[...]

    for p in range(npair):
        b0 = 2 * p
        seq0 = step * nseq + b0
        rows = pl.ds(pl.multiple_of(seq0 * dec, pair_rows), pair_rows)

        qz = qz_s[:, rows, :].reshape(nq, LANES).astype(BF16)
        kn = kn_s[rows, :]
        vn = vn_s[rows, :]
        kt = [ck_ref[b0], ck_ref[b0 + 1]]
        vt = [cv_ref[b0], cv_ref[b0 + 1]]

        s_c = jnp.where(first_seq, _dot(qz, kt[0].astype(BF16)), _dot(qz, kt[1].astype(BF16)))
        s_n = _dot_nt(qz, kn.astype(BF16))
        rets = []
        for hh in range(B_HEADS):
            sl = slice(hh * LANES, (hh + 1) * LANES)
            q = qb_s[rows, sl].astype(BF16)
            kf = kb_s[rows, sl]
            k = kf.astype(BF16)
            kd = kf * kdec_ref[hh]
            v = vb_s[rows, sl].astype(BF16)
            sc = (_dot_nt(q, k) * d8_ref[hh]).astype(BF16)
            crosses = []
            for j in range(2):
                S = st_ref[b0 + j, hh]
                crosses.append(_dot(q, S.astype(BF16)))
                kdj = jnp.where((r8 // dec) == j, kd, 0.0).astype(BF16)
                so_ref[b0 + j, hh] = gam_ref[hh] * S + _dot_tn(kdj, v)
            cross = jnp.where(r8 < dec, crosses[0], crosses[1]) * qdec_ref[hh]
            rets.append((sc, v, cross))

        s_c = jnp.where(cache_mask, s_c, NEG)
        s_n = jnp.where(new_mask, s_n, NEG)
        p_c, p_n = _softmax_parts([s_c, s_n], sink)
        p_c = p_c.astype(BF16)

        o = jnp.where(first_seq, _dot_nt(p_c, vt[0].astype(BF16)), _dot_nt(p_c, vt[1].astype(BF16)))
        o = o + _dot(p_n.astype(BF16), vn.astype(BF16))
        for m in range(A_TILES):
            oa = o[(2 * m) * pair_rows:(2 * m + 1) * pair_rows, :]
            ob = o[(2 * m + 1) * pair_rows:(2 * m + 2) * pair_rows, :]
            if m < A_TILES // 2:
                blk = jnp.where(lo8, oa, pltpu.roll(ob, A_HEAD_DIM, 1))
            else:
                blk = jnp.where(lo8, pltpu.roll(oa, A_HEAD_DIM, 1), ob)
            sl = slice(m * LANES, (m + 1) * LANES)
            mix_s[rows, sl] = g_s[rows, sl] * blk
        for hh, (sc, v, cross) in enumerate(rets):
            osl = slice(A_WIDTH + hh * LANES, A_WIDTH + (hh + 1) * LANES)
            mix_s[rows, osl] = g_s[rows, osl] * _row_rms(_dot(sc, v) + cross)

        for j in range(2):
            tok0 = (seq0 + j) * dec
            tt = tok0 // LANES
            shift = (WINDOW - dec) - tok0 % LANES
            ko_ref[b0 + j] = jnp.where(keep, pltpu.roll(kt[j], WINDOW - dec, 1), pltpu.roll(knt_s[tt], shift, 1))
            vo_ref[b0 + j] = jnp.where(keep, pltpu.roll(vt[j], WINDOW - dec, 1), pltpu.roll(vnt_s[tt], shift, 1))

    @pl.when(step == pl.num_programs(1) - 1)
    def _():
        xs_s[...] = xs_s[...] + _dot(mix_s[...].astype(BF16), wout_ref[...])

    @pl.when(jnp.logical_and(layer == pl.num_programs(0) - 1, step == pl.num_programs(1) - 1))
    def _():
        y_ref[...] = xs_s[...]


def _stacked_spec(shape):
    nd = len(shape)
    return pl.BlockSpec((None,) + tuple(shape), lambda l, i: (l,) + (0,) * nd)


def _sample_layers(x, win, wout, ng, gq, gk, sink_col, gam, rot_a, rot_b, d8, qdec, kdec, ck, cv, st, dec):
    ntok, D = x.shape
    depth, nb = ck.shape[:2]
    nseq = SAMPLE_SEQS
    assert nb % nseq == 0 and nseq % 2 == 0 and ntok == nb * dec and 2 * dec == 8 and ntok % LANES == 0
    assert LANES % dec == 0 and WINDOW == LANES
    pair_rows = 2 * dec
    nq = A_HEADS * pair_rows
    smem = pl.BlockSpec(memory_space=pltpu.SMEM)
    kv = pl.BlockSpec((None, nseq, A_KV_WIDTH, WINDOW), lambda l, i: (l, i, 0, 0))
    stt = pl.BlockSpec((None, nseq, B_HEADS, B_DK, B_DV), lambda l, i: (l, i, 0, 0, 0))
    tab = _const_spec((ntok, LANES))
    return pl.pallas_call(
        functools.partial(_sample_kernel, nseq=nseq, dec=dec),
        grid=(depth, nb // nseq),
        in_specs=[smem, _const_spec((ntok, D)), _stacked_spec((D, IN_WIDTH)), _stacked_spec((MIX_WIDTH, D)),
                  _stacked_spec((1, D)), _stacked_spec((1, LANES)), _stacked_spec((1, LANES)),
                  tab, tab, tab, tab, tab,
                  _stacked_spec((nq, 1)), _const_spec((B_HEADS, pair_rows, pair_rows)),
                  _const_spec((B_HEADS, pair_rows, LANES)), _const_spec((B_HEADS, pair_rows, LANES)),
                  kv, kv, stt],
        out_specs=[pl.BlockSpec((ntok, D), lambda l, i: (0, 0)), kv, kv, stt],
        out_shape=[jax.ShapeDtypeStruct((ntok, D), F32),
                   jax.ShapeDtypeStruct(ck.shape, F32),
                   jax.ShapeDtypeStruct(cv.shape, F32),
                   jax.ShapeDtypeStruct(st.shape, F32)],
        scratch_shapes=[pltpu.VMEM((ntok, D), F32),
                        pltpu.VMEM((A_HEADS, ntok, LANES), F32),
                        pltpu.VMEM((ntok, LANES), F32),
                        pltpu.VMEM((ntok, LANES), F32),
                        pltpu.VMEM((ntok // LANES, A_KV_WIDTH, LANES), F32),
                        pltpu.VMEM((ntok // LANES, A_KV_WIDTH, LANES), F32),
                        pltpu.VMEM((ntok, B_QK_WIDTH), F32),
                        pltpu.VMEM((ntok, B_QK_WIDTH), F32),
                        pltpu.VMEM((ntok, B_V_WIDTH), F32),
                        pltpu.VMEM((ntok, MIX_WIDTH), F32),
                        pltpu.VMEM((ntok, MIX_WIDTH), F32)],
        compiler_params=pltpu.CompilerParams(dimension_semantics=("arbitrary", "arbitrary"),
                                             vmem_limit_bytes=VMEM_LIMIT),
        name="sample_layers",
    )(gam, x, win, wout, ng, gq, gk, *rot_a, *rot_b, sink_col, d8, qdec, kdec, ck, cv, st)


def _rot_tables_a(pos):
    half = A_ROT_DIM // 2
    inv_freq = ROPE_THETA_A ** (-jnp.arange(half, dtype=F32) / half)
    ang = pos.astype(F32)[:, None] * inv_freq[None, :]
    cos, sin = jnp.cos(ang), jnp.sin(ang)
    n = pos.shape[0]
    zh = jnp.zeros((n, half), F32)
    rest = A_HEAD_DIM - A_ROT_DIM
    c = jnp.concatenate([cos, cos, jnp.ones((n, rest), F32)], axis=-1)
    s1 = jnp.concatenate([-sin, zh, jnp.zeros((n, rest), F32)], axis=-1)
    s2 = jnp.concatenate([zh, sin, jnp.zeros((n, rest), F32)], axis=-1)
    return tuple(jnp.tile(a, (1, HEADS_PER_TILE)) for a in (c, s1, s2))


def _rot_tables_b(pos):
    half = B_DK // 2
    inv_freq = ROPE_THETA_B ** (-jnp.arange(half, dtype=F32) / half)
    ang = pos.astype(F32)[:, None] * inv_freq[None, :]
    cos, sin = jnp.cos(ang), jnp.sin(ang)
    return jnp.concatenate([cos, cos], axis=-1), jnp.concatenate([-sin, sin], axis=-1)


def _log_decay():
    return jnp.log(1.0 - 2.0 ** (-5.0 - jnp.arange(B_HEADS, dtype=F32)))


def _decay_tables(C):
    lg = _log_decay()
    idx = jnp.arange(C, dtype=F32)
    diff = idx[:, None] - idx[None, :]
    dmat = jnp.where(diff >= 0, jnp.exp(lg[:, None, None] * jnp.maximum(diff, 0.0)), 0.0)
    q_dec = jnp.exp(lg[None, :] * (idx[:, None] + 1.0))
    k_dec = jnp.exp(lg[None, :] * (C - 1.0 - idx[:, None]))
    gam = jnp.exp(lg * C)
    return dmat, q_dec, k_dec, gam


def kernel(x_prompt, x_sample, cache_swa_k, cache_swa_v, state_ret, w_in, w_out, norm_g, q_norm_g, k_norm_g, sinks):
    N, T, D = x_prompt.shape
    nb, dec, _ = x_sample.shape
    w_buf = cache_swa_k.shape[2]
    assert w_buf == WINDOW

    pos_p = jnp.arange(T, dtype=jnp.int32)
    pos_s = PAST_LEN + jnp.arange(dec, dtype=jnp.int32)
    rot_a_p, rot_b_p = _rot_tables_a(pos_p), _rot_tables_b(pos_p)
    rot_a_s = tuple(jnp.tile(a, (nb, 1)) for a in _rot_tables_a(pos_s))
    rot_b_s = tuple(jnp.tile(a, (nb, 1)) for a in _rot_tables_b(pos_s))

    dmat, q_dec, k_dec, gam_p = _decay_tables(RET_CHUNK)
    qdec_p = jnp.broadcast_to(q_dec.T[:, :, None], (B_HEADS, RET_CHUNK, LANES))
    kdec_p = jnp.broadcast_to(k_dec.T[:, :, None], (B_HEADS, RET_CHUNK, LANES))
    dmat4, q_dec4, k_dec4, gam_s = _decay_tables(dec)
    d8 = jnp.kron(jnp.eye(2, dtype=F32)[None], jnp.ones((1, dec, dec), F32)) * jnp.tile(dmat4, (1, 2, 2))
    qdec_s = jnp.broadcast_to(jnp.tile(q_dec4.T, (1, 2))[:, :, None], (B_HEADS, 2 * dec, LANES))
    kdec_s = jnp.broadcast_to(jnp.tile(k_dec4.T, (1, 2))[:, :, None], (B_HEADS, 2 * dec, LANES))

    w_in_b = w_in.astype(BF16)
    w_out_b = w_out.astype(BF16)
    ck = cache_swa_k.transpose(0, 1, 3, 4, 2).reshape(DEPTH, nb, A_KV_WIDTH, w_buf)
    cv = cache_swa_v.transpose(0, 1, 3, 4, 2).reshape(DEPTH, nb, A_KV_WIDTH, w_buf)

    ng = norm_g[:, None, :]
    gq = jnp.tile(q_norm_g, (1, HEADS_PER_TILE))[:, None, :] * (A_HEAD_DIM ** -0.5 * LOG2E)
    gk = jnp.tile(k_norm_g, (1, HEADS_PER_TILE))[:, None, :]
    sinks2 = sinks * LOG2E

    sink_col = jnp.repeat(sinks2, 2 * dec, axis=1)[:, :, None]
    xs, ks, vs, ss = _sample_layers(x_sample.reshape(nb * dec, D), w_in_b, w_out_b, ng, gq, gk, sink_col, gam_s,
                                    rot_a_s, rot_b_s, d8, qdec_s, kdec_s, ck, cv, state_ret, dec)

    xp = x_prompt
    kp_l, vp_l, sp_l = [], [], []
    for l in range(DEPTH):
        xp, kl, vl, S = _prompt_layer(l, xp, w_in_b, w_out_b, ng, gq, gk, sinks2[l], gam_p,
                                      rot_a_p, rot_b_p, dmat, qdec_p, kdec_p)
        kp_l.append(kl)
        vp_l.append(vl)
        sp_l.append(S)

    kv_p = (DEPTH, N, WINDOW, A_KV_HEADS, A_HEAD_DIM)
    kv_t = (DEPTH, nb, A_KV_HEADS, A_HEAD_DIM, w_buf)
    return (xp, xs.reshape(nb, dec, D),
            jnp.stack(kp_l).reshape(kv_p), jnp.stack(vp_l).reshape(kv_p), jnp.stack(sp_l),
            ks.reshape(kv_t).transpose(0, 1, 4, 2, 3), vs.reshape(kv_t).transpose(0, 1, 4, 2, 3), ss)
```

```python
import functools

import jax
import jax.numpy as jnp
from jax import lax
from jax.experimental import pallas as pl
from jax.experimental.pallas import tpu as pltpu

D_MODEL = 1024
DEPTH = 4
PAST_LEN = 8192
A_HEADS = 8
A_KV_HEADS = 2
A_GROUP = A_HEADS // A_KV_HEADS
A_HEAD_DIM = 64
WINDOW = 128
A_ROT_DIM = A_HEAD_DIM // 4
ROPE_THETA_A = 500000.0
B_HEADS = 4
B_DK = 128
B_DV = 128
RET_CHUNK = 128
ROPE_THETA_B = 10000.0
EPS = 1e-6

A_WIDTH = A_HEADS * A_HEAD_DIM
A_KV_WIDTH = A_KV_HEADS * A_HEAD_DIM
B_QK_WIDTH = B_HEADS * B_DK
B_V_WIDTH = B_HEADS * B_DV
MIX_WIDTH = A_WIDTH + B_V_WIDTH
OFF_QA = 0
OFF_KA = OFF_QA + A_WIDTH
OFF_VA = OFF_KA + A_KV_WIDTH
OFF_GA = OFF_VA + A_KV_WIDTH
OFF_QB = OFF_GA + A_WIDTH
OFF_KB = OFF_QB + B_QK_WIDTH
OFF_VB = OFF_KB + B_QK_WIDTH
OFF_GB = OFF_VB + B_V_WIDTH
IN_WIDTH = OFF_GB + B_V_WIDTH

LANES = 128
HEADS_PER_TILE = LANES // A_HEAD_DIM
A_TILES = A_WIDTH // LANES
NEG = -1e30
LOG2E = 1.4426950408889634

PROMPT_BLOCK = 512
SAMPLE_SEQS = 8
VMEM_LIMIT = 56 * 1024 * 1024

F32 = jnp.float32
BF16 = jnp.bfloat16


def _dot(a, b):
    return jnp.dot(a, b, preferred_element_type=F32)


def _dot_nt(a, b):
    return lax.dot_general(a, b, (((1,), (1,)), ((), ())), preferred_element_type=F32)


def _dot_tn(a, b):
    return lax.dot_general(a, b, (((0,), (0,)), ((), ())), preferred_element_type=F32)


def _silu(x):
    return x * (1.0 / (1.0 + jnp.exp(-x)))


def _row_rms(x):
    n = x.shape[-1]
    return x * lax.rsqrt(jnp.sum(x * x, axis=-1, keepdims=True) * (1.0 / n) + EPS)


def _pair_norm_rot(blk, g_row, ca, sa1, sa2, lo_mask):
    sq = blk * blk
    tot = jnp.sum(sq, axis=-1, keepdims=True)
    lo = jnp.sum(jnp.where(lo_mask, sq, 0.0), axis=-1, keepdims=True)
    hi = tot - lo
    inv_lo = lax.rsqrt(lo * (1.0 / A_HEAD_DIM) + EPS)
    inv_hi = lax.rsqrt(hi * (1.0 / A_HEAD_DIM) + EPS)
    xn = blk * jnp.where(lo_mask, inv_lo, inv_hi) * g_row
    half = A_ROT_DIM // 2
    return xn * ca + pltpu.roll(xn, LANES - half, 1) * sa1 + pltpu.roll(xn, half, 1) * sa2


def _rot_b(blk, cb, sb):
    return blk * cb + pltpu.roll(blk, B_DK // 2, 1) * sb


def _softmax_parts(parts, sink):
    mx = sink
    for s in parts:
        mx = jnp.maximum(mx, jnp.max(s, axis=-1, keepdims=True))
    ps = [jnp.exp2(s - mx) for s in parts]
    den = jnp.exp2(sink - mx)
    for p in ps:
        den = den + jnp.sum(p, axis=-1, keepdims=True)
    rinv = 1.0 / den
    return [p * rinv for p in ps]


def _prompt_kernel(sinks_ref, gam_ref, x_ref, xres_ref, win_ref, wout_ref, ng_ref, gq_ref, gk_ref,
                   ca_ref, sa1_ref, sa2_ref, cb_ref, sb_ref, dmat_ref, qdec_ref, kdec_ref,
                   y_ref, klast_ref, vlast_ref, S_ref,
                   q_s, kz_s, vz_s, qb_s, kb_s, kbd_s, vb_s, g_s, mix_s, bias_s, s_s, p_s, es_s,
                   sc_s, kv_s, sb_s, *, tm, nt):
    g = pl.program_id(0)
    nchunk = tm // RET_CHUNK
    C = RET_CHUNK
    lo_c = lax.broadcasted_iota(jnp.int32, (C, LANES), 1) < A_HEAD_DIM
    t_fin = jnp.maximum(g - 1, 0) % nt

    @pl.when(g == 0)
    def _():
        qi = lax.broadcasted_iota(jnp.int32, (C, 2 * C), 0)
        kj = lax.broadcasted_iota(jnp.int32, (C, 2 * C), 1)
        bias_s[...] = jnp.where((kj > qi) & (kj <= qi + C), 0.0, NEG)
        for ref in (q_s, kz_s, vz_s, qb_s, kb_s, kbd_s, vb_s, g_s):
            ref[1] = jnp.zeros(ref.shape[1:], ref.dtype)

    @pl.when(t_fin == 0)
    def _():
        S_ref[...] = jnp.zeros_like(S_ref)

    def tile(a, i):
        return a[:, i * LANES:(i + 1) * LANES]

    def body(w, r):
        lo_mask = lax.broadcasted_iota(jnp.int32, (tm, LANES), 1) < A_HEAD_DIM
        ca, sa1, sa2 = ca_ref[...], sa1_ref[...], sa2_ref[...]
        cb, sb = cb_ref[...], sb_ref[...]
        first_bias = jnp.where(t_fin == 0, NEG, 0.0)

        def chunk_rows(c):
            return slice(c * C, (c + 1) * C)

        def window_rows(c):
            return slice(c * C, (c + 2) * C)

        def scores(c):
            for m in range(A_TILES):
                grp = m // (A_GROUP // HEADS_PER_TILE)
                qp = q_s[r, chunk_rows(c), m * LANES:(m + 1) * LANES]
                for half in range(HEADS_PER_TILE):
                    s = _dot_nt(qp, kz_s[r, 2 * grp + half, window_rows(c), :]) + bias_s[...]
                    if c == 0:
                        s = jnp.concatenate([s[:, :C] + first_bias, s[:, C:]], axis=1)
                    s_s[c * A_HEADS + m * HEADS_PER_TILE + half] = s

        def softmax(c):
            for m in range(A_TILES):
                es = []
                for half in range(HEADS_PER_TILE):
                    hd = m * HEADS_PER_TILE + half
                    s = s_s[c * A_HEADS + hd]
                    sink = sinks_ref[hd]
                    mx = jnp.maximum(jnp.max(s, axis=-1, keepdims=True), sink)
                    p_s[c * A_HEADS + hd] = jnp.exp2(s - mx).astype(BF16)
                    es.append(jnp.exp2(sink - mx))
                es_s[c * A_TILES + m] = jnp.where(lo_c, es[0], es[1])

        def attend(c):
            for m in range(A_TILES):
                grp = m // (A_GROUP // HEADS_PER_TILE)
                sl = slice(m * LANES, (m + 1) * LANES)
                oe = _dot(p_s[c * A_HEADS + 2 * m], vz_s[r, 2 * grp, window_rows(c), :])
                oo = _dot(p_s[c * A_HEADS + 2 * m + 1], vz_s[r, 2 * grp + 1, window_rows(c), :])
                pv = jnp.where(lo_c, oe, oo)
                den = pltpu.roll(jnp.where(lo_c, oo, oe), A_HEAD_DIM, 1) + es_s[c * A_TILES + m]
                mix_s[chunk_rows(c), sl] = (g_s[r, chunk_rows(c), sl] * pv * (1.0 / den)).astype(BF16)

        def retain_local(c):
            for hh in range(B_HEADS):
                sl = slice(hh * LANES, (hh + 1) * LANES)
                rows = chunk_rows(c)
                sc_s[c * B_HEADS + hh] = (_dot_nt(qb_s[r, rows, sl], kb_s[r, rows, sl]) * dmat_ref[hh]).astype(BF16)
                kv_s[c * B_HEADS + hh] = _dot_tn(kbd_s[r, rows, sl], vb_s[r, rows, sl])

        def retain_states():
            for hh in range(B_HEADS):
                S = S_ref[0, hh]
                for c in range(nchunk):
                    sb_s[c * B_HEADS + hh] = S.astype(BF16)
                    S = gam_ref[hh] * S + kv_s[c * B_HEADS + hh]
                S_ref[0, hh] = S

        def retain_out(c):
            for hh in range(B_HEADS):
                sl = slice(hh * LANES, (hh + 1) * LANES)
                rows = chunk_rows(c)
                i = c * B_HEADS + hh
                ret = _dot(sc_s[i], vb_s[r, rows, sl]) + _dot(qb_s[r, rows, sl], sb_s[i]) * qdec_ref[hh]
                osl = slice(A_WIDTH + hh * LANES, A_WIDTH + (hh + 1) * LANES)
                mix_s[rows, osl] = (g_s[r, rows, osl] * _row_rms(ret)).astype(BF16)

        def project_attention():
            qa = _dot(h, win_ref[:, OFF_QA:OFF_QA + A_WIDTH])
            for m in range(A_TILES):
                q_s[w, :, m * LANES:(m + 1) * LANES] = _pair_norm_rot(
                    tile(qa, m), gq_ref[...], ca, sa1, sa2, lo_mask).astype(BF16)
            kv = _dot(h, win_ref[:, OFF_KA:OFF_KA + 2 * A_KV_WIDTH])
            kr = _pair_norm_rot(tile(kv, 0), gk_ref[...], ca, sa1, sa2, lo_mask)
            va = tile(kv, 1)
            klast_ref[0] = kr[tm - WINDOW:, :]
            vlast_ref[0] = va[tm - WINDOW:, :]
            for src, dst, fill in ((kr, kz_s, 0.0), (va, vz_s, 1.0)):
                sw = pltpu.roll(src, A_HEAD_DIM, 1)
                dst[w, 0, C:, :] = jnp.where(lo_mask, src, fill).astype(BF16)
                dst[w, 1, C:, :] = jnp.where(lo_mask, fill, sw).astype(BF16)
                dst[w, 2, C:, :] = jnp.where(lo_mask, sw, fill).astype(BF16)
                dst[w, 3, C:, :] = jnp.where(lo_mask, fill, src).astype(BF16)
                for i in range(2 * A_KV_HEADS):
                    dst[w, i, 0:C, :] = dst[r, i, tm:tm + C, :]

        def project_gate_a():
            ga = _dot(h, win_ref[:, OFF_GA:OFF_GA + A_WIDTH])
            for m in range(A_TILES):
                g_s[w, :, m * LANES:(m + 1) * LANES] = _silu(tile(ga, m))

        def project_qb():
            qb = _dot(h, win_ref[:, OFF_QB:OFF_QB + B_QK_WIDTH])
            for hh in range(B_HEADS):
                qb_s[w, :, hh * LANES:(hh + 1) * LANES] = _rot_b(tile(qb, hh), cb, sb).astype(BF16)

        def project_kb():
            kb = _dot(h, win_ref[:, OFF_KB:OFF_KB + B_QK_WIDTH])
            for hh in range(B_HEADS):
                sl = slice(hh * LANES, (hh + 1) * LANES)
                kbh = _rot_b(tile(kb, hh), cb, sb) * (B_DK ** -0.5)
                kb_s[w, :, sl] = kbh.astype(BF16)
                kbd_s[w, :, sl] = (kbh.reshape(nchunk, C, LANES) * kdec_ref[hh][None]).reshape(tm, LANES).astype(BF16)

        def project_vb_gate_b():
            vb_s[w] = _dot(h, win_ref[:, OFF_VB:OFF_VB + B_V_WIDTH]).astype(BF16)
            gb = _dot(h, win_ref[:, OFF_GB:OFF_GB + B_V_WIDTH])
            for hh in range(B_HEADS):
                g_s[w, :, A_WIDTH + hh * LANES:A_WIDTH + (hh + 1) * LANES] = _silu(tile(gb, hh))

        h = (_row_rms(x_ref[0]) * ng_ref[...]).astype(BF16)
        for c in range(nchunk):
            scores(c)
            retain_local(c)
        project_attention()
        for c in range(nchunk):
            softmax(c)
        retain_states()
        project_gate_a()
        project_qb()
        for c in range(nchunk):
            attend(c)
            retain_out(c)
        project_kb()
        project_vb_gate_b()
        y_ref[0] = xres_ref[0] + _dot(mix_s[...], wout_ref[...])

    @pl.when(g % 2 == 0)
    def _():
        body(0, 1)

    @pl.when(g % 2 == 1)
    def _():
        body(1, 0)


def _const_spec(shape):
    nd = len(shape)
    return pl.BlockSpec(shape, lambda *_: (0,) * nd, pipeline_mode=pl.Buffered(1))


def _layer_spec(shape, l):
    nd = len(shape)
    return pl.BlockSpec((None,) + tuple(shape), lambda *_: (l,) + (0,) * nd, pipeline_mode=pl.Buffered(1))


def _prompt_layer(l, x, win, wout, ng, gq, gk, sinks, gam, rot_a, rot_b, dmat, qdec, kdec):
    N, T, D = x.shape
    tm = PROMPT_BLOCK
    assert T % tm == 0 and tm % RET_CHUNK == 0 and WINDOW == RET_CHUNK
    nt = T // tm
    nblk = N * nt
    nchunk = tm // RET_CHUNK

    def proj_blk(g):
        return jnp.minimum(g, nblk - 1)

    def fin_blk(g):
        return jnp.maximum(g - 1, 0)

    smem = pl.BlockSpec(memory_space=pltpu.SMEM)
    tab = pl.BlockSpec((tm, LANES), lambda g: (proj_blk(g) % nt, 0))
    x_proj = pl.BlockSpec((1, tm, D), lambda g: (proj_blk(g) // nt, proj_blk(g) % nt, 0))
    x_fin = pl.BlockSpec((1, tm, D), lambda g: (fin_blk(g) // nt, fin_blk(g) % nt, 0))
    kv_out = pl.BlockSpec((1, WINDOW, LANES), lambda g: (proj_blk(g) // nt, 0, 0))

    def two(*shape, dtype):
        return pltpu.VMEM((2,) + shape, dtype)

    return pl.pallas_call(
        functools.partial(_prompt_kernel, tm=tm, nt=nt),
        grid=(nblk + 1,),
        in_specs=[smem, smem, x_proj, x_fin,
                  _layer_spec((D, IN_WIDTH), l), _layer_spec((MIX_WIDTH, D), l),
                  _layer_spec((1, D), l), _layer_spec((1, LANES), l), _layer_spec((1, LANES), l),
                  tab, tab, tab, tab, tab,
                  _const_spec((B_HEADS, RET_CHUNK, RET_CHUNK)), _const_spec((B_HEADS, RET_CHUNK, LANES)),
                  _const_spec((B_HEADS, RET_CHUNK, LANES))],
        out_specs=[x_fin, kv_out, kv_out,
                   pl.BlockSpec((1, B_HEADS, B_DK, B_DV), lambda g: (fin_blk(g) // nt, 0, 0, 0))],
        out_shape=[jax.ShapeDtypeStruct((N, T, D), F32),
                   jax.ShapeDtypeStruct((N, WINDOW, LANES), F32),
                   jax.ShapeDtypeStruct((N, WINDOW, LANES), F32),
                   jax.ShapeDtypeStruct((N, B_HEADS, B_DK, B_DV), F32)],
        scratch_shapes=[two(tm, A_WIDTH, dtype=BF16),
                        two(2 * A_KV_HEADS, RET_CHUNK + tm, LANES, dtype=BF16),
                        two(2 * A_KV_HEADS, RET_CHUNK + tm, LANES, dtype=BF16),
                        two(tm, B_QK_WIDTH, dtype=BF16),
                        two(tm, B_QK_WIDTH, dtype=BF16),
                        two(tm, B_QK_WIDTH, dtype=BF16),
                        two(tm, B_V_WIDTH, dtype=BF16),
                        two(tm, MIX_WIDTH, dtype=F32),
                        pltpu.VMEM((tm, MIX_WIDTH), BF16),
                        pltpu.VMEM((RET_CHUNK, 2 * RET_CHUNK), F32),
                        pltpu.VMEM((nchunk * A_HEADS, RET_CHUNK, 2 * RET_CHUNK), F32),
                        pltpu.VMEM((nchunk * A_HEADS, RET_CHUNK, 2 * RET_CHUNK), BF16),
                        pltpu.VMEM((nchunk * A_TILES, RET_CHUNK, LANES), F32),
                        pltpu.VMEM((nchunk * B_HEADS, RET_CHUNK, RET_CHUNK), BF16),
                        pltpu.VMEM((nchunk * B_HEADS, B_DK, B_DV), F32),
                        pltpu.VMEM((nchunk * B_HEADS, B_DK, B_DV), BF16)],
        compiler_params=pltpu.CompilerParams(dimension_semantics=("arbitrary",),
                                             vmem_limit_bytes=VMEM_LIMIT),
        name="prompt_layer",
    )(sinks, gam, x, x, win, wout, ng, gq, gk, *rot_a, *rot_b, dmat, qdec, kdec)


def _sample_kernel(gam_ref, x_ref, win_ref, wout_ref, ng_ref, gq_ref, gk_ref,
                   ca_ref, sa1_ref, sa2_ref, cb_ref, sb_ref, sink_ref, d8_ref, qdec_ref, kdec_ref,
                   ck_ref, cv_ref, st_ref,
                   y_ref, ko_ref, vo_ref, so_ref,
                   xs_s, qz_s, kn_s, vn_s, knt_s, vnt_s, qb_s, kb_s, vb_s, g_s, mix_s, *, nseq, dec):
    layer = pl.program_id(0)
    step = pl.program_id(1)
    ntok = x_ref.shape[0]
    pair_rows = 2 * dec
    npair = nseq // 2

    @pl.when(jnp.logical_and(layer == 0, step == 0))
    def _():
        xs_s[...] = x_ref[...]

    @pl.when(step == 0)
    def _():
        h = (_row_rms(xs_s[...]) * ng_ref[...]).astype(BF16)
        lo_mask = lax.broadcasted_iota(jnp.int32, (ntok, LANES), 1) < A_HEAD_DIM
        ca, sa1, sa2 = ca_ref[...], sa1_ref[...], sa2_ref[...]
        cb, sb = cb_ref[...], sb_ref[...]

        def tile(a, i):
            return a[:, i * LANES:(i + 1) * LANES]

        qa = _dot(h, win_ref[:, OFF_QA:OFF_QA + A_WIDTH])
        for m in range(A_TILES):
            qr = _pair_norm_rot(tile(qa, m), gq_ref[...], ca, sa1, sa2, lo_mask)
            sw = pltpu.roll(qr, A_HEAD_DIM, 1)
            if m < A_TILES // 2:
                qz_s[2 * m] = jnp.where(lo_mask, qr, 0.0)
                qz_s[2 * m + 1] = jnp.where(lo_mask, sw, 0.0)
            else:
                qz_s[2 * m] = jnp.where(lo_mask, 0.0, sw)
                qz_s[2 * m + 1] = jnp.where(lo_mask, 0.0, qr)
        kv = _dot(h, win_ref[:, OFF_KA:OFF_KA + 2 * A_KV_WIDTH])
        kn = _pair_norm_rot(tile(kv, 0), gk_ref[...], ca, sa1, sa2, lo_mask)
        vn = tile(kv, 1)
        kn_s[...] = kn
        vn_s[...] = vn
        for tt in range(ntok // LANES):
            knt_s[tt] = kn[tt * LANES:(tt + 1) * LANES, :].T
            vnt_s[tt] = vn[tt * LANES:(tt + 1) * LANES, :].T
        ga = _dot(h, win_ref[:, OFF_GA:OFF_GA + A_WIDTH])
        for m in range(A_TILES):
            g_s[:, m * LANES:(m + 1) * LANES] = _silu(tile(ga, m))
        qb = _dot(h, win_ref[:, OFF_QB:OFF_QB + B_QK_WIDTH])
        for hh in range(B_HEADS):
            qb_s[:, hh * LANES:(hh + 1) * LANES] = _rot_b(tile(qb, hh), cb, sb)
        kb = _dot(h, win_ref[:, OFF_KB:OFF_KB + B_QK_WIDTH])
        for hh in range(B_HEADS):
            kb_s[:, hh * LANES:(hh + 1) * LANES] = _rot_b(tile(kb, hh), cb, sb) * (B_DK ** -0.5)
        vb_s[...] = _dot(h, win_ref[:, OFF_VB:OFF_VB + B_V_WIDTH])
        gb = _dot(h, win_ref[:, OFF_GB:OFF_GB + B_V_WIDTH])
        for hh in range(B_HEADS):
            g_s[:, A_WIDTH + hh * LANES:A_WIDTH + (hh + 1) * LANES] = _silu(tile(gb, hh))

    nq = A_HEADS * pair_rows
    row = lax.broadcasted_iota(jnp.int32, (nq, LANES), 0)
    slot = lax.broadcasted_iota(jnp.int32, (nq, LANES), 1)
    first_seq = (row % pair_rows) < dec
    cache_mask = slot > (row % dec)
    rown = lax.broadcasted_iota(jnp.int32, (nq, pair_rows), 0)
    coln = lax.broadcasted_iota(jnp.int32, (nq, pair_rows), 1)
    new_mask = ((coln // dec) == ((rown % pair_rows) // dec)) & ((coln % dec) <= (rown % dec))
    lo8 = lax.broadcasted_iota(jnp.int32, (pair_rows, LANES), 1) < A_HEAD_DIM
    r8 = lax.broadcasted_iota(jnp.int32, (pair_rows, LANES), 0)
    keep = lax.broadcasted_iota(jnp.int32, (LANES, WINDOW), 1) < WINDOW - dec
    sink = sink_ref[...]

    for p in range(npair):
        b0 = 2 * p
        seq0 = step * nseq + b0
        rows = pl.ds(pl.multiple_of(seq0 * dec, pair_rows), pair_rows)

        qz = qz_s[:, rows, :].reshape(nq, LANES).astype(BF16)
        kn = kn_s[rows, :]
        vn = vn_s[rows, :]
        kt = [ck_ref[b0], ck_ref[b0 + 1]]
        vt = [cv_ref[b0], cv_ref[b0 + 1]]

        s_c = jnp.where(first_seq, _dot(qz, kt[0].astype(BF16)), _dot(qz, kt[1].astype(BF16)))
        s_n = _dot_nt(qz, kn.astype(BF16))
        rets = []
        for hh in range(B_HEADS):
            sl = slice(hh * LANES, (hh + 1) * LANES)
            q = qb_s[rows, sl].astype(BF16)
            kf = kb_s[rows, sl]
            k = kf.astype(BF16)
            kd = kf * kdec_ref[hh]
            v = vb_s[rows, sl].astype(BF16)
            sc = (_dot_nt(q, k) * d8_ref[hh]).astype(BF16)
            crosses = []
            for j in range(2):
                S = st_ref[b0 + j, hh]
                crosses.append(_dot(q, S.astype(BF16)))
                kdj = jnp.where((r8 // dec) == j, kd, 0.0).astype(BF16)
                so_ref[b0 + j, hh] = gam_ref[hh] * S + _dot_tn(kdj, v)
            cross = jnp.where(r8 < dec, crosses[0], crosses[1]) * qdec_ref[hh]
            rets.append((sc, v, cross))

        s_c = jnp.where(cache_mask, s_c, NEG)
        s_n = jnp.where(new_mask, s_n, NEG)
        p_c, p_n = _softmax_parts([s_c, s_n], sink)
        p_c = p_c.astype(BF16)

        o = jnp.where(first_seq, _dot_nt(p_c, vt[0].astype(BF16)), _dot_nt(p_c, vt[1].astype(BF16)))
        o = o + _dot(p_n.astype(BF16), vn.astype(BF16))
        for m in range(A_TILES):
            oa = o[(2 * m) * pair_rows:(2 * m + 1) * pair_rows, :]
            ob = o[(2 * m + 1) * pair_rows:(2 * m + 2) * pair_rows, :]
            if m < A_TILES // 2:
                blk = jnp.where(lo8, oa, pltpu.roll(ob, A_HEAD_DIM, 1))
            else:
                blk = jnp.where(lo8, pltpu.roll(oa, A_HEAD_DIM, 1), ob)
            sl = slice(m * LANES, (m + 1) * LANES)
            mix_s[rows, sl] = g_s[rows, sl] * blk
        for hh, (sc, v, cross) in enumerate(rets):
            osl = slice(A_WIDTH + hh * LANES, A_WIDTH + (hh + 1) * LANES)
            mix_s[rows, osl] = g_s[rows, osl] * _row_rms(_dot(sc, v) + cross)

        for j in range(2):
            tok0 = (seq0 + j) * dec
            tt = tok0 // LANES
            shift = (WINDOW - dec) - tok0 % LANES
            ko_ref[b0 + j] = jnp.where(keep, pltpu.roll(kt[j], WINDOW - dec, 1), pltpu.roll(knt_s[tt], shift, 1))
            vo_ref[b0 + j] = jnp.where(keep, pltpu.roll(vt[j], WINDOW - dec, 1), pltpu.roll(vnt_s[tt], shift, 1))

    @pl.when(step == pl.num_programs(1) - 1)
    def _():
        xs_s[...] = xs_s[...] + _dot(mix_s[...].astype(BF16), wout_ref[...])

    @pl.when(jnp.logical_and(layer == pl.num_programs(0) - 1, step == pl.num_programs(1) - 1))
    def _():
        y_ref[...] = xs_s[...]


def _stacked_spec(shape):
    nd = len(shape)
    return pl.BlockSpec((None,) + tuple(shape), lambda l, i: (l,) + (0,) * nd)


def _sample_layers(x, win, wout, ng, gq, gk, sink_col, gam, rot_a, rot_b, d8, qdec, kdec, ck, cv, st, dec):
    ntok, D = x.shape
    depth, nb = ck.shape[:2]
    nseq = SAMPLE_SEQS
    assert nb % nseq == 0 and nseq % 2 == 0 and ntok == nb * dec and 2 * dec == 8 and ntok % LANES == 0
    assert LANES % dec == 0 and WINDOW == LANES
    pair_rows = 2 * dec
    nq = A_HEADS * pair_rows
    smem = pl.BlockSpec(memory_space=pltpu.SMEM)
    kv = pl.BlockSpec((None, nseq, A_KV_WIDTH, WINDOW), lambda l, i: (l, i, 0, 0))
    stt = pl.BlockSpec((None, nseq, B_HEADS, B_DK, B_DV), lambda l, i: (l, i, 0, 0, 0))
    tab = _const_spec((ntok, LANES))
    return pl.pallas_call(
        functools.partial(_sample_kernel, nseq=nseq, dec=dec),
        grid=(depth, nb // nseq),
        in_specs=[smem, _const_spec((ntok, D)), _stacked_spec((D, IN_WIDTH)), _stacked_spec((MIX_WIDTH, D)),
                  _stacked_spec((1, D)), _stacked_spec((1, LANES)), _stacked_spec((1, LANES)),
                  tab, tab, tab, tab, tab,
                  _stacked_spec((nq, 1)), _const_spec((B_HEADS, pair_rows, pair_rows)),
                  _const_spec((B_HEADS, pair_rows, LANES)), _const_spec((B_HEADS, pair_rows, LANES)),
                  kv, kv, stt],
        out_specs=[pl.BlockSpec((ntok, D), lambda l, i: (0, 0)), kv, kv, stt],
        out_shape=[jax.ShapeDtypeStruct((ntok, D), F32),
                   jax.ShapeDtypeStruct(ck.shape, F32),
                   jax.ShapeDtypeStruct(cv.shape, F32),
                   jax.ShapeDtypeStruct(st.shape, F32)],
        scratch_shapes=[pltpu.VMEM((ntok, D), F32),
                        pltpu.VMEM((A_HEADS, ntok, LANES), F32),
                        pltpu.VMEM((ntok, LANES), F32),
                        pltpu.VMEM((ntok, LANES), F32),
                        pltpu.VMEM((ntok // LANES, A_KV_WIDTH, LANES), F32),
                        pltpu.VMEM((ntok // LANES, A_KV_WIDTH, LANES), F32),
                        pltpu.VMEM((ntok, B_QK_WIDTH), F32),
                        pltpu.VMEM((ntok, B_QK_WIDTH), F32),
                        pltpu.VMEM((ntok, B_V_WIDTH), F32),
                        pltpu.VMEM((ntok, MIX_WIDTH), F32),
                        pltpu.VMEM((ntok, MIX_WIDTH), F32)],
        compiler_params=pltpu.CompilerParams(dimension_semantics=("arbitrary", "arbitrary"),
                                             vmem_limit_bytes=VMEM_LIMIT),
        name="sample_layers",
    )(gam, x, win, wout, ng, gq, gk, *rot_a, *rot_b, sink_col, d8, qdec, kdec, ck, cv, st)


def _rot_tables_a(pos):
    half = A_ROT_DIM // 2
    inv_freq = ROPE_THETA_A ** (-jnp.arange(half, dtype=F32) / half)
    ang = pos.astype(F32)[:, None] * inv_freq[None, :]
    cos, sin = jnp.cos(ang), jnp.sin(ang)
    n = pos.shape[0]
    zh = jnp.zeros((n, half), F32)
    rest = A_HEAD_DIM - A_ROT_DIM
    c = jnp.concatenate([cos, cos, jnp.ones((n, rest), F32)], axis=-1)
    s1 = jnp.concatenate([-sin, zh, jnp.zeros((n, rest), F32)], axis=-1)
    s2 = jnp.concatenate([zh, sin, jnp.zeros((n, rest), F32)], axis=-1)
    return tuple(jnp.tile(a, (1, HEADS_PER_TILE)) for a in (c, s1, s2))


def _rot_tables_b(pos):
    half = B_DK // 2
    inv_freq = ROPE_THETA_B ** (-jnp.arange(half, dtype=F32) / half)
    ang = pos.astype(F32)[:, None] * inv_freq[None, :]
    cos, sin = jnp.cos(ang), jnp.sin(ang)
    return jnp.concatenate([cos, cos], axis=-1), jnp.concatenate([-sin, sin], axis=-1)


def _log_decay():
    return jnp.log(1.0 - 2.0 ** (-5.0 - jnp.arange(B_HEADS, dtype=F32)))


def _decay_tables(C):
    lg = _log_decay()
    idx = jnp.arange(C, dtype=F32)
    diff = idx[:, None] - idx[None, :]
    dmat = jnp.where(diff >= 0, jnp.exp(lg[:, None, None] * jnp.maximum(diff, 0.0)), 0.0)
    q_dec = jnp.exp(lg[None, :] * (idx[:, None] + 1.0))
    k_dec = jnp.exp(lg[None, :] * (C - 1.0 - idx[:, None]))
    gam = jnp.exp(lg * C)
    return dmat, q_dec, k_dec, gam


def kernel(x_prompt, x_sample, cache_swa_k, cache_swa_v, state_ret, w_in, w_out, norm_g, q_norm_g, k_norm_g, sinks):
    N, T, D = x_prompt.shape
    nb, dec, _ = x_sample.shape
    w_buf = cache_swa_k.shape[2]
    assert w_buf == WINDOW

    pos_p = jnp.arange(T, dtype=jnp.int32)
    pos_s = PAST_LEN + jnp.arange(dec, dtype=jnp.int32)
    rot_a_p, rot_b_p = _rot_tables_a(pos_p), _rot_tables_b(pos_p)
    rot_a_s = tuple(jnp.tile(a, (nb, 1)) for a in _rot_tables_a(pos_s))
    rot_b_s = tuple(jnp.tile(a, (nb, 1)) for a in _rot_tables_b(pos_s))

    dmat, q_dec, k_dec, gam_p = _decay_tables(RET_CHUNK)
    qdec_p = jnp.broadcast_to(q_dec.T[:, :, None], (B_HEADS, RET_CHUNK, LANES))
    kdec_p = jnp.broadcast_to(k_dec.T[:, :, None], (B_HEADS, RET_CHUNK, LANES))
    dmat4, q_dec4, k_dec4, gam_s = _decay_tables(dec)
    d8 = jnp.kron(jnp.eye(2, dtype=F32)[None], jnp.ones((1, dec, dec), F32)) * jnp.tile(dmat4, (1, 2, 2))
    qdec_s = jnp.broadcast_to(jnp.tile(q_dec4.T, (1, 2))[:, :, None], (B_HEADS, 2 * dec, LANES))
    kdec_s = jnp.broadcast_to(jnp.tile(k_dec4.T, (1, 2))[:, :, None], (B_HEADS, 2 * dec, LANES))

    w_in_b = w_in.astype(BF16)
    w_out_b = w_out.astype(BF16)
    ck = cache_swa_k.transpose(0, 1, 3, 4, 2).reshape(DEPTH, nb, A_KV_WIDTH, w_buf)
    cv = cache_swa_v.transpose(0, 1, 3, 4, 2).reshape(DEPTH, nb, A_KV_WIDTH, w_buf)

    ng = norm_g[:, None, :]
    gq = jnp.tile(q_norm_g, (1, HEADS_PER_TILE))[:, None, :] * (A_HEAD_DIM ** -0.5 * LOG2E)
    gk = jnp.tile(k_norm_g, (1, HEADS_PER_TILE))[:, None, :]
    sinks2 = sinks * LOG2E

    sink_col = jnp.repeat(sinks2, 2 * dec, axis=1)[:, :, None]
    xs, ks, vs, ss = _sample_layers(x_sample.reshape(nb * dec, D), w_in_b, w_out_b, ng, gq, gk, sink_col, gam_s,
                                    rot_a_s, rot_b_s, d8, qdec_s, kdec_s, ck, cv, state_ret, dec)

    xp = x_prompt
    kp_l, vp_l, sp_l = [], [], []
    for l in range(DEPTH):
        xp, kl, vl, S = _prompt_layer(l, xp, w_in_b, w_out_b, ng, gq, gk, sinks2[l], gam_p,
                                      rot_a_p, rot_b_p, dmat, qdec_p, kdec_p)
        kp_l.append(kl)
        vp_l.append(vl)
        sp_l.append(S)

    kv_p = (DEPTH, N, WINDOW, A_KV_HEADS, A_HEAD_DIM)
    kv_t = (DEPTH, nb, A_KV_HEADS, A_HEAD_DIM, w_buf)
    return (xp, xs.reshape(nb, dec, D),
            jnp.stack(kp_l).reshape(kv_p), jnp.stack(vp_l).reshape(kv_p), jnp.stack(sp_l),
            ks.reshape(kv_t).transpose(0, 1, 4, 2, 3), vs.reshape(kv_t).transpose(0, 1, 4, 2, 3), ss)
```

```python
import functools

import jax
import jax.numpy as jnp
from jax import lax
from jax.experimental import pallas as pl
from jax.experimental.pallas import tpu as pltpu

D_MODEL = 1024
DEPTH = 4
PAST_LEN = 8192
A_HEADS = 8
A_KV_HEADS = 2
A_GROUP = A_HEADS // A_KV_HEADS
A_HEAD_DIM = 64
WINDOW = 128
A_ROT_DIM = A_HEAD_DIM // 4
ROPE_THETA_A = 500000.0
B_HEADS = 4
B_DK = 128
B_DV = 128
RET_CHUNK = 128
ROPE_THETA_B = 10000.0
EPS = 1e-6

A_WIDTH = A_HEADS * A_HEAD_DIM
A_KV_WIDTH = A_KV_HEADS * A_HEAD_DIM
B_QK_WIDTH = B_HEADS * B_DK
B_V_WIDTH = B_HEADS * B_DV
MIX_WIDTH = A_WIDTH + B_V_WIDTH
OFF_QA = 0
OFF_KA = OFF_QA + A_WIDTH
OFF_VA = OFF_KA + A_KV_WIDTH
OFF_GA = OFF_VA + A_KV_WIDTH
OFF_QB = OFF_GA + A_WIDTH
OFF_KB = OFF_QB + B_QK_WIDTH
OFF_VB = OFF_KB + B_QK_WIDTH
OFF_GB = OFF_VB + B_V_WIDTH
IN_WIDTH = OFF_GB + B_V_WIDTH

LANES = 128
HEADS_PER_TILE = LANES // A_HEAD_DIM
A_TILES = A_WIDTH // LANES
NEG = -1e30
LOG2E = 1.4426950408889634

PROMPT_BLOCK = 512
SAMPLE_SEQS = 8
VMEM_LIMIT = 56 * 1024 * 1024

F32 = jnp.float32
BF16 = jnp.bfloat16


def _dot(a, b):
    return jnp.dot(a, b, preferred_element_type=F32)


def _dot_nt(a, b):
    return lax.dot_general(a, b, (((1,), (1,)), ((), ())), preferred_element_type=F32)


def _dot_tn(a, b):
    return lax.dot_general(a, b, (((0,), (0,)), ((), ())), preferred_element_type=F32)


def _silu(x):
    return x * (1.0 / (1.0 + jnp.exp(-x)))


def _row_rms(x):
    n = x.shape[-1]
    return x * lax.rsqrt(jnp.sum(x * x, axis=-1, keepdims=True) * (1.0 / n) + EPS)


def _pair_norm_rot(blk, g_row, ca, sa1, sa2, lo_mask):
    sq = blk * blk
    tot = jnp.sum(sq, axis=-1, keepdims=True)
    lo = jnp.sum(jnp.where(lo_mask, sq, 0.0), axis=-1, keepdims=True)
    hi = tot - lo
    inv_lo = lax.rsqrt(lo * (1.0 / A_HEAD_DIM) + EPS)
    inv_hi = lax.rsqrt(hi * (1.0 / A_HEAD_DIM) + EPS)
    xn = blk * jnp.where(lo_mask, inv_lo, inv_hi) * g_row
    half = A_ROT_DIM // 2
    return xn * ca + pltpu.roll(xn, LANES - half, 1) * sa1 + pltpu.roll(xn, half, 1) * sa2


def _rot_b(blk, cb, sb):
    return blk * cb + pltpu.roll(blk, B_DK // 2, 1) * sb


def _softmax_parts(parts, sink):
    mx = sink
    for s in parts:
        mx = jnp.maximum(mx, jnp.max(s, axis=-1, keepdims=True))
    ps = [jnp.exp2(s - mx) for s in parts]
    den = jnp.exp2(sink - mx)
    for p in ps:
        den = den + jnp.sum(p, axis=-1, keepdims=True)
    rinv = 1.0 / den
    return [p * rinv for p in ps]


def _prompt_kernel(sinks_ref, gam_ref, x_ref, xres_ref, win_ref, wout_ref, ng_ref, gq_ref, gk_ref,
                   ca_ref, sa1_ref, sa2_ref, cb_ref, sb_ref, dmat_ref, qdec_ref, kdec_ref,
                   y_ref, klast_ref, vlast_ref, S_ref,
                   q_s, kz_s, vz_s, qb_s, kb_s, kbd_s, vb_s, g_s, mix_s, bias_s, s_s, p_s, es_s,
                   sc_s, kv_s, sb_s, *, tm, nt):
    g = pl.program_id(0)
    nchunk = tm // RET_CHUNK
    C = RET_CHUNK
    lo_c = lax.broadcasted_iota(jnp.int32, (C, LANES), 1) < A_HEAD_DIM
    t_fin = jnp.maximum(g - 1, 0) % nt

    @pl.when(g == 0)
    def _():
        qi = lax.broadcasted_iota(jnp.int32, (C, 2 * C), 0)
        kj = lax.broadcasted_iota(jnp.int32, (C, 2 * C), 1)
        bias_s[...] = jnp.where((kj > qi) & (kj <= qi + C), 0.0, NEG)
        for ref in (q_s, kz_s, vz_s, qb_s, kb_s, kbd_s, vb_s, g_s):
            ref[1] = jnp.zeros(ref.shape[1:], ref.dtype)

    @pl.when(t_fin == 0)
    def _():
        S_ref[...] = jnp.zeros_like(S_ref)

    def tile(a, i):
        return a[:, i * LANES:(i + 1) * LANES]

    def body(w, r):
        lo_mask = lax.broadcasted_iota(jnp.int32, (tm, LANES), 1) < A_HEAD_DIM
        ca, sa1, sa2 = ca_ref[...], sa1_ref[...], sa2_ref[...]
        cb, sb = cb_ref[...], sb_ref[...]
        first_bias = jnp.where(t_fin == 0, NEG, 0.0)

        def chunk_rows(c):
            return slice(c * C, (c + 1) * C)

        def window_rows(c):
            return slice(c * C, (c + 2) * C)

        def scores(c):
            for m in range(A_TILES):
                grp = m // (A_GROUP // HEADS_PER_TILE)
                qp = q_s[r, chunk_rows(c), m * LANES:(m + 1) * LANES]
                for half in range(HEADS_PER_TILE):
                    s = _dot_nt(qp, kz_s[r, 2 * grp + half, window_rows(c), :]) + bias_s[...]
                    if c == 0:
                        s = jnp.concatenate([s[:, :C] + first_bias, s[:, C:]], axis=1)
                    s_s[c * A_HEADS + m * HEADS_PER_TILE + half] = s

        def softmax(c):
            for m in range(A_TILES):
                es = []
                for half in range(HEADS_PER_TILE):
                    hd = m * HEADS_PER_TILE + half
                    s = s_s[c * A_HEADS + hd]
                    sink = sinks_ref[hd]
                    mx = jnp.maximum(jnp.max(s, axis=-1, keepdims=True), sink)
                    p_s[c * A_HEADS + hd] = jnp.exp2(s - mx).astype(BF16)
                    es.append(jnp.exp2(sink - mx))
                es_s[c * A_TILES + m] = jnp.where(lo_c, es[0], es[1])

        def attend(c):
            for m in range(A_TILES):
                grp = m // (A_GROUP // HEADS_PER_TILE)
                sl = slice(m * LANES, (m + 1) * LANES)
                oe = _dot(p_s[c * A_HEADS + 2 * m], vz_s[r, 2 * grp, window_rows(c), :])
                oo = _dot(p_s[c * A_HEADS + 2 * m + 1], vz_s[r, 2 * grp + 1, window_rows(c), :])
                pv = jnp.where(lo_c, oe, oo)
                den = pltpu.roll(jnp.where(lo_c, oo, oe), A_HEAD_DIM, 1) + es_s[c * A_TILES + m]
                mix_s[chunk_rows(c), sl] = (g_s[r, chunk_rows(c), sl] * pv * (1.0 / den)).astype(BF16)

        def retain_local(c):
            for hh in range(B_HEADS):
                sl = slice(hh * LANES, (hh + 1) * LANES)
                rows = chunk_rows(c)
                sc_s[c * B_HEADS + hh] = (_dot_nt(qb_s[r, rows, sl], kb_s[r, rows, sl]) * dmat_ref[hh]).astype(BF16)
                kv_s[c * B_HEADS + hh] = _dot_tn(kbd_s[r, rows, sl], vb_s[r, rows, sl])

        def retain_states():
            for hh in range(B_HEADS):
                S = S_ref[0, hh]
                for c in range(nchunk):
                    sb_s[c * B_HEADS + hh] = S.astype(BF16)
                    S = gam_ref[hh] * S + kv_s[c * B_HEADS + hh]
                S_ref[0, hh] = S

        def retain_out(c):
            for hh in range(B_HEADS):
                sl = slice(hh * LANES, (hh + 1) * LANES)
                rows = chunk_rows(c)
                i = c * B_HEADS + hh
                ret = _dot(sc_s[i], vb_s[r, rows, sl]) + _dot(qb_s[r, rows, sl], sb_s[i]) * qdec_ref[hh]
                osl = slice(A_WIDTH + hh * LANES, A_WIDTH + (hh + 1) * LANES)
                mix_s[rows, osl] = (g_s[r, rows, osl] * _row_rms(ret)).astype(BF16)

        def project_attention():
            qa = _dot(h, win_ref[:, OFF_QA:OFF_QA + A_WIDTH])
            for m in range(A_TILES):
                q_s[w, :, m * LANES:(m + 1) * LANES] = _pair_norm_rot(
                    tile(qa, m), gq_ref[...], ca, sa1, sa2, lo_mask).astype(BF16)
            kv = _dot(h, win_ref[:, OFF_KA:OFF_KA + 2 * A_KV_WIDTH])
            kr = _pair_norm_rot(tile(kv, 0), gk_ref[...], ca, sa1, sa2, lo_mask)
            va = tile(kv, 1)
            klast_ref[0] = kr[tm - WINDOW:, :]
            vlast_ref[0] = va[tm - WINDOW:, :]
            for src, dst, fill in ((kr, kz_s, 0.0), (va, vz_s, 1.0)):
                sw = pltpu.roll(src, A_HEAD_DIM, 1)
                dst[w, 0, C:, :] = jnp.where(lo_mask, src, fill).astype(BF16)
                dst[w, 1, C:, :] = jnp.where(lo_mask, fill, sw).astype(BF16)
                dst[w, 2, C:, :] = jnp.where(lo_mask, sw, fill).astype(BF16)
                dst[w, 3, C:, :] = jnp.where(lo_mask, fill, src).astype(BF16)
                for i in range(2 * A_KV_HEADS):
                    dst[w, i, 0:C, :] = dst[r, i, tm:tm + C, :]

        def project_gate_a():
            ga = _dot(h, win_ref[:, OFF_GA:OFF_GA + A_WIDTH])
            for m in range(A_TILES):
                g_s[w, :, m * LANES:(m + 1) * LANES] = _silu(tile(ga, m))

        def project_qb():
            qb = _dot(h, win_ref[:, OFF_QB:OFF_QB + B_QK_WIDTH])
            for hh in range(B_HEADS):
                qb_s[w, :, hh * LANES:(hh + 1) * LANES] = _rot_b(tile(qb, hh), cb, sb).astype(BF16)

        def project_kb():
            kb = _dot(h, win_ref[:, OFF_KB:OFF_KB + B_QK_WIDTH])
            for hh in range(B_HEADS):
                sl = slice(hh * LANES, (hh + 1) * LANES)
                kbh = _rot_b(tile(kb, hh), cb, sb) * (B_DK ** -0.5)
                kb_s[w, :, sl] = kbh.astype(BF16)
                kbd_s[w, :, sl] = (kbh.reshape(nchunk, C, LANES) * kdec_ref[hh][None]).reshape(tm, LANES).astype(BF16)

        def project_vb_gate_b():
            vb_s[w] = _dot(h, win_ref[:, OFF_VB:OFF_VB + B_V_WIDTH]).astype(BF16)
            gb = _dot(h, win_ref[:, OFF_GB:OFF_GB + B_V_WIDTH])
            for hh in range(B_HEADS):
                g_s[w, :, A_WIDTH + hh * LANES:A_WIDTH + (hh + 1) * LANES] = _silu(tile(gb, hh))

        h = (_row_rms(x_ref[0]) * ng_ref[...]).astype(BF16)
        for c in range(nchunk):
            scores(c)
            retain_local(c)
        project_attention()
        for c in range(nchunk):
            softmax(c)
        retain_states()
        project_gate_a()
        project_qb()
        for c in range(nchunk):
            attend(c)
            retain_out(c)
        project_kb()
        project_vb_gate_b()
        y_ref[0] = xres_ref[0] + _dot(mix_s[...], wout_ref[...])

    @pl.when(g % 2 == 0)
    def _():
        body(0, 1)

    @pl.when(g % 2 == 1)
    def _():
        body(1, 0)


def _const_spec(shape):
    nd = len(shape)
    return pl.BlockSpec(shape, lambda *_: (0,) * nd, pipeline_mode=pl.Buffered(1))


def _layer_spec(shape, l):
    nd = len(shape)
    return pl.BlockSpec((None,) + tuple(shape), lambda *_: (l,) + (0,) * nd, pipeline_mode=pl.Buffered(1))


def _prompt_layer(l, x, win, wout, ng, gq, gk, sinks, gam, rot_a, rot_b, dmat, qdec, kdec):
    N, T, D = x.shape
    tm = PROMPT_BLOCK
    assert T % tm == 0 and tm % RET_CHUNK == 0 and WINDOW == RET_CHUNK
    nt = T // tm
    nblk = N * nt
    nchunk = tm // RET_CHUNK

    def proj_blk(g):
        return jnp.minimum(g, nblk - 1)

    def fin_blk(g):
        return jnp.maximum(g - 1, 0)

    smem = pl.BlockSpec(memory_space=pltpu.SMEM)
    tab = pl.BlockSpec((tm, LANES), lambda g: (proj_blk(g) % nt, 0))
    x_proj = pl.BlockSpec((1, tm, D), lambda g: (proj_blk(g) // nt, proj_blk(g) % nt, 0))
    x_fin = pl.BlockSpec((1, tm, D), lambda g: (fin_blk(g) // nt, fin_blk(g) % nt, 0))
    kv_out = pl.BlockSpec((1, WINDOW, LANES), lambda g: (proj_blk(g) // nt, 0, 0))

    def two(*shape, dtype):
        return pltpu.VMEM((2,) + shape, dtype)

    return pl.pallas_call(
        functools.partial(_prompt_kernel, tm=tm, nt=nt),
        grid=(nblk + 1,),
        in_specs=[smem, smem, x_proj, x_fin,
                  _layer_spec((D, IN_WIDTH), l), _layer_spec((MIX_WIDTH, D), l),
                  _layer_spec((1, D), l), _layer_spec((1, LANES), l), _layer_spec((1, LANES), l),
                  tab, tab, tab, tab, tab,
                  _const_spec((B_HEADS, RET_CHUNK, RET_CHUNK)), _const_spec((B_HEADS, RET_CHUNK, LANES)),
                  _const_spec((B_HEADS, RET_CHUNK, LANES))],
        out_specs=[x_fin, kv_out, kv_out,
                   pl.BlockSpec((1, B_HEADS, B_DK, B_DV), lambda g: (fin_blk(g) // nt, 0, 0, 0))],
        out_shape=[jax.ShapeDtypeStruct((N, T, D), F32),
                   jax.ShapeDtypeStruct((N, WINDOW, LANES), F32),
                   jax.ShapeDtypeStruct((N, WINDOW, LANES), F32),
                   jax.ShapeDtypeStruct((N, B_HEADS, B_DK, B_DV), F32)],
        scratch_shapes=[two(tm, A_WIDTH, dtype=BF16),
                        two(2 * A_KV_HEADS, RET_CHUNK + tm, LANES, dtype=BF16),
                        two(2 * A_KV_HEADS, RET_CHUNK + tm, LANES, dtype=BF16),
                        two(tm, B_QK_WIDTH, dtype=BF16),
                        two(tm, B_QK_WIDTH, dtype=BF16),
                        two(tm, B_QK_WIDTH, dtype=BF16),
                        two(tm, B_V_WIDTH, dtype=BF16),
                        two(tm, MIX_WIDTH, dtype=F32),
                        pltpu.VMEM((tm, MIX_WIDTH), BF16),
                        pltpu.VMEM((RET_CHUNK, 2 * RET_CHUNK), F32),
                        pltpu.VMEM((nchunk * A_HEADS, RET_CHUNK, 2 * RET_CHUNK), F32),
                        pltpu.VMEM((nchunk * A_HEADS, RET_CHUNK, 2 * RET_CHUNK), BF16),
                        pltpu.VMEM((nchunk * A_TILES, RET_CHUNK, LANES), F32),
                        pltpu.VMEM((nchunk * B_HEADS, RET_CHUNK, RET_CHUNK), BF16),
                        pltpu.VMEM((nchunk * B_HEADS, B_DK, B_DV), F32),
                        pltpu.VMEM((nchunk * B_HEADS, B_DK, B_DV), BF16)],
        compiler_params=pltpu.CompilerParams(dimension_semantics=("arbitrary",),
                                             vmem_limit_bytes=VMEM_LIMIT),
        name="prompt_layer",
    )(sinks, gam, x, x, win, wout, ng, gq, gk, *rot_a, *rot_b, dmat, qdec, kdec)


def _sample_kernel(gam_ref, x_ref, win_ref, wout_ref, ng_ref, gq_ref, gk_ref,
                   ca_ref, sa1_ref, sa2_ref, cb_ref, sb_ref, sink_ref, d8_ref, qdec_ref, kdec_ref,
                   ck_ref, cv_ref, st_ref,
                   y_ref, ko_ref, vo_ref, so_ref,
                   xs_s, qz_s, kn_s, vn_s, knt_s, vnt_s, qb_s, kb_s, vb_s, g_s, mix_s, *, nseq, dec):
    layer = pl.program_id(0)
    step = pl.program_id(1)
    ntok = x_ref.shape[0]
    pair_rows = 2 * dec
    npair = nseq // 2

    @pl.when(jnp.logical_and(layer == 0, step == 0))
    def _():
        xs_s[...] = x_ref[...]

    @pl.when(step == 0)
    def _():
        h = (_row_rms(xs_s[...]) * ng_ref[...]).astype(BF16)
        lo_mask = lax.broadcasted_iota(jnp.int32, (ntok, LANES), 1) < A_HEAD_DIM
        ca, sa1, sa2 = ca_ref[...], sa1_ref[...], sa2_ref[...]
        cb, sb = cb_ref[...], sb_ref[...]

        def tile(a, i):
            return a[:, i * LANES:(i + 1) * LANES]

        qa = _dot(h, win_ref[:, OFF_QA:OFF_QA + A_WIDTH])
        for m in range(A_TILES):
            qr = _pair_norm_rot(tile(qa, m), gq_ref[...], ca, sa1, sa2, lo_mask)
            sw = pltpu.roll(qr, A_HEAD_DIM, 1)
            if m < A_TILES // 2:
                qz_s[2 * m] = jnp.where(lo_mask, qr, 0.0)
                qz_s[2 * m + 1] = jnp.where(lo_mask, sw, 0.0)
            else:
                qz_s[2 * m] = jnp.where(lo_mask, 0.0, sw)
                qz_s[2 * m + 1] = jnp.where(lo_mask, 0.0, qr)
        kv = _dot(h, win_ref[:, OFF_KA:OFF_KA + 2 * A_KV_WIDTH])
        kn = _pair_norm_rot(tile(kv, 0), gk_ref[...], ca, sa1, sa2, lo_mask)
        vn = tile(kv, 1)
        kn_s[...] = kn
        vn_s[...] = vn
        for tt in range(ntok // LANES):
            knt_s[tt] = kn[tt * LANES:(tt + 1) * LANES, :].T
            vnt_s[tt] = vn[tt * LANES:(tt + 1) * LANES, :].T
        ga = _dot(h, win_ref[:, OFF_GA:OFF_GA + A_WIDTH])
        for m in range(A_TILES):
            g_s[:, m * LANES:(m + 1) * LANES] = _silu(tile(ga, m))
        qb = _dot(h, win_ref[:, OFF_QB:OFF_QB + B_QK_WIDTH])
        for hh in range(B_HEADS):
            qb_s[:, hh * LANES:(hh + 1) * LANES] = _rot_b(tile(qb, hh), cb, sb)
        kb = _dot(h, win_ref[:, OFF_KB:OFF_KB + B_QK_WIDTH])
        for hh in range(B_HEADS):
            kb_s[:, hh * LANES:(hh + 1) * LANES] = _rot_b(tile(kb, hh), cb, sb) * (B_DK ** -0.5)
        vb_s[...] = _dot(h, win_ref[:, OFF_VB:OFF_VB + B_V_WIDTH])
        gb = _dot(h, win_ref[:, OFF_GB:OFF_GB + B_V_WIDTH])
        for hh in range(B_HEADS):
            g_s[:, A_WIDTH + hh * LANES:A_WIDTH + (hh + 1) * LANES] = _silu(tile(gb, hh))

    nq = A_HEADS * pair_rows
    row = lax.broadcasted_iota(jnp.int32, (nq, LANES), 0)
    slot = lax.broadcasted_iota(jnp.int32, (nq, LANES), 1)
    first_seq = (row % pair_rows) < dec
    cache_mask = slot > (row % dec)
    rown = lax.broadcasted_iota(jnp.int32, (nq, pair_rows), 0)
    coln = lax.broadcasted_iota(jnp.int32, (nq, pair_rows), 1)
    new_mask = ((coln // dec) == ((rown % pair_rows) // dec)) & ((coln % dec) <= (rown % dec))
    lo8 = lax.broadcasted_iota(jnp.int32, (pair_rows, LANES), 1) < A_HEAD_DIM
    r8 = lax.broadcasted_iota(jnp.int32, (pair_rows, LANES), 0)
    keep = lax.broadcasted_iota(jnp.int32, (LANES, WINDOW), 1) < WINDOW - dec
    sink = sink_ref[...]

    def pair_rows_of(p):
        seq0 = step * nseq + 2 * p
        return pl.ds(pl.multiple_of(seq0 * dec, pair_rows), pair_rows)


    stage1 = []
    for p in range(npair):
        b0 = 2 * p
        rows = pair_rows_of(p)
        qz = qz_s[:, rows, :].reshape(nq, LANES).astype(BF16)
        s_c = jnp.where(first_seq, _dot(qz, ck_ref[b0].astype(BF16)), _dot(qz, ck_ref[b0 + 1].astype(BF16)))
        s_n = _dot_nt(qz, kn_s[rows, :].astype(BF16))
        rets = []
        for hh in range(B_HEADS):
            sl = slice(hh * LANES, (hh + 1) * LANES)
            q = qb_s[rows, sl].astype(BF16)
            kf = kb_s[rows, sl]
            kd = kf * kdec_ref[hh]
            v = vb_s[rows, sl].astype(BF16)
            sc = (_dot_nt(q, kf.astype(BF16)) * d8_ref[hh]).astype(BF16)
            crosses = []
            for j in range(2):
                S = st_ref[b0 + j, hh]
                crosses.append(_dot(q, S.astype(BF16)))
                kdj = jnp.where((r8 // dec) == j, kd, 0.0).astype(BF16)
                so_ref[b0 + j, hh] = gam_ref[hh] * S + _dot_tn(kdj, v)
            cross = jnp.where(r8 < dec, crosses[0], crosses[1]) * qdec_ref[hh]
            rets.append((sc, v, cross))
        stage1.append((s_c, s_n, rets))

    probs = []
    for s_c, s_n, _ in stage1:
        p_c, p_n = _softmax_parts([jnp.where(cache_mask, s_c, NEG), jnp.where(new_mask, s_n, NEG)], sink)
        probs.append((p_c.astype(BF16), p_n.astype(BF16)))

    for p in range(npair):
        b0 = 2 * p
        rows = pair_rows_of(p)
        p_c, p_n = probs[p]
        o = jnp.where(first_seq, _dot_nt(p_c, cv_ref[b0].astype(BF16)), _dot_nt(p_c, cv_ref[b0 + 1].astype(BF16)))
        o = o + _dot(p_n, vn_s[rows, :].astype(BF16))
        for m in range(A_TILES):
            oa = o[(2 * m) * pair_rows:(2 * m + 1) * pair_rows, :]
            ob = o[(2 * m + 1) * pair_rows:(2 * m + 2) * pair_rows, :]
            if m < A_TILES // 2:
                blk = jnp.where(lo8, oa, pltpu.roll(ob, A_HEAD_DIM, 1))
            else:
                blk = jnp.where(lo8, pltpu.roll(oa, A_HEAD_DIM, 1), ob)
            sl = slice(m * LANES, (m + 1) * LANES)
            mix_s[rows, sl] = g_s[rows, sl] * blk
        for hh, (sc, v, cross) in enumerate(stage1[p][2]):
            osl = slice(A_WIDTH + hh * LANES, A_WIDTH + (hh + 1) * LANES)
            mix_s[rows, osl] = g_s[rows, osl] * _row_rms(_dot(sc, v) + cross)

    for b in range(nseq):
        tok0 = (step * nseq + b) * dec
        tt = tok0 // LANES
        shift = (WINDOW - dec) - tok0 % LANES
        ko_ref[b] = jnp.where(keep, pltpu.roll(ck_ref[b], WINDOW - dec, 1), pltpu.roll(knt_s[tt], shift, 1))
        vo_ref[b] = jnp.where(keep, pltpu.roll(cv_ref[b], WINDOW - dec, 1), pltpu.roll(vnt_s[tt], shift, 1))

    @pl.when(step == pl.num_programs(1) - 1)
    def _():
        xs_s[...] = xs_s[...] + _dot(mix_s[...].astype(BF16), wout_ref[...])

    @pl.when(jnp.logical_and(layer == pl.num_programs(0) - 1, step == pl.num_programs(1) - 1))
    def _():
        y_ref[...] = xs_s[...]


def _stacked_spec(shape):
    nd = len(shape)
    return pl.BlockSpec((None,) + tuple(shape), lambda l, i: (l,) + (0,) * nd)


def _sample_layers(x, win, wout, ng, gq, gk, sink_col, gam, rot_a, rot_b, d8, qdec, kdec, ck, cv, st, dec):
    ntok, D = x.shape
    depth, nb = ck.shape[:2]
    nseq = SAMPLE_SEQS
    assert nb % nseq == 0 and nseq % 2 == 0 and ntok == nb * dec and 2 * dec == 8 and ntok % LANES == 0
    assert LANES % dec == 0 and WINDOW == LANES
    pair_rows = 2 * dec
    nq = A_HEADS * pair_rows
    smem = pl.BlockSpec(memory_space=pltpu.SMEM)
    kv = pl.BlockSpec((None, nseq, A_KV_WIDTH, WINDOW), lambda l, i: (l, i, 0, 0))
    stt = pl.BlockSpec((None, nseq, B_HEADS, B_DK, B_DV), lambda l, i: (l, i, 0, 0, 0))
    tab = _const_spec((ntok, LANES))
    return pl.pallas_call(
        functools.partial(_sample_kernel, nseq=nseq, dec=dec),
        grid=(depth, nb // nseq),
        in_specs=[smem, _const_spec((ntok, D)), _stacked_spec((D, IN_WIDTH)), _stacked_spec((MIX_WIDTH, D)),
                  _stacked_spec((1, D)), _stacked_spec((1, LANES)), _stacked_spec((1, LANES)),
                  tab, tab, tab, tab, tab,
                  _stacked_spec((nq, 1)), _const_spec((B_HEADS, pair_rows, pair_rows)),
                  _const_spec((B_HEADS, pair_rows, LANES)), _const_spec((B_HEADS, pair_rows, LANES)),
                  kv, kv, stt],
        out_specs=[pl.BlockSpec((ntok, D), lambda l, i: (0, 0)), kv, kv, stt],
        out_shape=[jax.ShapeDtypeStruct((ntok, D), F32),
                   jax.ShapeDtypeStruct(ck.shape, F32),
                   jax.ShapeDtypeStruct(cv.shape, F32),
                   jax.ShapeDtypeStruct(st.shape, F32)],
        scratch_shapes=[pltpu.VMEM((ntok, D), F32),
                        pltpu.VMEM((A_HEADS, ntok, LANES), F32),
                        pltpu.VMEM((ntok, LANES), F32),
                        pltpu.VMEM((ntok, LANES), F32),
                        pltpu.VMEM((ntok // LANES, A_KV_WIDTH, LANES), F32),
                        pltpu.VMEM((ntok // LANES, A_KV_WIDTH, LANES), F32),
                        pltpu.VMEM((ntok, B_QK_WIDTH), F32),
                        pltpu.VMEM((ntok, B_QK_WIDTH), F32),
                        pltpu.VMEM((ntok, B_V_WIDTH), F32),
                        pltpu.VMEM((ntok, MIX_WIDTH), F32),
                        pltpu.VMEM((ntok, MIX_WIDTH), F32)],
        compiler_params=pltpu.CompilerParams(dimension_semantics=("arbitrary", "arbitrary"),
                                             vmem_limit_bytes=VMEM_LIMIT),
        name="sample_layers",
    )(gam, x, win, wout, ng, gq, gk, *rot_a, *rot_b, sink_col, d8, qdec, kdec, ck, cv, st)


def _rot_tables_a(pos):
    half = A_ROT_DIM // 2
    inv_freq = ROPE_THETA_A ** (-jnp.arange(half, dtype=F32) / half)
    ang = pos.astype(F32)[:, None] * inv_freq[None, :]
    cos, sin = jnp.cos(ang), jnp.sin(ang)
    n = pos.shape[0]
    zh = jnp.zeros((n, half), F32)
    rest = A_HEAD_DIM - A_ROT_DIM
    c = jnp.concatenate([cos, cos, jnp.ones((n, rest), F32)], axis=-1)
    s1 = jnp.concatenate([-sin, zh, jnp.zeros((n, rest), F32)], axis=-1)
    s2 = jnp.concatenate([zh, sin, jnp.zeros((n, rest), F32)], axis=-1)
    return tuple(jnp.tile(a, (1, HEADS_PER_TILE)) for a in (c, s1, s2))


def _rot_tables_b(pos):
    half = B_DK // 2
    inv_freq = ROPE_THETA_B ** (-jnp.arange(half, dtype=F32) / half)
    ang = pos.astype(F32)[:, None] * inv_freq[None, :]
    cos, sin = jnp.cos(ang), jnp.sin(ang)
    return jnp.concatenate([cos, cos], axis=-1), jnp.concatenate([-sin, sin], axis=-1)


def _log_decay():
    return jnp.log(1.0 - 2.0 ** (-5.0 - jnp.arange(B_HEADS, dtype=F32)))


def _decay_tables(C):
    lg = _log_decay()
    idx = jnp.arange(C, dtype=F32)
    diff = idx[:, None] - idx[None, :]
    dmat = jnp.where(diff >= 0, jnp.exp(lg[:, None, None] * jnp.maximum(diff, 0.0)), 0.0)
    q_dec = jnp.exp(lg[None, :] * (idx[:, None] + 1.0))
    k_dec = jnp.exp(lg[None, :] * (C - 1.0 - idx[:, None]))
    gam = jnp.exp(lg * C)
    return dmat, q_dec, k_dec, gam


def kernel(x_prompt, x_sample, cache_swa_k, cache_swa_v, state_ret, w_in, w_out, norm_g, q_norm_g, k_norm_g, sinks):
    N, T, D = x_prompt.shape
    nb, dec, _ = x_sample.shape
    w_buf = cache_swa_k.shape[2]
    assert w_buf == WINDOW

    pos_p = jnp.arange(T, dtype=jnp.int32)
    pos_s = PAST_LEN + jnp.arange(dec, dtype=jnp.int32)
    rot_a_p, rot_b_p = _rot_tables_a(pos_p), _rot_tables_b(pos_p)
    rot_a_s = tuple(jnp.tile(a, (nb, 1)) for a in _rot_tables_a(pos_s))
    rot_b_s = tuple(jnp.tile(a, (nb, 1)) for a in _rot_tables_b(pos_s))

    dmat, q_dec, k_dec, gam_p = _decay_tables(RET_CHUNK)
    qdec_p = jnp.broadcast_to(q_dec.T[:, :, None], (B_HEADS, RET_CHUNK, LANES))
    kdec_p = jnp.broadcast_to(k_dec.T[:, :, None], (B_HEADS, RET_CHUNK, LANES))
    dmat4, q_dec4, k_dec4, gam_s = _decay_tables(dec)
    d8 = jnp.kron(jnp.eye(2, dtype=F32)[None], jnp.ones((1, dec, dec), F32)) * jnp.tile(dmat4, (1, 2, 2))
    qdec_s = jnp.broadcast_to(jnp.tile(q_dec4.T, (1, 2))[:, :, None], (B_HEADS, 2 * dec, LANES))
    kdec_s = jnp.broadcast_to(jnp.tile(k_dec4.T, (1, 2))[:, :, None], (B_HEADS, 2 * dec, LANES))

    w_in_b = w_in.astype(BF16)
    w_out_b = w_out.astype(BF16)
    ck = cache_swa_k.transpose(0, 1, 3, 4, 2).reshape(DEPTH, nb, A_KV_WIDTH, w_buf)
    cv = cache_swa_v.transpose(0, 1, 3, 4, 2).reshape(DEPTH, nb, A_KV_WIDTH, w_buf)

    ng = norm_g[:, None, :]
    gq = jnp.tile(q_norm_g, (1, HEADS_PER_TILE))[:, None, :] * (A_HEAD_DIM ** -0.5 * LOG2E)
    gk = jnp.tile(k_norm_g, (1, HEADS_PER_TILE))[:, None, :]
    sinks2 = sinks * LOG2E

    sink_col = jnp.repeat(sinks2, 2 * dec, axis=1)[:, :, None]
    xs, ks, vs, ss = _sample_layers(x_sample.reshape(nb * dec, D), w_in_b, w_out_b, ng, gq, gk, sink_col, gam_s,
                                    rot_a_s, rot_b_s, d8, qdec_s, kdec_s, ck, cv, state_ret, dec)

    xp = x_prompt
    kp_l, vp_l, sp_l = [], [], []
    for l in range(DEPTH):
        xp, kl, vl, S = _prompt_layer(l, xp, w_in_b, w_out_b, ng, gq, gk, sinks2[l], gam_p,
                                      rot_a_p, rot_b_p, dmat, qdec_p, kdec_p)
        kp_l.append(kl)
        vp_l.append(vl)
        sp_l.append(S)

    kv_p = (DEPTH, N, WINDOW, A_KV_HEADS, A_HEAD_DIM)
    kv_t = (DEPTH, nb, A_KV_HEADS, A_HEAD_DIM, w_buf)
    return (xp, xs.reshape(nb, dec, D),
            jnp.stack(kp_l).reshape(kv_p), jnp.stack(vp_l).reshape(kv_p), jnp.stack(sp_l),
            ks.reshape(kv_t).transpose(0, 1, 4, 2, 3), vs.reshape(kv_t).transpose(0, 1, 4, 2, 3), ss)
```

```python
import functools

import jax
import jax.numpy as jnp
from jax import lax
from jax.experimental import pallas as pl
from jax.experimental.pallas import tpu as pltpu

D_MODEL = 1024
DEPTH = 4
PAST_LEN = 8192
A_HEADS = 8
A_KV_HEADS = 2
A_GROUP = A_HEADS // A_KV_HEADS
A_HEAD_DIM = 64
WINDOW = 128
A_ROT_DIM = A_HEAD_DIM // 4
ROPE_THETA_A = 500000.0
B_HEADS = 4
B_DK = 128
B_DV = 128
RET_CHUNK = 128
ROPE_THETA_B = 10000.0
EPS = 1e-6

A_WIDTH = A_HEADS * A_HEAD_DIM
A_KV_WIDTH = A_KV_HEADS * A_HEAD_DIM
B_QK_WIDTH = B_HEADS * B_DK
B_V_WIDTH = B_HEADS * B_DV
MIX_WIDTH = A_WIDTH + B_V_WIDTH
OFF_QA = 0
OFF_KA = OFF_QA + A_WIDTH
OFF_VA = OFF_KA + A_KV_WIDTH
OFF_GA = OFF_VA + A_KV_WIDTH
OFF_QB = OFF_GA + A_WIDTH
OFF_KB = OFF_QB + B_QK_WIDTH
OFF_VB = OFF_KB + B_QK_WIDTH
OFF_GB = OFF_VB + B_V_WIDTH
IN_WIDTH = OFF_GB + B_V_WIDTH

LANES = 128
HEADS_PER_TILE = LANES // A_HEAD_DIM
A_TILES = A_WIDTH // LANES
NEG = -1e30
LOG2E = 1.4426950408889634

PROMPT_BLOCK = 512
SAMPLE_SEQS = 8
VMEM_LIMIT = 56 * 1024 * 1024

F32 = jnp.float32
BF16 = jnp.bfloat16


def _dot(a, b):
    return jnp.dot(a, b, preferred_element_type=F32)


def _dot_nt(a, b):
    return lax.dot_general(a, b, (((1,), (1,)), ((), ())), preferred_element_type=F32)


def _dot_tn(a, b):
    return lax.dot_general(a, b, (((0,), (0,)), ((), ())), preferred_element_type=F32)


def _silu(x):
    return x * (1.0 / (1.0 + jnp.exp(-x)))


def _row_rms(x):
    n = x.shape[-1]
    return x * lax.rsqrt(jnp.sum(x * x, axis=-1, keepdims=True) * (1.0 / n) + EPS)


def _pair_norm_rot(blk, g_row, ca, sa1, sa2, lo_mask):
    sq = blk * blk
    tot = jnp.sum(sq, axis=-1, keepdims=True)
    lo = jnp.sum(jnp.where(lo_mask, sq, 0.0), axis=-1, keepdims=True)
    hi = tot - lo
    inv_lo = lax.rsqrt(lo * (1.0 / A_HEAD_DIM) + EPS)
    inv_hi = lax.rsqrt(hi * (1.0 / A_HEAD_DIM) + EPS)
    xn = blk * jnp.where(lo_mask, inv_lo, inv_hi) * g_row
    half = A_ROT_DIM // 2
    return xn * ca + pltpu.roll(xn, LANES - half, 1) * sa1 + pltpu.roll(xn, half, 1) * sa2


def _rot_b(blk, cb, sb):
    return blk * cb + pltpu.roll(blk, B_DK // 2, 1) * sb


def _softmax_parts(parts, sink):
    mx = sink
    for s in parts:
        mx = jnp.maximum(mx, jnp.max(s, axis=-1, keepdims=True))
    ps = [jnp.exp2(s - mx) for s in parts]
    den = jnp.exp2(sink - mx)
    for p in ps:
        den = den + jnp.sum(p, axis=-1, keepdims=True)
    rinv = 1.0 / den
    return [p * rinv for p in ps]


def _prompt_kernel(sinks_ref, gam_ref, x_ref, xres_ref, win_ref, wout_ref, ng_ref, gq_ref, gk_ref,
                   ca_ref, sa1_ref, sa2_ref, cb_ref, sb_ref, dmat_ref, qdec_ref, kdec_ref,
                   y_ref, klast_ref, vlast_ref, S_ref,
                   q_s, kz_s, vz_s, qb_s, kb_s, kbd_s, vb_s, g_s, mix_s, bias_s, s_s, p_s, es_s,
                   sc_s, kv_s, sb_s, *, tm, nt, nblk_parity):
    g = pl.program_id(0)
    last = pl.num_programs(0) - 1
    nchunk = tm // RET_CHUNK
    C = RET_CHUNK
    lo_c = lax.broadcasted_iota(jnp.int32, (C, LANES), 1) < A_HEAD_DIM
    t_fin = jnp.maximum(g - 1, 0) % nt

    @pl.when(g == 0)
    def _():
        qi = lax.broadcasted_iota(jnp.int32, (C, 2 * C), 0)
        kj = lax.broadcasted_iota(jnp.int32, (C, 2 * C), 1)
        bias_s[...] = jnp.where((kj > qi) & (kj <= qi + C), 0.0, NEG)

    @pl.when(t_fin == 0)
    def _():
        S_ref[...] = jnp.zeros_like(S_ref)

    def tile(a, i):
        return a[:, i * LANES:(i + 1) * LANES]

    def body(w, r):
        lo_mask = lax.broadcasted_iota(jnp.int32, (tm, LANES), 1) < A_HEAD_DIM
        ca, sa1, sa2 = ca_ref[...], sa1_ref[...], sa2_ref[...]
        cb, sb = cb_ref[...], sb_ref[...]
        first_bias = jnp.where(t_fin == 0, NEG, 0.0)

        def chunk_rows(c):
            return slice(c * C, (c + 1) * C)

        def window_rows(c):
            return slice(c * C, (c + 2) * C)

        def scores(c):
            for m in range(A_TILES):
                grp = m // (A_GROUP // HEADS_PER_TILE)
                qp = q_s[r, chunk_rows(c), m * LANES:(m + 1) * LANES]
                for half in range(HEADS_PER_TILE):
                    s = _dot_nt(qp, kz_s[r, 2 * grp + half, window_rows(c), :]) + bias_s[...]
                    if c == 0:
                        s = jnp.concatenate([s[:, :C] + first_bias, s[:, C:]], axis=1)
                    s_s[c * A_HEADS + m * HEADS_PER_TILE + half] = s

        def softmax(c):
            for m in range(A_TILES):
                es = []
                for half in range(HEADS_PER_TILE):
                    hd = m * HEADS_PER_TILE + half
                    s = s_s[c * A_HEADS + hd]
                    sink = sinks_ref[hd]
                    mx = jnp.maximum(jnp.max(s, axis=-1, keepdims=True), sink)
                    p_s[c * A_HEADS + hd] = jnp.exp2(s - mx).astype(BF16)
                    es.append(jnp.exp2(sink - mx))
                es_s[c * A_TILES + m] = jnp.where(lo_c, es[0], es[1])

        def attend(c):
            for m in range(A_TILES):
                grp = m // (A_GROUP // HEADS_PER_TILE)
                sl = slice(m * LANES, (m + 1) * LANES)
                oe = _dot(p_s[c * A_HEADS + 2 * m], vz_s[r, 2 * grp, window_rows(c), :])
                oo = _dot(p_s[c * A_HEADS + 2 * m + 1], vz_s[r, 2 * grp + 1, window_rows(c), :])
                pv = jnp.where(lo_c, oe, oo)
                den = pltpu.roll(jnp.where(lo_c, oo, oe), A_HEAD_DIM, 1) + es_s[c * A_TILES + m]
                mix_s[chunk_rows(c), sl] = (g_s[r, chunk_rows(c), sl] * pv * (1.0 / den)).astype(BF16)

        def retain_local(c):
            for hh in range(B_HEADS):
                sl = slice(hh * LANES, (hh + 1) * LANES)
                rows = chunk_rows(c)
                sc_s[c * B_HEADS + hh] = (_dot_nt(qb_s[r, rows, sl], kb_s[r, rows, sl]) * dmat_ref[hh]).astype(BF16)
                kv_s[c * B_HEADS + hh] = _dot_tn(kbd_s[r, rows, sl], vb_s[r, rows, sl])

        def retain_states():
            for hh in range(B_HEADS):
                S = S_ref[0, hh]
                for c in range(nchunk):
                    sb_s[c * B_HEADS + hh] = S.astype(BF16)
                    S = gam_ref[hh] * S + kv_s[c * B_HEADS + hh]
                S_ref[0, hh] = S

        def retain_out(c):
            for hh in range(B_HEADS):
                sl = slice(hh * LANES, (hh + 1) * LANES)
                rows = chunk_rows(c)
                i = c * B_HEADS + hh
                ret = _dot(sc_s[i], vb_s[r, rows, sl]) + _dot(qb_s[r, rows, sl], sb_s[i]) * qdec_ref[hh]
                osl = slice(A_WIDTH + hh * LANES, A_WIDTH + (hh + 1) * LANES)
                mix_s[rows, osl] = (g_s[r, rows, osl] * _row_rms(ret)).astype(BF16)

        def project_attention():
            qa = _dot(h, win_ref[:, OFF_QA:OFF_QA + A_WIDTH])
            for m in range(A_TILES):
                q_s[w, :, m * LANES:(m + 1) * LANES] = _pair_norm_rot(
                    tile(qa, m), gq_ref[...], ca, sa1, sa2, lo_mask).astype(BF16)
            kv = _dot(h, win_ref[:, OFF_KA:OFF_KA + 2 * A_KV_WIDTH])
            kr = _pair_norm_rot(tile(kv, 0), gk_ref[...], ca, sa1, sa2, lo_mask)
            va = tile(kv, 1)
            klast_ref[0] = kr[tm - WINDOW:, :]
            vlast_ref[0] = va[tm - WINDOW:, :]
            for src, dst, fill in ((kr, kz_s, 0.0), (va, vz_s, 1.0)):
                sw = pltpu.roll(src, A_HEAD_DIM, 1)
                dst[w, 0, C:, :] = jnp.where(lo_mask, src, fill).astype(BF16)
                dst[w, 1, C:, :] = jnp.where(lo_mask, fill, sw).astype(BF16)
                dst[w, 2, C:, :] = jnp.where(lo_mask, sw, fill).astype(BF16)
                dst[w, 3, C:, :] = jnp.where(lo_mask, fill, src).astype(BF16)
                for i in range(2 * A_KV_HEADS):
                    if r is None:
                        dst[w, i, 0:C, :] = jnp.zeros((C, LANES), BF16)
                    else:
                        dst[w, i, 0:C, :] = dst[r, i, tm:tm + C, :]

        def project_gate_a():
            ga = _dot(h, win_ref[:, OFF_GA:OFF_GA + A_WIDTH])
            for m in range(A_TILES):
                g_s[w, :, m * LANES:(m + 1) * LANES] = _silu(tile(ga, m))

        def project_qb():
            qb = _dot(h, win_ref[:, OFF_QB:OFF_QB + B_QK_WIDTH])
            for hh in range(B_HEADS):
                qb_s[w, :, hh * LANES:(hh + 1) * LANES] = _rot_b(tile(qb, hh), cb, sb).astype(BF16)

        def project_kb():
            kb = _dot(h, win_ref[:, OFF_KB:OFF_KB + B_QK_WIDTH])
            for hh in range(B_HEADS):
                sl = slice(hh * LANES, (hh + 1) * LANES)
                kbh = _rot_b(tile(kb, hh), cb, sb) * (B_DK ** -0.5)
                kb_s[w, :, sl] = kbh.astype(BF16)
                kbd_s[w, :, sl] = (kbh.reshape(nchunk, C, LANES) * kdec_ref[hh][None]).reshape(tm, LANES).astype(BF16)

        def project_vb_gate_b():
            vb_s[w] = _dot(h, win_ref[:, OFF_VB:OFF_VB + B_V_WIDTH]).astype(BF16)
            gb = _dot(h, win_ref[:, OFF_GB:OFF_GB + B_V_WIDTH])
            for hh in range(B_HEADS):
                g_s[w, :, A_WIDTH + hh * LANES:A_WIDTH + (hh + 1) * LANES] = _silu(tile(gb, hh))

        project, finish = w is not None, r is not None
        if project:
            h = (_row_rms(x_ref[0]) * ng_ref[...]).astype(BF16)
        if finish:
            for c in range(nchunk):
                scores(c)
                retain_local(c)
        if project:
            project_attention()
        if finish:
            for c in range(nchunk):
                softmax(c)
            retain_states()
        if project:
            project_gate_a()
            project_qb()
        if finish:
            for c in range(nchunk):
                attend(c)
                retain_out(c)
        if project:
            project_kb()
            project_vb_gate_b()
        if finish:
            y_ref[0] = xres_ref[0] + _dot(mix_s[...], wout_ref[...])

    steady = jnp.logical_and(g > 0, g < last)

    @pl.when(g == 0)
    def _():
        body(0, None)

    @pl.when(jnp.logical_and(steady, g % 2 == 0))
    def _():
        body(0, 1)

    @pl.when(jnp.logical_and(steady, g % 2 == 1))
    def _():
        body(1, 0)

    @pl.when(g == last)
    def _():
        body(None, nblk_parity)


def _const_spec(shape):
    nd = len(shape)
    return pl.BlockSpec(shape, lambda *_: (0,) * nd, pipeline_mode=pl.Buffered(1))


def _layer_spec(shape, l):
    nd = len(shape)
    return pl.BlockSpec((None,) + tuple(shape), lambda *_: (l,) + (0,) * nd, pipeline_mode=pl.Buffered(1))


def _prompt_layer(l, x, win, wout, ng, gq, gk, sinks, gam, rot_a, rot_b, dmat, qdec, kdec):
    N, T, D = x.shape
    tm = PROMPT_BLOCK
    assert T % tm == 0 and tm % RET_CHUNK == 0 and WINDOW == RET_CHUNK
    nt = T // tm
    nblk = N * nt
    nchunk = tm // RET_CHUNK

    def proj_blk(g):
        return jnp.minimum(g, nblk - 1)

    def fin_blk(g):
        return jnp.maximum(g - 1, 0)

    smem = pl.BlockSpec(memory_space=pltpu.SMEM)
    tab = pl.BlockSpec((tm, LANES), lambda g: (proj_blk(g) % nt, 0))
    x_proj = pl.BlockSpec((1, tm, D), lambda g: (proj_blk(g) // nt, proj_blk(g) % nt, 0))
    x_fin = pl.BlockSpec((1, tm, D), lambda g: (fin_blk(g) // nt, fin_blk(g) % nt, 0))
    kv_out = pl.BlockSpec((1, WINDOW, LANES), lambda g: (proj_blk(g) // nt, 0, 0))

    def two(*shape, dtype):
        return pltpu.VMEM((2,) + shape, dtype)

    return pl.pallas_call(
        functools.partial(_prompt_kernel, tm=tm, nt=nt, nblk_parity=(nblk - 1) % 2),
        grid=(nblk + 1,),
        in_specs=[smem, smem, x_proj, x_fin,
                  _layer_spec((D, IN_WIDTH), l), _layer_spec((MIX_WIDTH, D), l),
                  _layer_spec((1, D), l), _layer_spec((1, LANES), l), _layer_spec((1, LANES), l),
                  tab, tab, tab, tab, tab,
                  _const_spec((B_HEADS, RET_CHUNK, RET_CHUNK)), _const_spec((B_HEADS, RET_CHUNK, LANES)),
                  _const_spec((B_HEADS, RET_CHUNK, LANES))],
        out_specs=[x_fin, kv_out, kv_out,
                   pl.BlockSpec((1, B_HEADS, B_DK, B_DV), lambda g: (fin_blk(g) // nt, 0, 0, 0))],
        out_shape=[jax.ShapeDtypeStruct((N, T, D), F32),
                   jax.ShapeDtypeStruct((N, WINDOW, LANES), F32),
                   jax.ShapeDtypeStruct((N, WINDOW, LANES), F32),
                   jax.ShapeDtypeStruct((N, B_HEADS, B_DK, B_DV), F32)],
        scratch_shapes=[two(tm, A_WIDTH, dtype=BF16),
                        two(2 * A_KV_HEADS, RET_CHUNK + tm, LANES, dtype=BF16),
                        two(2 * A_KV_HEADS, RET_CHUNK + tm, LANES, dtype=BF16),
                        two(tm, B_QK_WIDTH, dtype=BF16),
                        two(tm, B_QK_WIDTH, dtype=BF16),
                        two(tm, B_QK_WIDTH, dtype=BF16),
                        two(tm, B_V_WIDTH, dtype=BF16),
                        two(tm, MIX_WIDTH, dtype=F32),
                        pltpu.VMEM((tm, MIX_WIDTH), BF16),
                        pltpu.VMEM((RET_CHUNK, 2 * RET_CHUNK), F32),
                        pltpu.VMEM((nchunk * A_HEADS, RET_CHUNK, 2 * RET_CHUNK), F32),
                        pltpu.VMEM((nchunk * A_HEADS, RET_CHUNK, 2 * RET_CHUNK), BF16),
                        pltpu.VMEM((nchunk * A_TILES, RET_CHUNK, LANES), F32),
                        pltpu.VMEM((nchunk * B_HEADS, RET_CHUNK, RET_CHUNK), BF16),
                        pltpu.VMEM((nchunk * B_HEADS, B_DK, B_DV), F32),
                        pltpu.VMEM((nchunk * B_HEADS, B_DK, B_DV), BF16)],
        compiler_params=pltpu.CompilerParams(dimension_semantics=("arbitrary",),
                                             vmem_limit_bytes=VMEM_LIMIT),
        name="prompt_layer",
    )(sinks, gam, x, x, win, wout, ng, gq, gk, *rot_a, *rot_b, dmat, qdec, kdec)


def _sample_kernel(gam_ref, x_ref, win_ref, wout_ref, ng_ref, gq_ref, gk_ref,
                   ca_ref, sa1_ref, sa2_ref, cb_ref, sb_ref, sink_ref, d8_ref, qdec_ref, kdec_ref,
                   ck_ref, cv_ref, st_ref,
                   y_ref, ko_ref, vo_ref, so_ref,
                   xs_s, qz_s, kn_s, vn_s, knt_s, vnt_s, qb_s, kb_s, vb_s, g_s, mix_s, *, nseq, dec):
    layer = pl.program_id(0)
    step = pl.program_id(1)
    ntok = x_ref.shape[0]
    pair_rows = 2 * dec
    npair = nseq // 2

    @pl.when(jnp.logical_and(layer == 0, step == 0))
    def _():
        xs_s[...] = x_ref[...]

    @pl.when(step == 0)
    def _():
        h = (_row_rms(xs_s[...]) * ng_ref[...]).astype(BF16)
        lo_mask = lax.broadcasted_iota(jnp.int32, (ntok, LANES), 1) < A_HEAD_DIM
        ca, sa1, sa2 = ca_ref[...], sa1_ref[...], sa2_ref[...]
        cb, sb = cb_ref[...], sb_ref[...]

        def tile(a, i):
            return a[:, i * LANES:(i + 1) * LANES]

        qa = _dot(h, win_ref[:, OFF_QA:OFF_QA + A_WIDTH])
        for m in range(A_TILES):
            qr = _pair_norm_rot(tile(qa, m), gq_ref[...], ca, sa1, sa2, lo_mask)
            sw = pltpu.roll(qr, A_HEAD_DIM, 1)
            if m < A_TILES // 2:
                qz_s[2 * m] = jnp.where(lo_mask, qr, 0.0)
                qz_s[2 * m + 1] = jnp.where(lo_mask, sw, 0.0)
            else:
                qz_s[2 * m] = jnp.where(lo_mask, 0.0, sw)
                qz_s[2 * m + 1] = jnp.where(lo_mask, 0.0, qr)
        kv = _dot(h, win_ref[:, OFF_KA:OFF_KA + 2 * A_KV_WIDTH])
        kn = _pair_norm_rot(tile(kv, 0), gk_ref[...], ca, sa1, sa2, lo_mask)
        vn = tile(kv, 1)
        kn_s[...] = kn
        vn_s[...] = vn
        for tt in range(ntok // LANES):
            knt_s[tt] = kn[tt * LANES:(tt + 1) * LANES, :].T
            vnt_s[tt] = vn[tt * LANES:(tt + 1) * LANES, :].T
        ga = _dot(h, win_ref[:, OFF_GA:OFF_GA + A_WIDTH])
        for m in range(A_TILES):
            g_s[:, m * LANES:(m + 1) * LANES] = _silu(tile(ga, m))
        qb = _dot(h, win_ref[:, OFF_QB:OFF_QB + B_QK_WIDTH])
        for hh in range(B_HEADS):
            qb_s[:, hh * LANES:(hh + 1) * LANES] = _rot_b(tile(qb, hh), cb, sb)
        kb = _dot(h, win_ref[:, OFF_KB:OFF_KB + B_QK_WIDTH])
        for hh in range(B_HEADS):
            kb_s[:, hh * LANES:(hh + 1) * LANES] = _rot_b(tile(kb, hh), cb, sb) * (B_DK ** -0.5)
        vb_s[...] = _dot(h, win_ref[:, OFF_VB:OFF_VB + B_V_WIDTH])
        gb = _dot(h, win_ref[:, OFF_GB:OFF_GB + B_V_WIDTH])
        for hh in range(B_HEADS):
            g_s[:, A_WIDTH + hh * LANES:A_WIDTH + (hh + 1) * LANES] = _silu(tile(gb, hh))

    nq = A_HEADS * pair_rows
    row = lax.broadcasted_iota(jnp.int32, (nq, LANES), 0)
    slot = lax.broadcasted_iota(jnp.int32, (nq, LANES), 1)
    first_seq = (row % pair_rows) < dec
    cache_mask = slot > (row % dec)
    rown = lax.broadcasted_iota(jnp.int32, (nq, pair_rows), 0)
    coln = lax.broadcasted_iota(jnp.int32, (nq, pair_rows), 1)
    new_mask = ((coln // dec) == ((rown % pair_rows) // dec)) & ((coln % dec) <= (rown % dec))
    lo8 = lax.broadcasted_iota(jnp.int32, (pair_rows, LANES), 1) < A_HEAD_DIM
    r8 = lax.broadcasted_iota(jnp.int32, (pair_rows, LANES), 0)
    keep = lax.broadcasted_iota(jnp.int32, (LANES, WINDOW), 1) < WINDOW - dec
    sink = sink_ref[...]

    def pair_rows_of(p):
        seq0 = step * nseq + 2 * p
        return pl.ds(pl.multiple_of(seq0 * dec, pair_rows), pair_rows)


    stage1 = []
    for p in range(npair):
        b0 = 2 * p
        rows = pair_rows_of(p)
        qz = qz_s[:, rows, :].reshape(nq, LANES).astype(BF16)
        s_c = jnp.where(first_seq, _dot(qz, ck_ref[b0].astype(BF16)), _dot(qz, ck_ref[b0 + 1].astype(BF16)))
        s_n = _dot_nt(qz, kn_s[rows, :].astype(BF16))
        rets = []
        for hh in range(B_HEADS):
            sl = slice(hh * LANES, (hh + 1) * LANES)
            q = qb_s[rows, sl].astype(BF16)
            kf = kb_s[rows, sl]
            kd = kf * kdec_ref[hh]
            v = vb_s[rows, sl].astype(BF16)
            sc = (_dot_nt(q, kf.astype(BF16)) * d8_ref[hh]).astype(BF16)
            crosses = []
            for j in range(2):
                S = st_ref[b0 + j, hh]
                crosses.append(_dot(q, S.astype(BF16)))
                kdj = jnp.where((r8 // dec) == j, kd, 0.0).astype(BF16)
                so_ref[b0 + j, hh] = gam_ref[hh] * S + _dot_tn(kdj, v)
            cross = jnp.where(r8 < dec, crosses[0], crosses[1]) * qdec_ref[hh]
            rets.append((sc, v, cross))
        stage1.append((s_c, s_n, rets))

    probs = []
    for s_c, s_n, _ in stage1:
        p_c, p_n = _softmax_parts([jnp.where(cache_mask, s_c, NEG), jnp.where(new_mask, s_n, NEG)], sink)
        probs.append((p_c.astype(BF16), p_n.astype(BF16)))

    for p in range(npair):
        b0 = 2 * p
        rows = pair_rows_of(p)
        p_c, p_n = probs[p]
        o = jnp.where(first_seq, _dot_nt(p_c, cv_ref[b0].astype(BF16)), _dot_nt(p_c, cv_ref[b0 + 1].astype(BF16)))
        o = o + _dot(p_n, vn_s[rows, :].astype(BF16))
        for m in range(A_TILES):
            oa = o[(2 * m) * pair_rows:(2 * m + 1) * pair_rows, :]
            ob = o[(2 * m + 1) * pair_rows:(2 * m + 2) * pair_rows, :]
            if m < A_TILES // 2:
                blk = jnp.where(lo8, oa, pltpu.roll(ob, A_HEAD_DIM, 1))
            else:
                blk = jnp.where(lo8, pltpu.roll(oa, A_HEAD_DIM, 1), ob)
            sl = slice(m * LANES, (m + 1) * LANES)
            mix_s[rows, sl] = g_s[rows, sl] * blk
        for hh, (sc, v, cross) in enumerate(stage1[p][2]):
            osl = slice(A_WIDTH + hh * LANES, A_WIDTH + (hh + 1) * LANES)
            mix_s[rows, osl] = g_s[rows, osl] * _row_rms(_dot(sc, v) + cross)

    for b in range(nseq):
        tok0 = (step * nseq + b) * dec
        tt = tok0 // LANES
        shift = (WINDOW - dec) - tok0 % LANES
        ko_ref[b] = jnp.where(keep, pltpu.roll(ck_ref[b], WINDOW - dec, 1), pltpu.roll(knt_s[tt], shift, 1))
        vo_ref[b] = jnp.where(keep, pltpu.roll(cv_ref[b], WINDOW - dec, 1), pltpu.roll(vnt_s[tt], shift, 1))

    @pl.when(step == pl.num_programs(1) - 1)
    def _():
        xs_s[...] = xs_s[...] + _dot(mix_s[...].astype(BF16), wout_ref[...])

    @pl.when(jnp.logical_and(layer == pl.num_programs(0) - 1, step == pl.num_programs(1) - 1))
    def _():
        y_ref[...] = xs_s[...]


def _stacked_spec(shape):
    nd = len(shape)
    return pl.BlockSpec((None,) + tuple(shape), lambda l, i: (l,) + (0,) * nd)


def _sample_layers(x, win, wout, ng, gq, gk, sink_col, gam, rot_a, rot_b, d8, qdec, kdec, ck, cv, st, dec):
    ntok, D = x.shape
    depth, nb = ck.shape[:2]
    nseq = SAMPLE_SEQS
    assert nb % nseq == 0 and nseq % 2 == 0 and ntok == nb * dec and 2 * dec == 8 and ntok % LANES == 0
    assert LANES % dec == 0 and WINDOW == LANES
    pair_rows = 2 * dec
    nq = A_HEADS * pair_rows
    smem = pl.BlockSpec(memory_space=pltpu.SMEM)
    kv = pl.BlockSpec((None, nseq, A_KV_WIDTH, WINDOW), lambda l, i: (l, i, 0, 0))
    stt = pl.BlockSpec((None, nseq, B_HEADS, B_DK, B_DV), lambda l, i: (l, i, 0, 0, 0))
    tab = _const_spec((ntok, LANES))
    return pl.pallas_call(
        functools.partial(_sample_kernel, nseq=nseq, dec=dec),
        grid=(depth, nb // nseq),
        in_specs=[smem, _const_spec((ntok, D)), _stacked_spec((D, IN_WIDTH)), _stacked_spec((MIX_WIDTH, D)),
                  _stacked_spec((1, D)), _stacked_spec((1, LANES)), _stacked_spec((1, LANES)),
                  tab, tab, tab, tab, tab,
                  _stacked_spec((nq, 1)), _const_spec((B_HEADS, pair_rows, pair_rows)),
                  _const_spec((B_HEADS, pair_rows, LANES)), _const_spec((B_HEADS, pair_rows, LANES)),
                  kv, kv, stt],
        out_specs=[pl.BlockSpec((ntok, D), lambda l, i: (0, 0)), kv, kv, stt],
        out_shape=[jax.ShapeDtypeStruct((ntok, D), F32),
                   jax.ShapeDtypeStruct(ck.shape, F32),
                   jax.ShapeDtypeStruct(cv.shape, F32),
                   jax.ShapeDtypeStruct(st.shape, F32)],
        scratch_shapes=[pltpu.VMEM((ntok, D), F32),
                        pltpu.VMEM((A_HEADS, ntok, LANES), F32),
                        pltpu.VMEM((ntok, LANES), F32),
                        pltpu.VMEM((ntok, LANES), F32),
                        pltpu.VMEM((ntok // LANES, A_KV_WIDTH, LANES), F32),
                        pltpu.VMEM((ntok // LANES, A_KV_WIDTH, LANES), F32),
                        pltpu.VMEM((ntok, B_QK_WIDTH), F32),
                        pltpu.VMEM((ntok, B_QK_WIDTH), F32),
                        pltpu.VMEM((ntok, B_V_WIDTH), F32),
                        pltpu.VMEM((ntok, MIX_WIDTH), F32),
                        pltpu.VMEM((ntok, MIX_WIDTH), F32)],
        compiler_params=pltpu.CompilerParams(dimension_semantics=("arbitrary", "arbitrary"),
                                             vmem_limit_bytes=VMEM_LIMIT),
        name="sample_layers",
    )(gam, x, win, wout, ng, gq, gk, *rot_a, *rot_b, sink_col, d8, qdec, kdec, ck, cv, st)


def _rot_tables_a(pos):
    half = A_ROT_DIM // 2
    inv_freq = ROPE_THETA_A ** (-jnp.arange(half, dtype=F32) / half)
    ang = pos.astype(F32)[:, None] * inv_freq[None, :]
    cos, sin = jnp.cos(ang), jnp.sin(ang)
    n = pos.shape[0]
    zh = jnp.zeros((n, half), F32)
    rest = A_HEAD_DIM - A_ROT_DIM
    c = jnp.concatenate([cos, cos, jnp.ones((n, rest), F32)], axis=-1)
    s1 = jnp.concatenate([-sin, zh, jnp.zeros((n, rest), F32)], axis=-1)
    s2 = jnp.concatenate([zh, sin, jnp.zeros((n, rest), F32)], axis=-1)
    return tuple(jnp.tile(a, (1, HEADS_PER_TILE)) for a in (c, s1, s2))


def _rot_tables_b(pos):
    half = B_DK // 2
    inv_freq = ROPE_THETA_B ** (-jnp.arange(half, dtype=F32) / half)
    ang = pos.astype(F32)[:, None] * inv_freq[None, :]
    cos, sin = jnp.cos(ang), jnp.sin(ang)
    return jnp.concatenate([cos, cos], axis=-1), jnp.concatenate([-sin, sin], axis=-1)


def _log_decay():
    return jnp.log(1.0 - 2.0 ** (-5.0 - jnp.arange(B_HEADS, dtype=F32)))


def _decay_tables(C):
    lg = _log_decay()
    idx = jnp.arange(C, dtype=F32)
    diff = idx[:, None] - idx[None, :]
    dmat = jnp.where(diff >= 0, jnp.exp(lg[:, None, None] * jnp.maximum(diff, 0.0)), 0.0)
    q_dec = jnp.exp(lg[None, :] * (idx[:, None] + 1.0))
    k_dec = jnp.exp(lg[None, :] * (C - 1.0 - idx[:, None]))
    gam = jnp.exp(lg * C)
    return dmat, q_dec, k_dec, gam


def kernel(x_prompt, x_sample, cache_swa_k, cache_swa_v, state_ret, w_in, w_out, norm_g, q_norm_g, k_norm_g, sinks):
    N, T, D = x_prompt.shape
    nb, dec, _ = x_sample.shape
    w_buf = cache_swa_k.shape[2]
    assert w_buf == WINDOW

    pos_p = jnp.arange(T, dtype=jnp.int32)
    pos_s = PAST_LEN + jnp.arange(dec, dtype=jnp.int32)
    rot_a_p, rot_b_p = _rot_tables_a(pos_p), _rot_tables_b(pos_p)
    rot_a_s = tuple(jnp.tile(a, (nb, 1)) for a in _rot_tables_a(pos_s))
    rot_b_s = tuple(jnp.tile(a, (nb, 1)) for a in _rot_tables_b(pos_s))

    dmat, q_dec, k_dec, gam_p = _decay_tables(RET_CHUNK)
    qdec_p = jnp.broadcast_to(q_dec.T[:, :, None], (B_HEADS, RET_CHUNK, LANES))
    kdec_p = jnp.broadcast_to(k_dec.T[:, :, None], (B_HEADS, RET_CHUNK, LANES))
    dmat4, q_dec4, k_dec4, gam_s = _decay_tables(dec)
    d8 = jnp.kron(jnp.eye(2, dtype=F32)[None], jnp.ones((1, dec, dec), F32)) * jnp.tile(dmat4, (1, 2, 2))
    qdec_s = jnp.broadcast_to(jnp.tile(q_dec4.T, (1, 2))[:, :, None], (B_HEADS, 2 * dec, LANES))
    kdec_s = jnp.broadcast_to(jnp.tile(k_dec4.T, (1, 2))[:, :, None], (B_HEADS, 2 * dec, LANES))

    w_in_b = w_in.astype(BF16)
    w_out_b = w_out.astype(BF16)
    ck = cache_swa_k.transpose(0, 1, 3, 4, 2).reshape(DEPTH, nb, A_KV_WIDTH, w_buf)
    cv = cache_swa_v.transpose(0, 1, 3, 4, 2).reshape(DEPTH, nb, A_KV_WIDTH, w_buf)

    ng = norm_g[:, None, :]
    gq = jnp.tile(q_norm_g, (1, HEADS_PER_TILE))[:, None, :] * (A_HEAD_DIM ** -0.5 * LOG2E)
    gk = jnp.tile(k_norm_g, (1, HEADS_PER_TILE))[:, None, :]
    sinks2 = sinks * LOG2E

    sink_col = jnp.repeat(sinks2, 2 * dec, axis=1)[:, :, None]
    xs, ks, vs, ss = _sample_layers(x_sample.reshape(nb * dec, D), w_in_b, w_out_b, ng, gq, gk, sink_col, gam_s,
                                    rot_a_s, rot_b_s, d8, qdec_s, kdec_s, ck, cv, state_ret, dec)

    xp = x_prompt
    kp_l, vp_l, sp_l = [], [], []
    for l in range(DEPTH):
        xp, kl, vl, S = _prompt_layer(l, xp, w_in_b, w_out_b, ng, gq, gk, sinks2[l], gam_p,
                                      rot_a_p, rot_b_p, dmat, qdec_p, kdec_p)
        kp_l.append(kl)
        vp_l.append(vl)
        sp_l.append(S)

    kv_p = (DEPTH, N, WINDOW, A_KV_HEADS, A_HEAD_DIM)
    kv_t = (DEPTH, nb, A_KV_HEADS, A_HEAD_DIM, w_buf)
    return (xp, xs.reshape(nb, dec, D),
            jnp.stack(kp_l).reshape(kv_p), jnp.stack(vp_l).reshape(kv_p), jnp.stack(sp_l),
            ks.reshape(kv_t).transpose(0, 1, 4, 2, 3), vs.reshape(kv_t).transpose(0, 1, 4, 2, 3), ss)
```

```python
import functools

import jax
import jax.numpy as jnp
from jax import lax
from jax.experimental import pallas as pl
from jax.experimental.pallas import tpu as pltpu

D_MODEL = 1024
DEPTH = 4
PAST_LEN = 8192
A_HEADS = 8
A_KV_HEADS = 2
A_GROUP = A_HEADS // A_KV_HEADS
A_HEAD_DIM = 64
WINDOW = 128
A_ROT_DIM = A_HEAD_DIM // 4
ROPE_THETA_A = 500000.0
B_HEADS = 4
B_DK = 128
B_DV = 128
RET_CHUNK = 128
ROPE_THETA_B = 10000.0
EPS = 1e-6

A_WIDTH = A_HEADS * A_HEAD_DIM
A_KV_WIDTH = A_KV_HEADS * A_HEAD_DIM
B_QK_WIDTH = B_HEADS * B_DK
B_V_WIDTH = B_HEADS * B_DV
MIX_WIDTH = A_WIDTH + B_V_WIDTH
OFF_QA = 0
OFF_KA = OFF_QA + A_WIDTH
OFF_VA = OFF_KA + A_KV_WIDTH
OFF_GA = OFF_VA + A_KV_WIDTH
OFF_QB = OFF_GA + A_WIDTH
OFF_KB = OFF_QB + B_QK_WIDTH
OFF_VB = OFF_KB + B_QK_WIDTH
OFF_GB = OFF_VB + B_V_WIDTH
IN_WIDTH = OFF_GB + B_V_WIDTH

LANES = 128
HEADS_PER_TILE = LANES // A_HEAD_DIM
A_TILES = A_WIDTH // LANES
NEG = -1e30
LOG2E = 1.4426950408889634

PROMPT_BLOCK = 512
SAMPLE_SEQS = 8
VMEM_LIMIT = 56 * 1024 * 1024

F32 = jnp.float32
BF16 = jnp.bfloat16


def _dot(a, b):
    return jnp.dot(a, b, preferred_element_type=F32)


def _dot_nt(a, b):
    return lax.dot_general(a, b, (((1,), (1,)), ((), ())), preferred_element_type=F32)


def _dot_tn(a, b):
    return lax.dot_general(a, b, (((0,), (0,)), ((), ())), preferred_element_type=F32)


def _silu(x):
    return x * (1.0 / (1.0 + jnp.exp(-x)))


def _row_rms(x):
    n = x.shape[-1]
    return x * lax.rsqrt(jnp.sum(x * x, axis=-1, keepdims=True) * (1.0 / n) + EPS)


def _pair_norm_rot(blk, g_row, ca, sa1, sa2, lo_mask):
    sq = blk * blk
    tot = jnp.sum(sq, axis=-1, keepdims=True)
    lo = jnp.sum(jnp.where(lo_mask, sq, 0.0), axis=-1, keepdims=True)
    hi = tot - lo
    inv_lo = lax.rsqrt(lo * (1.0 / A_HEAD_DIM) + EPS)
    inv_hi = lax.rsqrt(hi * (1.0 / A_HEAD_DIM) + EPS)
    xn = blk * jnp.where(lo_mask, inv_lo, inv_hi) * g_row
    half = A_ROT_DIM // 2
    return xn * ca + pltpu.roll(xn, LANES - half, 1) * sa1 + pltpu.roll(xn, half, 1) * sa2


def _rot_b(blk, cb, sb):
    return blk * cb + pltpu.roll(blk, B_DK // 2, 1) * sb


def _softmax_parts(parts, sink):
    mx = sink
    for s in parts:
        mx = jnp.maximum(mx, jnp.max(s, axis=-1, keepdims=True))
    ps = [jnp.exp2(s - mx) for s in parts]
    den = jnp.exp2(sink - mx)
    for p in ps:
        den = den + jnp.sum(p, axis=-1, keepdims=True)
    rinv = 1.0 / den
    return [p * rinv for p in ps]


def _prompt_kernel(sinks_ref, gam_ref, x_ref, xres_ref, win_ref, wout_ref, ng_ref, gq_ref, gk_ref,
                   ca_ref, sa1_ref, sa2_ref, cb_ref, sb_ref, dmat_ref, qdec_ref, kdec_ref,
                   y_ref, klast_ref, vlast_ref, S_ref,
                   q_s, kz_s, vz_s, qb_s, kb_s, kbd_s, vb_s, g_s, mix_s, bias_s, s_s, p_s, es_s,
                   sc_s, kv_s, sb_s, *, tm, nt):
    g = pl.program_id(0)
    nchunk = tm // RET_CHUNK
    C = RET_CHUNK
    lo_c = lax.broadcasted_iota(jnp.int32, (C, LANES), 1) < A_HEAD_DIM
    t_fin = jnp.maximum(g - 1, 0) % nt

    @pl.when(g == 0)
    def _():
        qi = lax.broadcasted_iota(jnp.int32, (C, 2 * C), 0)
        kj = lax.broadcasted_iota(jnp.int32, (C, 2 * C), 1)
        bias_s[...] = jnp.where((kj > qi) & (kj <= qi + C), 0.0, NEG)
        for ref in (q_s, kz_s, vz_s, qb_s, kb_s, kbd_s, vb_s, g_s):
            ref[1] = jnp.zeros(ref.shape[1:], ref.dtype)

    @pl.when(t_fin == 0)
    def _():
        S_ref[...] = jnp.zeros_like(S_ref)

    def tile(a, i):
        return a[:, i * LANES:(i + 1) * LANES]

    def body(w, r):
        lo_mask = lax.broadcasted_iota(jnp.int32, (tm, LANES), 1) < A_HEAD_DIM
        ca, sa1, sa2 = ca_ref[...], sa1_ref[...], sa2_ref[...]
        cb, sb = cb_ref[...], sb_ref[...]
        first_bias = jnp.where(t_fin == 0, NEG, 0.0)

        def chunk_rows(c):
            return slice(c * C, (c + 1) * C)

        def window_rows(c):
            return slice(c * C, (c + 2) * C)

        def scores(c):
            for m in range(A_TILES):
                grp = m // (A_GROUP // HEADS_PER_TILE)
                qp = q_s[r, chunk_rows(c), m * LANES:(m + 1) * LANES]
                for half in range(HEADS_PER_TILE):
                    s = _dot_nt(qp, kz_s[r, 2 * grp + half, window_rows(c), :]) + bias_s[...]
                    if c == 0:
                        s = jnp.concatenate([s[:, :C] + first_bias, s[:, C:]], axis=1)
                    s_s[c * A_HEADS + m * HEADS_PER_TILE + half] = s

        def softmax(c):
            for m in range(A_TILES):
                es = []
                for half in range(HEADS_PER_TILE):
                    hd = m * HEADS_PER_TILE + half
                    s = s_s[c * A_HEADS + hd]
                    sink = sinks_ref[hd]
                    mx = jnp.maximum(jnp.max(s, axis=-1, keepdims=True), sink)
                    p_s[c * A_HEADS + hd] = jnp.exp2(s - mx).astype(BF16)
                    es.append(jnp.exp2(sink - mx))
                es_s[c * A_TILES + m] = jnp.where(lo_c, es[0], es[1])

        def attend(c):
            for m in range(A_TILES):
                grp = m // (A_GROUP // HEADS_PER_TILE)
                sl = slice(m * LANES, (m + 1) * LANES)
                oe = _dot(p_s[c * A_HEADS + 2 * m], vz_s[r, 2 * grp, window_rows(c), :])
                oo = _dot(p_s[c * A_HEADS + 2 * m + 1], vz_s[r, 2 * grp + 1, window_rows(c), :])
                pv = jnp.where(lo_c, oe, oo)
                den = pltpu.roll(jnp.where(lo_c, oo, oe), A_HEAD_DIM, 1) + es_s[c * A_TILES + m]
                mix_s[chunk_rows(c), sl] = (g_s[r, chunk_rows(c), sl] * pv * (1.0 / den)).astype(BF16)

        def retain_local(c):
            for hh in range(B_HEADS):
                sl = slice(hh * LANES, (hh + 1) * LANES)
                rows = chunk_rows(c)
                sc_s[c * B_HEADS + hh] = (_dot_nt(qb_s[r, rows, sl], kb_s[r, rows, sl]) * dmat_ref[hh]).astype(BF16)
                kv_s[c * B_HEADS + hh] = _dot_tn(kbd_s[r, rows, sl], vb_s[r, rows, sl])

        def retain_states():
            for hh in range(B_HEADS):
                S = S_ref[0, hh]
                for c in range(nchunk):
                    sb_s[c * B_HEADS + hh] = S.astype(BF16)
                    S = gam_ref[hh] * S + kv_s[c * B_HEADS + hh]
                S_ref[0, hh] = S

        def retain_out(c):
            for hh in range(B_HEADS):
                sl = slice(hh * LANES, (hh + 1) * LANES)
                rows = chunk_rows(c)
                i = c * B_HEADS + hh
                ret = _dot(sc_s[i], vb_s[r, rows, sl]) + _dot(qb_s[r, rows, sl], sb_s[i]) * qdec_ref[hh]
                osl = slice(A_WIDTH + hh * LANES, A_WIDTH + (hh + 1) * LANES)
                mix_s[rows, osl] = (g_s[r, rows, osl] * _row_rms(ret)).astype(BF16)

        def project_attention():
            qa = _dot(h, win_ref[:, OFF_QA:OFF_QA + A_WIDTH])
            for m in range(A_TILES):
                q_s[w, :, m * LANES:(m + 1) * LANES] = _pair_norm_rot(
                    tile(qa, m), gq_ref[...], ca, sa1, sa2, lo_mask).astype(BF16)
            kv = _dot(h, win_ref[:, OFF_KA:OFF_KA + 2 * A_KV_WIDTH])
            kr = _pair_norm_rot(tile(kv, 0), gk_ref[...], ca, sa1, sa2, lo_mask)
            va = tile(kv, 1)
            klast_ref[0] = kr[tm - WINDOW:, :]
            vlast_ref[0] = va[tm - WINDOW:, :]
            for src, dst, fill in ((kr, kz_s, 0.0), (va, vz_s, 1.0)):
                sw = pltpu.roll(src, A_HEAD_DIM, 1)
                dst[w, 0, C:, :] = jnp.where(lo_mask, src, fill).astype(BF16)
                dst[w, 1, C:, :] = jnp.where(lo_mask, fill, sw).astype(BF16)
                dst[w, 2, C:, :] = jnp.where(lo_mask, sw, fill).astype(BF16)
                dst[w, 3, C:, :] = jnp.where(lo_mask, fill, src).astype(BF16)
                for i in range(2 * A_KV_HEADS):
                    dst[w, i, 0:C, :] = dst[r, i, tm:tm + C, :]

        def project_gate_a():
            ga = _dot(h, win_ref[:, OFF_GA:OFF_GA + A_WIDTH])
            for m in range(A_TILES):
                g_s[w, :, m * LANES:(m + 1) * LANES] = _silu(tile(ga, m))

        def project_qb():
            qb = _dot(h, win_ref[:, OFF_QB:OFF_QB + B_QK_WIDTH])
            for hh in range(B_HEADS):
                qb_s[w, :, hh * LANES:(hh + 1) * LANES] = _rot_b(tile(qb, hh), cb, sb).astype(BF16)

        def project_kb():
            kb = _dot(h, win_ref[:, OFF_KB:OFF_KB + B_QK_WIDTH])
            for hh in range(B_HEADS):
                sl = slice(hh * LANES, (hh + 1) * LANES)
                kbh = _rot_b(tile(kb, hh), cb, sb) * (B_DK ** -0.5)
                kb_s[w, :, sl] = kbh.astype(BF16)
                kbd_s[w, :, sl] = (kbh.reshape(nchunk, C, LANES) * kdec_ref[hh][None]).reshape(tm, LANES).astype(BF16)

        def project_vb_gate_b():
            vb_s[w] = _dot(h, win_ref[:, OFF_VB:OFF_VB + B_V_WIDTH]).astype(BF16)
            gb = _dot(h, win_ref[:, OFF_GB:OFF_GB + B_V_WIDTH])
            for hh in range(B_HEADS):
                g_s[w, :, A_WIDTH + hh * LANES:A_WIDTH + (hh + 1) * LANES] = _silu(tile(gb, hh))

        h = (_row_rms(x_ref[0]) * ng_ref[...]).astype(BF16)
        for c in range(nchunk):
            scores(c)
            retain_local(c)
        project_attention()
        for c in range(nchunk):
            softmax(c)
        retain_states()
        project_gate_a()
        project_qb()
        for c in range(nchunk):
            attend(c)
            retain_out(c)
        project_kb()
        project_vb_gate_b()
        y_ref[0] = xres_ref[0] + _dot(mix_s[...], wout_ref[...])

    @pl.when(g % 2 == 0)
    def _():
        body(0, 1)

    @pl.when(g % 2 == 1)
    def _():
        body(1, 0)


def _const_spec(shape):
    nd = len(shape)
    return pl.BlockSpec(shape, lambda *_: (0,) * nd, pipeline_mode=pl.Buffered(1))


def _layer_spec(shape, l):
    nd = len(shape)
    return pl.BlockSpec((None,) + tuple(shape), lambda *_: (l,) + (0,) * nd, pipeline_mode=pl.Buffered(1))


def _prompt_layer(l, x, win, wout, ng, gq, gk, sinks, gam, rot_a, rot_b, dmat, qdec, kdec):
    N, T, D = x.shape
    tm = PROMPT_BLOCK
    assert T % tm == 0 and tm % RET_CHUNK == 0 and WINDOW == RET_CHUNK
    nt = T // tm
    nblk = N * nt
    nchunk = tm // RET_CHUNK

    def proj_blk(g):
        return jnp.minimum(g, nblk - 1)

    def fin_blk(g):
        return jnp.maximum(g - 1, 0)

    smem = pl.BlockSpec(memory_space=pltpu.SMEM)
    tab = pl.BlockSpec((tm, LANES), lambda g: (proj_blk(g) % nt, 0))
    x_proj = pl.BlockSpec((1, tm, D), lambda g: (proj_blk(g) // nt, proj_blk(g) % nt, 0))
    x_fin = pl.BlockSpec((1, tm, D), lambda g: (fin_blk(g) // nt, fin_blk(g) % nt, 0))
    kv_out = pl.BlockSpec((1, WINDOW, LANES), lambda g: (proj_blk(g) // nt, 0, 0))

    def two(*shape, dtype):
        return pltpu.VMEM((2,) + shape, dtype)

    return pl.pallas_call(
        functools.partial(_prompt_kernel, tm=tm, nt=nt),
        grid=(nblk + 1,),
        in_specs=[smem, smem, x_proj, x_fin,
                  _layer_spec((D, IN_WIDTH), l), _layer_spec((MIX_WIDTH, D), l),
                  _layer_spec((1, D), l), _layer_spec((1, LANES), l), _layer_spec((1, LANES), l),
                  tab, tab, tab, tab, tab,
                  _const_spec((B_HEADS, RET_CHUNK, RET_CHUNK)), _const_spec((B_HEADS, RET_CHUNK, LANES)),
                  _const_spec((B_HEADS, RET_CHUNK, LANES))],
        out_specs=[x_fin, kv_out, kv_out,
                   pl.BlockSpec((1, B_HEADS, B_DK, B_DV), lambda g: (fin_blk(g) // nt, 0, 0, 0))],
        out_shape=[jax.ShapeDtypeStruct((N, T, D), F32),
                   jax.ShapeDtypeStruct((N, WINDOW, LANES), F32),
                   jax.ShapeDtypeStruct((N, WINDOW, LANES), F32),
                   jax.ShapeDtypeStruct((N, B_HEADS, B_DK, B_DV), F32)],
        scratch_shapes=[two(tm, A_WIDTH, dtype=BF16),
                        two(2 * A_KV_HEADS, RET_CHUNK + tm, LANES, dtype=BF16),
                        two(2 * A_KV_HEADS, RET_CHUNK + tm, LANES, dtype=BF16),
                        two(tm, B_QK_WIDTH, dtype=BF16),
                        two(tm, B_QK_WIDTH, dtype=BF16),
                        two(tm, B_QK_WIDTH, dtype=BF16),
                        two(tm, B_V_WIDTH, dtype=BF16),
                        two(tm, MIX_WIDTH, dtype=F32),
                        pltpu.VMEM((tm, MIX_WIDTH), BF16),
                        pltpu.VMEM((RET_CHUNK, 2 * RET_CHUNK), F32),
                        pltpu.VMEM((nchunk * A_HEADS, RET_CHUNK, 2 * RET_CHUNK), F32),
                        pltpu.VMEM((nchunk * A_HEADS, RET_CHUNK, 2 * RET_CHUNK), BF16),
                        pltpu.VMEM((nchunk * A_TILES, RET_CHUNK, LANES), F32),
                        pltpu.VMEM((nchunk * B_HEADS, RET_CHUNK, RET_CHUNK), BF16),
                        pltpu.VMEM((nchunk * B_HEADS, B_DK, B_DV), F32),
                        pltpu.VMEM((nchunk * B_HEADS, B_DK, B_DV), BF16)],
        compiler_params=pltpu.CompilerParams(dimension_semantics=("arbitrary",),
                                             vmem_limit_bytes=VMEM_LIMIT),
        name="prompt_layer",
    )(sinks, gam, x, x, win, wout, ng, gq, gk, *rot_a, *rot_b, dmat, qdec, kdec)


def _sample_kernel(gam_ref, x_ref, win_ref, wout_ref, ng_ref, gq_ref, gk_ref,
                   ca_ref, sa1_ref, sa2_ref, cb_ref, sb_ref, sink_ref, d8_ref, qdec_ref, kdec_ref,
                   ck_ref, cv_ref, st_ref,
                   y_ref, ko_ref, vo_ref, so_ref,
                   xs_s, qz_s, kn_s, vn_s, knt_s, vnt_s, qb_s, kb_s, vb_s, g_s, mix_s, *, nseq, dec):
    layer = pl.program_id(0)
    step = pl.program_id(1)
    ntok = x_ref.shape[0]
    pair_rows = 2 * dec
    npair = nseq // 2

    @pl.when(jnp.logical_and(layer == 0, step == 0))
    def _():
        xs_s[...] = x_ref[...]

    @pl.when(step == 0)
    def _():
        h = (_row_rms(xs_s[...]) * ng_ref[...]).astype(BF16)
        lo_mask = lax.broadcasted_iota(jnp.int32, (ntok, LANES), 1) < A_HEAD_DIM
        ca, sa1, sa2 = ca_ref[...], sa1_ref[...], sa2_ref[...]
        cb, sb = cb_ref[...], sb_ref[...]

        def tile(a, i):
            return a[:, i * LANES:(i + 1) * LANES]

        qa = _dot(h, win_ref[:, OFF_QA:OFF_QA + A_WIDTH])
        for m in range(A_TILES):
            qr = _pair_norm_rot(tile(qa, m), gq_ref[...], ca, sa1, sa2, lo_mask)
            sw = pltpu.roll(qr, A_HEAD_DIM, 1)
            if m < A_TILES // 2:
                qz_s[2 * m] = jnp.where(lo_mask, qr, 0.0)
                qz_s[2 * m + 1] = jnp.where(lo_mask, sw, 0.0)
            else:
                qz_s[2 * m] = jnp.where(lo_mask, 0.0, sw)
                qz_s[2 * m + 1] = jnp.where(lo_mask, 0.0, qr)
        kv = _dot(h, win_ref[:, OFF_KA:OFF_KA + 2 * A_KV_WIDTH])
        kn = _pair_norm_rot(tile(kv, 0), gk_ref[...], ca, sa1, sa2, lo_mask)
        vn = tile(kv, 1)
        kn_s[...] = kn
        vn_s[...] = vn
        for tt in range(ntok // LANES):
            knt_s[tt] = kn[tt * LANES:(tt + 1) * LANES, :].T
            vnt_s[tt] = vn[tt * LANES:(tt + 1) * LANES, :].T
        ga = _dot(h, win_ref[:, OFF_GA:OFF_GA + A_WIDTH])
        for m in range(A_TILES):
            g_s[:, m * LANES:(m + 1) * LANES] = _silu(tile(ga, m))
        qb = _dot(h, win_ref[:, OFF_QB:OFF_QB + B_QK_WIDTH])
        for hh in range(B_HEADS):
            qb_s[:, hh * LANES:(hh + 1) * LANES] = _rot_b(tile(qb, hh), cb, sb)
        kb = _dot(h, win_ref[:, OFF_KB:OFF_KB + B_QK_WIDTH])
        for hh in range(B_HEADS):
            kb_s[:, hh * LANES:(hh + 1) * LANES] = _rot_b(tile(kb, hh), cb, sb) * (B_DK ** -0.5)
        vb_s[...] = _dot(h, win_ref[:, OFF_VB:OFF_VB + B_V_WIDTH])
        gb = _dot(h, win_ref[:, OFF_GB:OFF_GB + B_V_WIDTH])
        for hh in range(B_HEADS):
            g_s[:, A_WIDTH + hh * LANES:A_WIDTH + (hh + 1) * LANES] = _silu(tile(gb, hh))

    nq = A_HEADS * pair_rows
    row = lax.broadcasted_iota(jnp.int32, (nq, LANES), 0)
    slot = lax.broadcasted_iota(jnp.int32, (nq, LANES), 1)
    first_seq = (row % pair_rows) < dec
    cache_mask = slot > (row % dec)
    rown = lax.broadcasted_iota(jnp.int32, (nq, pair_rows), 0)
    coln = lax.broadcasted_iota(jnp.int32, (nq, pair_rows), 1)
    new_mask = ((coln // dec) == ((rown % pair_rows) // dec)) & ((coln % dec) <= (rown % dec))
    lo8 = lax.broadcasted_iota(jnp.int32, (pair_rows, LANES), 1) < A_HEAD_DIM
    r8 = lax.broadcasted_iota(jnp.int32, (pair_rows, LANES), 0)
    keep = lax.broadcasted_iota(jnp.int32, (LANES, WINDOW), 1) < WINDOW - dec
    sink = sink_ref[...]

    def pair_rows_of(p):
        seq0 = step * nseq + 2 * p
        return pl.ds(pl.multiple_of(seq0 * dec, pair_rows), pair_rows)


    stage1 = []
    for p in range(npair):
        b0 = 2 * p
        rows = pair_rows_of(p)
        qz = qz_s[:, rows, :].reshape(nq, LANES).astype(BF16)
        s_c = jnp.where(first_seq, _dot(qz, ck_ref[b0].astype(BF16)), _dot(qz, ck_ref[b0 + 1].astype(BF16)))
        s_n = _dot_nt(qz, kn_s[rows, :].astype(BF16))
        rets = []
        for hh in range(B_HEADS):
            sl = slice(hh * LANES, (hh + 1) * LANES)
            q = qb_s[rows, sl].astype(BF16)
            kf = kb_s[rows, sl]
            kd = kf * kdec_ref[hh]
            v = vb_s[rows, sl].astype(BF16)
            sc = (_dot_nt(q, kf.astype(BF16)) * d8_ref[hh]).astype(BF16)
            crosses = []
            for j in range(2):
                S = st_ref[b0 + j, hh]
                crosses.append(_dot(q, S.astype(BF16)))
                kdj = jnp.where((r8 // dec) == j, kd, 0.0).astype(BF16)
                so_ref[b0 + j, hh] = gam_ref[hh] * S + _dot_tn(kdj, v)
            cross = jnp.where(r8 < dec, crosses[0], crosses[1]) * qdec_ref[hh]
            rets.append((sc, v, cross))
        stage1.append((s_c, s_n, rets))

    probs = []
    for s_c, s_n, _ in stage1:
        p_c, p_n = _softmax_parts([jnp.where(cache_mask, s_c, NEG), jnp.where(new_mask, s_n, NEG)], sink)
        probs.append((p_c.astype(BF16), p_n.astype(BF16)))

    for p in range(npair):
        b0 = 2 * p
        rows = pair_rows_of(p)
        p_c, p_n = probs[p]
        o = jnp.where(first_seq, _dot_nt(p_c, cv_ref[b0].astype(BF16)), _dot_nt(p_c, cv_ref[b0 + 1].astype(BF16)))
        o = o + _dot(p_n, vn_s[rows, :].astype(BF16))
        for m in range(A_TILES):
            oa = o[(2 * m) * pair_rows:(2 * m + 1) * pair_rows, :]
            ob = o[(2 * m + 1) * pair_rows:(2 * m + 2) * pair_rows, :]
            if m < A_TILES // 2:
                blk = jnp.where(lo8, oa, pltpu.roll(ob, A_HEAD_DIM, 1))
            else:
                blk = jnp.where(lo8, pltpu.roll(oa, A_HEAD_DIM, 1), ob)
            sl = slice(m * LANES, (m + 1) * LANES)
            mix_s[rows, sl] = g_s[rows, sl] * blk
        for hh, (sc, v, cross) in enumerate(stage1[p][2]):
            osl = slice(A_WIDTH + hh * LANES, A_WIDTH + (hh + 1) * LANES)
            mix_s[rows, osl] = g_s[rows, osl] * _row_rms(_dot(sc, v) + cross)

    for b in range(nseq):
        tok0 = (step * nseq + b) * dec
        tt = tok0 // LANES
        shift = (WINDOW - dec) - tok0 % LANES
        ko_ref[b] = jnp.where(keep, pltpu.roll(ck_ref[b], WINDOW - dec, 1), pltpu.roll(knt_s[tt], shift, 1))
        vo_ref[b] = jnp.where(keep, pltpu.roll(cv_ref[b], WINDOW - dec, 1), pltpu.roll(vnt_s[tt], shift, 1))

    @pl.when(step == pl.num_programs(1) - 1)
    def _():
        xs_s[...] = xs_s[...] + _dot(mix_s[...].astype(BF16), wout_ref[...])

    @pl.when(jnp.logical_and(layer == pl.num_programs(0) - 1, step == pl.num_programs(1) - 1))
    def _():
        y_ref[...] = xs_s[...]


def _stacked_spec(shape):
    nd = len(shape)
    return pl.BlockSpec((None,) + tuple(shape), lambda l, i: (l,) + (0,) * nd)


def _sample_layers(x, win, wout, ng, gq, gk, sink_col, gam, rot_a, rot_b, d8, qdec, kdec, ck, cv, st, dec):
    ntok, D = x.shape
    depth, nb = ck.shape[:2]
    nseq = SAMPLE_SEQS
    assert nb % nseq == 0 and nseq % 2 == 0 and ntok == nb * dec and 2 * dec == 8 and ntok % LANES == 0
    assert LANES % dec == 0 and WINDOW == LANES
    pair_rows = 2 * dec
    nq = A_HEADS * pair_rows
    smem = pl.BlockSpec(memory_space=pltpu.SMEM)
    kv = pl.BlockSpec((None, nseq, A_KV_WIDTH, WINDOW), lambda l, i: (l, i, 0, 0))
    stt = pl.BlockSpec((None, nseq, B_HEADS, B_DK, B_DV), lambda l, i: (l, i, 0, 0, 0))
    tab = _const_spec((ntok, LANES))
    return pl.pallas_call(
        functools.partial(_sample_kernel, nseq=nseq, dec=dec),
        grid=(depth, nb // nseq),
        in_specs=[smem, _const_spec((ntok, D)), _stacked_spec((D, IN_WIDTH)), _stacked_spec((MIX_WIDTH, D)),
                  _stacked_spec((1, D)), _stacked_spec((1, LANES)), _stacked_spec((1, LANES)),
                  tab, tab, tab, tab, tab,
                  _stacked_spec((nq, 1)), _const_spec((B_HEADS, pair_rows, pair_rows)),
                  _const_spec((B_HEADS, pair_rows, LANES)), _const_spec((B_HEADS, pair_rows, LANES)),
                  kv, kv, stt],
        out_specs=[pl.BlockSpec((ntok, D), lambda l, i: (0, 0)), kv, kv, stt],
        out_shape=[jax.ShapeDtypeStruct((ntok, D), F32),
                   jax.ShapeDtypeStruct(ck.shape, F32),
                   jax.ShapeDtypeStruct(cv.shape, F32),
                   jax.ShapeDtypeStruct(st.shape, F32)],
        scratch_shapes=[pltpu.VMEM((ntok, D), F32),
                        pltpu.VMEM((A_HEADS, ntok, LANES), F32),
                        pltpu.VMEM((ntok, LANES), F32),
                        pltpu.VMEM((ntok, LANES), F32),
                        pltpu.VMEM((ntok // LANES, A_KV_WIDTH, LANES), F32),
                        pltpu.VMEM((ntok // LANES, A_KV_WIDTH, LANES), F32),
                        pltpu.VMEM((ntok, B_QK_WIDTH), F32),
                        pltpu.VMEM((ntok, B_QK_WIDTH), F32),
                        pltpu.VMEM((ntok, B_V_WIDTH), F32),
                        pltpu.VMEM((ntok, MIX_WIDTH), F32),
                        pltpu.VMEM((ntok, MIX_WIDTH), F32)],
        compiler_params=pltpu.CompilerParams(dimension_semantics=("arbitrary", "arbitrary"),
                                             vmem_limit_bytes=VMEM_LIMIT),
        name="sample_layers",
    )(gam, x, win, wout, ng, gq, gk, *rot_a, *rot_b, sink_col, d8, qdec, kdec, ck, cv, st)


def _cos_sin(pos_start, n_pos, freq):
    if n_pos % LANES or pos_start:
        ang = (pos_start + jnp.arange(n_pos, dtype=jnp.int32)).astype(F32)[:, None] * freq[None, :]
        return jnp.cos(ang), jnp.sin(ang)
    a_hi = (jnp.arange(n_pos // LANES, dtype=jnp.int32) * LANES).astype(F32)[:, None] * freq[None, :]
    a_lo = jnp.arange(LANES, dtype=jnp.int32).astype(F32)[:, None] * freq[None, :]
    ch, sh = jnp.cos(a_hi)[:, None, :], jnp.sin(a_hi)[:, None, :]
    cl, sl = jnp.cos(a_lo)[None, :, :], jnp.sin(a_lo)[None, :, :]
    return (ch * cl - sh * sl).reshape(n_pos, LANES), (sh * cl + ch * sl).reshape(n_pos, LANES)


def _rot_tables_a(pos_start, n_pos):
    half = A_ROT_DIM // 2
    inv_freq = ROPE_THETA_A ** (-jnp.arange(half, dtype=F32) / half)
    d = jnp.arange(LANES, dtype=jnp.int32) % A_HEAD_DIM
    freq = jnp.where(d < A_ROT_DIM, inv_freq[d % half], 0.0)
    cos, sin = _cos_sin(pos_start, n_pos, freq)
    s1 = jnp.where(d < half, -sin, 0.0)
    s2 = jnp.where((d >= half) & (d < A_ROT_DIM), sin, 0.0)
    return cos, s1, s2


def _rot_tables_b(pos_start, n_pos):
    half = B_DK // 2
    inv_freq = ROPE_THETA_B ** (-jnp.arange(half, dtype=F32) / half)
    cos, sin = _cos_sin(pos_start, n_pos, jnp.tile(inv_freq, 2))
    return cos, jnp.where(jnp.arange(LANES) < half, -sin, sin)


def _log_decay():
    return jnp.log(1.0 - 2.0 ** (-5.0 - jnp.arange(B_HEADS, dtype=F32)))


def _decay_tables(C):
    lg = _log_decay()
    idx = jnp.arange(C, dtype=F32)
    diff = idx[:, None] - idx[None, :]
    dmat = jnp.where(diff >= 0, jnp.exp(lg[:, None, None] * jnp.maximum(diff, 0.0)), 0.0)
    q_dec = jnp.exp(lg[None, :] * (idx[:, None] + 1.0))
    k_dec = jnp.exp(lg[None, :] * (C - 1.0 - idx[:, None]))
    gam = jnp.exp(lg * C)
    return dmat, q_dec, k_dec, gam


def kernel(x_prompt, x_sample, cache_swa_k, cache_swa_v, state_ret, w_in, w_out, norm_g, q_norm_g, k_norm_g, sinks):
    N, T, D = x_prompt.shape
    nb, dec, _ = x_sample.shape
    w_buf = cache_swa_k.shape[2]
    assert w_buf == WINDOW

    rot_a_p, rot_b_p = _rot_tables_a(0, T), _rot_tables_b(0, T)
    rot_a_s = tuple(jnp.tile(a, (nb, 1)) for a in _rot_tables_a(PAST_LEN, dec))
    rot_b_s = tuple(jnp.tile(a, (nb, 1)) for a in _rot_tables_b(PAST_LEN, dec))

    dmat, q_dec, k_dec, gam_p = _decay_tables(RET_CHUNK)
    qdec_p = jnp.broadcast_to(q_dec.T[:, :, None], (B_HEADS, RET_CHUNK, LANES))
    kdec_p = jnp.broadcast_to(k_dec.T[:, :, None], (B_HEADS, RET_CHUNK, LANES))
    dmat4, q_dec4, k_dec4, gam_s = _decay_tables(dec)
    d8 = jnp.kron(jnp.eye(2, dtype=F32)[None], jnp.ones((1, dec, dec), F32)) * jnp.tile(dmat4, (1, 2, 2))
    qdec_s = jnp.broadcast_to(jnp.tile(q_dec4.T, (1, 2))[:, :, None], (B_HEADS, 2 * dec, LANES))
    kdec_s = jnp.broadcast_to(jnp.tile(k_dec4.T, (1, 2))[:, :, None], (B_HEADS, 2 * dec, LANES))

    w_in_b = w_in.astype(BF16)
    w_out_b = w_out.astype(BF16)
    ck = cache_swa_k.transpose(0, 1, 3, 4, 2).reshape(DEPTH, nb, A_KV_WIDTH, w_buf)
    cv = cache_swa_v.transpose(0, 1, 3, 4, 2).reshape(DEPTH, nb, A_KV_WIDTH, w_buf)

    ng = norm_g[:, None, :]
    gq = jnp.tile(q_norm_g, (1, HEADS_PER_TILE))[:, None, :] * (A_HEAD_DIM ** -0.5 * LOG2E)
    gk = jnp.tile(k_norm_g, (1, HEADS_PER_TILE))[:, None, :]
    sinks2 = sinks * LOG2E

    sink_col = jnp.repeat(sinks2, 2 * dec, axis=1)[:, :, None]
    xs, ks, vs, ss = _sample_layers(x_sample.reshape(nb * dec, D), w_in_b, w_out_b, ng, gq, gk, sink_col, gam_s,
                                    rot_a_s, rot_b_s, d8, qdec_s, kdec_s, ck, cv, state_ret, dec)

    xp = x_prompt
    kp_l, vp_l, sp_l = [], [], []
    for l in range(DEPTH):
        xp, kl, vl, S = _prompt_layer(l, xp, w_in_b, w_out_b, ng, gq, gk, sinks2[l], gam_p,
                                      rot_a_p, rot_b_p, dmat, qdec_p, kdec_p)
        kp_l.append(kl)
        vp_l.append(vl)
        sp_l.append(S)

    kv_p = (DEPTH, N, WINDOW, A_KV_HEADS, A_HEAD_DIM)
    kv_t = (DEPTH, nb, A_KV_HEADS, A_HEAD_DIM, w_buf)
    return (xp, xs.reshape(nb, dec, D),
            jnp.stack(kp_l).reshape(kv_p), jnp.stack(vp_l).reshape(kv_p), jnp.stack(sp_l),
            ks.reshape(kv_t).transpose(0, 1, 4, 2, 3), vs.reshape(kv_t).transpose(0, 1, 4, 2, 3), ss)
```

```python
import functools

import jax
import jax.numpy as jnp
from jax import lax
from jax.experimental import pallas as pl
from jax.experimental.pallas import tpu as pltpu

D_MODEL = 1024
DEPTH = 4
PAST_LEN = 8192
A_HEADS = 8
A_KV_HEADS = 2
A_GROUP = A_HEADS // A_KV_HEADS
A_HEAD_DIM = 64
WINDOW = 128
A_ROT_DIM = A_HEAD_DIM // 4
ROPE_THETA_A = 500000.0
B_HEADS = 4
B_DK = 128
B_DV = 128
RET_CHUNK = 128
ROPE_THETA_B = 10000.0
EPS = 1e-6

A_WIDTH = A_HEADS * A_HEAD_DIM
A_KV_WIDTH = A_KV_HEADS * A_HEAD_DIM
B_QK_WIDTH = B_HEADS * B_DK
B_V_WIDTH = B_HEADS * B_DV
MIX_WIDTH = A_WIDTH + B_V_WIDTH
OFF_QA = 0
OFF_KA = OFF_QA + A_WIDTH
OFF_VA = OFF_KA + A_KV_WIDTH
OFF_GA = OFF_VA + A_KV_WIDTH
OFF_QB = OFF_GA + A_WIDTH
OFF_KB = OFF_QB + B_QK_WIDTH
OFF_VB = OFF_KB + B_QK_WIDTH
OFF_GB = OFF_VB + B_V_WIDTH
IN_WIDTH = OFF_GB + B_V_WIDTH

LANES = 128
HEADS_PER_TILE = LANES // A_HEAD_DIM
A_TILES = A_WIDTH // LANES
NEG = -1e30
LOG2E = 1.4426950408889634

PROMPT_BLOCK = 512
SAMPLE_SEQS = 8
VMEM_LIMIT = 56 * 1024 * 1024

F32 = jnp.float32
BF16 = jnp.bfloat16


def _dot(a, b):
    return jnp.dot(a, b, preferred_element_type=F32)


def _dot_nt(a, b):
    return lax.dot_general(a, b, (((1,), (1,)), ((), ())), preferred_element_type=F32)


def _dot_tn(a, b):
    return lax.dot_general(a, b, (((0,), (0,)), ((), ())), preferred_element_type=F32)


def _silu(x):
    return x * (1.0 / (1.0 + jnp.exp(-x)))


def _row_rms(x):
    n = x.shape[-1]
    return x * lax.rsqrt(jnp.sum(x * x, axis=-1, keepdims=True) * (1.0 / n) + EPS)


def _pair_norm_rot(blk, g_row, ca, sa1, sa2, lo_mask):
    sq = blk * blk
    tot = jnp.sum(sq, axis=-1, keepdims=True)
    lo = jnp.sum(jnp.where(lo_mask, sq, 0.0), axis=-1, keepdims=True)
    hi = tot - lo
    inv_lo = lax.rsqrt(lo * (1.0 / A_HEAD_DIM) + EPS)
    inv_hi = lax.rsqrt(hi * (1.0 / A_HEAD_DIM) + EPS)
    xn = blk * jnp.where(lo_mask, inv_lo, inv_hi) * g_row
    half = A_ROT_DIM // 2
    return xn * ca + pltpu.roll(xn, LANES - half, 1) * sa1 + pltpu.roll(xn, half, 1) * sa2


def _rot_b(blk, cb, sb):
    return blk * cb + pltpu.roll(blk, B_DK // 2, 1) * sb


def _softmax_parts(parts, sink):
    mx = sink
    for s in parts:
        mx = jnp.maximum(mx, jnp.max(s, axis=-1, keepdims=True))
    ps = [jnp.exp2(s - mx) for s in parts]
    den = jnp.exp2(sink - mx)
    for p in ps:
        den = den + jnp.sum(p, axis=-1, keepdims=True)
    rinv = 1.0 / den
    return [p * rinv for p in ps]


def _prompt_kernel(sinks_ref, gam_ref, x_ref, xres_ref, win_ref, wout_ref, ng_ref, gq_ref, gk_ref,
                   ca_ref, sa1_ref, sa2_ref, cb_ref, sb_ref, dmat_ref, qdec_ref, kdec_ref,
                   y_ref, klast_ref, vlast_ref, S_ref,
                   q_s, kz_s, vz_s, qb_s, kb_s, kbd_s, vb_s, g_s, mix_s, bias_s, s_s, p_s, es_s,
                   sc_s, kv_s, sb_s, *, tm, nt):
    g = pl.program_id(0)
    nchunk = tm // RET_CHUNK
    C = RET_CHUNK
    lo_c = lax.broadcasted_iota(jnp.int32, (C, LANES), 1) < A_HEAD_DIM
    t_fin = jnp.maximum(g - 1, 0) % nt

    @pl.when(g == 0)
    def _():
        qi = lax.broadcasted_iota(jnp.int32, (C, 2 * C), 0)
        kj = lax.broadcasted_iota(jnp.int32, (C, 2 * C), 1)
        bias_s[...] = jnp.where((kj > qi) & (kj <= qi + C), 0.0, NEG)
        for ref in (q_s, kz_s, vz_s, qb_s, kb_s, kbd_s, vb_s, g_s):
            ref[1] = jnp.zeros(ref.shape[1:], ref.dtype)

    @pl.when(t_fin == 0)
    def _():
        S_ref[...] = jnp.zeros_like(S_ref)

    def tile(a, i):
        return a[:, i * LANES:(i + 1) * LANES]

    def body(w, r):
        lo_mask = lax.broadcasted_iota(jnp.int32, (tm, LANES), 1) < A_HEAD_DIM
        ca, sa1, sa2 = ca_ref[...], sa1_ref[...], sa2_ref[...]
        cb, sb = cb_ref[...], sb_ref[...]
        first_bias = jnp.where(t_fin == 0, NEG, 0.0)

        def chunk_rows(c):
            return slice(c * C, (c + 1) * C)

        def window_rows(c):
            return slice(c * C, (c + 2) * C)

        def group_tiles(grp):
            per_group = A_GROUP // HEADS_PER_TILE
            return range(grp * per_group, (grp + 1) * per_group)

        def scores(c):
            for grp in range(A_KV_HEADS):
                qp = jnp.concatenate([q_s[r, chunk_rows(c), m * LANES:(m + 1) * LANES] for m in group_tiles(grp)],
                                     axis=0)
                for half in range(HEADS_PER_TILE):
                    s_grp = _dot_nt(qp, kz_s[r, 2 * grp + half, window_rows(c), :])
                    for i, m in enumerate(group_tiles(grp)):
                        s = s_grp[i * C:(i + 1) * C, :] + bias_s[...]
                        if c == 0:
                            s = jnp.concatenate([s[:, :C] + first_bias, s[:, C:]], axis=1)
                        s_s[c * A_HEADS + m * HEADS_PER_TILE + half] = s

        def softmax(c):
            for m in range(A_TILES):
                es = []
                for half in range(HEADS_PER_TILE):
                    hd = m * HEADS_PER_TILE + half
                    s = s_s[c * A_HEADS + hd]
                    sink = sinks_ref[hd]
                    mx = jnp.maximum(jnp.max(s, axis=-1, keepdims=True), sink)
                    p_s[c * A_HEADS + hd] = jnp.exp2(s - mx).astype(BF16)
                    es.append(jnp.exp2(sink - mx))
                es_s[c * A_TILES + m] = jnp.where(lo_c, es[0], es[1])

        def attend(c):
            for grp in range(A_KV_HEADS):
                outs = []
                for half in range(HEADS_PER_TILE):
                    p_grp = jnp.concatenate([p_s[c * A_HEADS + m * HEADS_PER_TILE + half] for m in group_tiles(grp)],
                                            axis=0)
                    outs.append(_dot(p_grp, vz_s[r, 2 * grp + half, window_rows(c), :]))
                for i, m in enumerate(group_tiles(grp)):
                    sl = slice(m * LANES, (m + 1) * LANES)
                    oe, oo = (o[i * C:(i + 1) * C, :] for o in outs)
                    pv = jnp.where(lo_c, oe, oo)
                    den = pltpu.roll(jnp.where(lo_c, oo, oe), A_HEAD_DIM, 1) + es_s[c * A_TILES + m]
                    mix_s[chunk_rows(c), sl] = (g_s[r, chunk_rows(c), sl] * pv * (1.0 / den)).astype(BF16)

        def retain_local(c):
            for hh in range(B_HEADS):
                sl = slice(hh * LANES, (hh + 1) * LANES)
                rows = chunk_rows(c)
                sc_s[c * B_HEADS + hh] = (_dot_nt(qb_s[r, rows, sl], kb_s[r, rows, sl]) * dmat_ref[hh]).astype(BF16)
                kv_s[c * B_HEADS + hh] = _dot_tn(kbd_s[r, rows, sl], vb_s[r, rows, sl])

        def retain_states():
            for hh in range(B_HEADS):
                S = S_ref[0, hh]
                for c in range(nchunk):
                    sb_s[c * B_HEADS + hh] = S.astype(BF16)
                    S = gam_ref[hh] * S + kv_s[c * B_HEADS + hh]
                S_ref[0, hh] = S

        def retain_out(c):
            for hh in range(B_HEADS):
                sl = slice(hh * LANES, (hh + 1) * LANES)
                rows = chunk_rows(c)
                i = c * B_HEADS + hh
                ret = _dot(sc_s[i], vb_s[r, rows, sl]) + _dot(qb_s[r, rows, sl], sb_s[i]) * qdec_ref[hh]
                osl = slice(A_WIDTH + hh * LANES, A_WIDTH + (hh + 1) * LANES)
                mix_s[rows, osl] = (g_s[r, rows, osl] * _row_rms(ret)).astype(BF16)

        def project_attention():
            qa = _dot(h, win_ref[:, OFF_QA:OFF_QA + A_WIDTH])
            for m in range(A_TILES):
                q_s[w, :, m * LANES:(m + 1) * LANES] = _pair_norm_rot(
                    tile(qa, m), gq_ref[...], ca, sa1, sa2, lo_mask).astype(BF16)
            kv = _dot(h, win_ref[:, OFF_KA:OFF_KA + 2 * A_KV_WIDTH])
            kr = _pair_norm_rot(tile(kv, 0), gk_ref[...], ca, sa1, sa2, lo_mask)
            va = tile(kv, 1)
            klast_ref[0] = kr[tm - WINDOW:, :]
            vlast_ref[0] = va[tm - WINDOW:, :]
            for src, dst, fill in ((kr, kz_s, 0.0), (va, vz_s, 1.0)):
                sw = pltpu.roll(src, A_HEAD_DIM, 1)
                dst[w, 0, C:, :] = jnp.where(lo_mask, src, fill).astype(BF16)
                dst[w, 1, C:, :] = jnp.where(lo_mask, fill, sw).astype(BF16)
                dst[w, 2, C:, :] = jnp.where(lo_mask, sw, fill).astype(BF16)
                dst[w, 3, C:, :] = jnp.where(lo_mask, fill, src).astype(BF16)
                for i in range(2 * A_KV_HEADS):
                    dst[w, i, 0:C, :] = dst[r, i, tm:tm + C, :]

        def project_gate_a():
            ga = _dot(h, win_ref[:, OFF_GA:OFF_GA + A_WIDTH])
            for m in range(A_TILES):
                g_s[w, :, m * LANES:(m + 1) * LANES] = _silu(tile(ga, m))

        def project_qb():
            qb = _dot(h, win_ref[:, OFF_QB:OFF_QB + B_QK_WIDTH])
            for hh in range(B_HEADS):
                qb_s[w, :, hh * LANES:(hh + 1) * LANES] = _rot_b(tile(qb, hh), cb, sb).astype(BF16)

        def project_kb():
            kb = _dot(h, win_ref[:, OFF_KB:OFF_KB + B_QK_WIDTH])
            for hh in range(B_HEADS):
                sl = slice(hh * LANES, (hh + 1) * LANES)
                kbh = _rot_b(tile(kb, hh), cb, sb) * (B_DK ** -0.5)
                kb_s[w, :, sl] = kbh.astype(BF16)
                kbd_s[w, :, sl] = (kbh.reshape(nchunk, C, LANES) * kdec_ref[hh][None]).reshape(tm, LANES).astype(BF16)

        def project_vb_gate_b():
            vb_s[w] = _dot(h, win_ref[:, OFF_VB:OFF_VB + B_V_WIDTH]).astype(BF16)
            gb = _dot(h, win_ref[:, OFF_GB:OFF_GB + B_V_WIDTH])
            for hh in range(B_HEADS):
                g_s[w, :, A_WIDTH + hh * LANES:A_WIDTH + (hh + 1) * LANES] = _silu(tile(gb, hh))

        h = (_row_rms(x_ref[0]) * ng_ref[...]).astype(BF16)
        for c in range(nchunk):
            scores(c)
            retain_local(c)
        project_attention()
        for c in range(nchunk):
            softmax(c)
        retain_states()
        project_gate_a()
        project_qb()
        for c in range(nchunk):
            attend(c)
            retain_out(c)
        project_kb()
        project_vb_gate_b()
        y_ref[0] = xres_ref[0] + _dot(mix_s[...], wout_ref[...])

    @pl.when(g % 2 == 0)
    def _():
        body(0, 1)

    @pl.when(g % 2 == 1)
    def _():
        body(1, 0)


def _const_spec(shape):
    nd = len(shape)
    return pl.BlockSpec(shape, lambda *_: (0,) * nd, pipeline_mode=pl.Buffered(1))


def _layer_spec(shape, l):
    nd = len(shape)
    return pl.BlockSpec((None,) + tuple(shape), lambda *_: (l,) + (0,) * nd, pipeline_mode=pl.Buffered(1))


def _prompt_layer(l, x, win, wout, ng, gq, gk, sinks, gam, rot_a, rot_b, dmat, qdec, kdec):
    N, T, D = x.shape
    tm = PROMPT_BLOCK
    assert T % tm == 0 and tm % RET_CHUNK == 0 and WINDOW == RET_CHUNK
    nt = T // tm
    nblk = N * nt
    nchunk = tm // RET_CHUNK

    def proj_blk(g):
        return jnp.minimum(g, nblk - 1)

    def fin_blk(g):
        return jnp.maximum(g - 1, 0)

    smem = pl.BlockSpec(memory_space=pltpu.SMEM)
    tab = pl.BlockSpec((tm, LANES), lambda g: (proj_blk(g) % nt, 0))
    x_proj = pl.BlockSpec((1, tm, D), lambda g: (proj_blk(g) // nt, proj_blk(g) % nt, 0))
    x_fin = pl.BlockSpec((1, tm, D), lambda g: (fin_blk(g) // nt, fin_blk(g) % nt, 0))
    kv_out = pl.BlockSpec((1, WINDOW, LANES), lambda g: (proj_blk(g) // nt, 0, 0))

    def two(*shape, dtype):
        return pltpu.VMEM((2,) + shape, dtype)

    return pl.pallas_call(
        functools.partial(_prompt_kernel, tm=tm, nt=nt),
        grid=(nblk + 1,),
        in_specs=[smem, smem, x_proj, x_fin,
                  _layer_spec((D, IN_WIDTH), l), _layer_spec((MIX_WIDTH, D), l),
                  _layer_spec((1, D), l), _layer_spec((1, LANES), l), _layer_spec((1, LANES), l),
                  tab, tab, tab, tab, tab,
                  _const_spec((B_HEADS, RET_CHUNK, RET_CHUNK)), _const_spec((B_HEADS, RET_CHUNK, LANES)),
                  _const_spec((B_HEADS, RET_CHUNK, LANES))],
        out_specs=[x_fin, kv_out, kv_out,
                   pl.BlockSpec((1, B_HEADS, B_DK, B_DV), lambda g: (fin_blk(g) // nt, 0, 0, 0))],
        out_shape=[jax.ShapeDtypeStruct((N, T, D), F32),
                   jax.ShapeDtypeStruct((N, WINDOW, LANES), F32),
                   jax.ShapeDtypeStruct((N, WINDOW, LANES), F32),
                   jax.ShapeDtypeStruct((N, B_HEADS, B_DK, B_DV), F32)],
        scratch_shapes=[two(tm, A_WIDTH, dtype=BF16),
                        two(2 * A_KV_HEADS, RET_CHUNK + tm, LANES, dtype=BF16),
                        two(2 * A_KV_HEADS, RET_CHUNK + tm, LANES, dtype=BF16),
                        two(tm, B_QK_WIDTH, dtype=BF16),
                        two(tm, B_QK_WIDTH, dtype=BF16),
                        two(tm, B_QK_WIDTH, dtype=BF16),
                        two(tm, B_V_WIDTH, dtype=BF16),
                        two(tm, MIX_WIDTH, dtype=F32),
                        pltpu.VMEM((tm, MIX_WIDTH), BF16),
                        pltpu.VMEM((RET_CHUNK, 2 * RET_CHUNK), F32),
                        pltpu.VMEM((nchunk * A_HEADS, RET_CHUNK, 2 * RET_CHUNK), F32),
                        pltpu.VMEM((nchunk * A_HEADS, RET_CHUNK, 2 * RET_CHUNK), BF16),
                        pltpu.VMEM((nchunk * A_TILES, RET_CHUNK, LANES), F32),
                        pltpu.VMEM((nchunk * B_HEADS, RET_CHUNK, RET_CHUNK), BF16),
                        pltpu.VMEM((nchunk * B_HEADS, B_DK, B_DV), F32),
                        pltpu.VMEM((nchunk * B_HEADS, B_DK, B_DV), BF16)],
        compiler_params=pltpu.CompilerParams(dimension_semantics=("arbitrary",),
                                             vmem_limit_bytes=VMEM_LIMIT),
        name="prompt_layer",
    )(sinks, gam, x, x, win, wout, ng, gq, gk, *rot_a, *rot_b, dmat, qdec, kdec)


def _sample_kernel(gam_ref, x_ref, win_ref, wout_ref, ng_ref, gq_ref, gk_ref,
                   ca_ref, sa1_ref, sa2_ref, cb_ref, sb_ref, sink_ref, d8_ref, qdec_ref, kdec_ref,
                   ck_ref, cv_ref, st_ref,
                   y_ref, ko_ref, vo_ref, so_ref,
                   xs_s, qz_s, kn_s, vn_s, knt_s, vnt_s, qb_s, kb_s, vb_s, g_s, mix_s, *, nseq, dec):
    layer = pl.program_id(0)
    step = pl.program_id(1)
    ntok = x_ref.shape[0]
    pair_rows = 2 * dec
    npair = nseq // 2

    @pl.when(jnp.logical_and(layer == 0, step == 0))
    def _():
        xs_s[...] = x_ref[...]

    @pl.when(step == 0)
    def _():
        h = (_row_rms(xs_s[...]) * ng_ref[...]).astype(BF16)
        lo_mask = lax.broadcasted_iota(jnp.int32, (ntok, LANES), 1) < A_HEAD_DIM
        ca, sa1, sa2 = ca_ref[...], sa1_ref[...], sa2_ref[...]
        cb, sb = cb_ref[...], sb_ref[...]

        def tile(a, i):
            return a[:, i * LANES:(i + 1) * LANES]

        qa = _dot(h, win_ref[:, OFF_QA:OFF_QA + A_WIDTH])
        for m in range(A_TILES):
            qr = _pair_norm_rot(tile(qa, m), gq_ref[...], ca, sa1, sa2, lo_mask)
            sw = pltpu.roll(qr, A_HEAD_DIM, 1)
            if m < A_TILES // 2:
                qz_s[2 * m] = jnp.where(lo_mask, qr, 0.0)
                qz_s[2 * m + 1] = jnp.where(lo_mask, sw, 0.0)
            else:
                qz_s[2 * m] = jnp.where(lo_mask, 0.0, sw)
                qz_s[2 * m + 1] = jnp.where(lo_mask, 0.0, qr)
        kv = _dot(h, win_ref[:, OFF_KA:OFF_KA + 2 * A_KV_WIDTH])
        kn = _pair_norm_rot(tile(kv, 0), gk_ref[...], ca, sa1, sa2, lo_mask)
        vn = tile(kv, 1)
        kn_s[...] = kn
        vn_s[...] = vn
        for tt in range(ntok // LANES):
            knt_s[tt] = kn[tt * LANES:(tt + 1) * LANES, :].T
            vnt_s[tt] = vn[tt * LANES:(tt + 1) * LANES, :].T
        ga = _dot(h, win_ref[:, OFF_GA:OFF_GA + A_WIDTH])
        for m in range(A_TILES):
            g_s[:, m * LANES:(m + 1) * LANES] = _silu(tile(ga, m))
        qb = _dot(h, win_ref[:, OFF_QB:OFF_QB + B_QK_WIDTH])
        for hh in range(B_HEADS):
            qb_s[:, hh * LANES:(hh + 1) * LANES] = _rot_b(tile(qb, hh), cb, sb)
        kb = _dot(h, win_ref[:, OFF_KB:OFF_KB + B_QK_WIDTH])
        for hh in range(B_HEADS):
            kb_s[:, hh * LANES:(hh + 1) * LANES] = _rot_b(tile(kb, hh), cb, sb) * (B_DK ** -0.5)
        vb_s[...] = _dot(h, win_ref[:, OFF_VB:OFF_VB + B_V_WIDTH])
        gb = _dot(h, win_ref[:, OFF_GB:OFF_GB + B_V_WIDTH])
        for hh in range(B_HEADS):
            g_s[:, A_WIDTH + hh * LANES:A_WIDTH + (hh + 1) * LANES] = _silu(tile(gb, hh))

    nq = A_HEADS * pair_rows
    row = lax.broadcasted_iota(jnp.int32, (nq, LANES), 0)
    slot = lax.broadcasted_iota(jnp.int32, (nq, LANES), 1)
    first_seq = (row % pair_rows) < dec
    cache_mask = slot > (row % dec)
    rown = lax.broadcasted_iota(jnp.int32, (nq, pair_rows), 0)
    coln = lax.broadcasted_iota(jnp.int32, (nq, pair_rows), 1)
    new_mask = ((coln // dec) == ((rown % pair_rows) // dec)) & ((coln % dec) <= (rown % dec))
    lo8 = lax.broadcasted_iota(jnp.int32, (pair_rows, LANES), 1) < A_HEAD_DIM
    r8 = lax.broadcasted_iota(jnp.int32, (pair_rows, LANES), 0)
    keep = lax.broadcasted_iota(jnp.int32, (LANES, WINDOW), 1) < WINDOW - dec
    sink = sink_ref[...]

    def pair_rows_of(p):
        seq0 = step * nseq + 2 * p
        return pl.ds(pl.multiple_of(seq0 * dec, pair_rows), pair_rows)


    stage1 = []
    for p in range(npair):
        b0 = 2 * p
        rows = pair_rows_of(p)
        qz = qz_s[:, rows, :].reshape(nq, LANES).astype(BF16)
        s_c = jnp.where(first_seq, _dot(qz, ck_ref[b0].astype(BF16)), _dot(qz, ck_ref[b0 + 1].astype(BF16)))
        s_n = _dot_nt(qz, kn_s[rows, :].astype(BF16))
        rets = []
        for hh in range(B_HEADS):
            sl = slice(hh * LANES, (hh + 1) * LANES)
            q = qb_s[rows, sl].astype(BF16)
            kf = kb_s[rows, sl]
            kd = kf * kdec_ref[hh]
            v = vb_s[rows, sl].astype(BF16)
            sc = (_dot_nt(q, kf.astype(BF16)) * d8_ref[hh]).astype(BF16)
            crosses = []
            for j in range(2):
                S = st_ref[b0 + j, hh]
                crosses.append(_dot(q, S.astype(BF16)))
                kdj = jnp.where((r8 // dec) == j, kd, 0.0).astype(BF16)
                so_ref[b0 + j, hh] = gam_ref[hh] * S + _dot_tn(kdj, v)
            cross = jnp.where(r8 < dec, crosses[0], crosses[1]) * qdec_ref[hh]
            rets.append((sc, v, cross))
        stage1.append((s_c, s_n, rets))

    probs = []
    for s_c, s_n, _ in stage1:
        p_c, p_n = _softmax_parts([jnp.where(cache_mask, s_c, NEG), jnp.where(new_mask, s_n, NEG)], sink)
        probs.append((p_c.astype(BF16), p_n.astype(BF16)))

    for p in range(npair):
        b0 = 2 * p
        rows = pair_rows_of(p)
        p_c, p_n = probs[p]
        o = jnp.where(first_seq, _dot_nt(p_c, cv_ref[b0].astype(BF16)), _dot_nt(p_c, cv_ref[b0 + 1].astype(BF16)))
        o = o + _dot(p_n, vn_s[rows, :].astype(BF16))
        for m in range(A_TILES):
            oa = o[(2 * m) * pair_rows:(2 * m + 1) * pair_rows, :]
            ob = o[(2 * m + 1) * pair_rows:(2 * m + 2) * pair_rows, :]
            if m < A_TILES // 2:
                blk = jnp.where(lo8, oa, pltpu.roll(ob, A_HEAD_DIM, 1))
            else:
                blk = jnp.where(lo8, pltpu.roll(oa, A_HEAD_DIM, 1), ob)
            sl = slice(m * LANES, (m + 1) * LANES)
            mix_s[rows, sl] = g_s[rows, sl] * blk
        for hh, (sc, v, cross) in enumerate(stage1[p][2]):
            osl = slice(A_WIDTH + hh * LANES, A_WIDTH + (hh + 1) * LANES)
            mix_s[rows, osl] = g_s[rows, osl] * _row_rms(_dot(sc, v) + cross)

    for b in range(nseq):
        tok0 = (step * nseq + b) * dec
        tt = tok0 // LANES
        shift = (WINDOW - dec) - tok0 % LANES
        ko_ref[b] = jnp.where(keep, pltpu.roll(ck_ref[b], WINDOW - dec, 1), pltpu.roll(knt_s[tt], shift, 1))
        vo_ref[b] = jnp.where(keep, pltpu.roll(cv_ref[b], WINDOW - dec, 1), pltpu.roll(vnt_s[tt], shift, 1))

    @pl.when(step == pl.num_programs(1) - 1)
    def _():
        xs_s[...] = xs_s[...] + _dot(mix_s[...].astype(BF16), wout_ref[...])

    @pl.when(jnp.logical_and(layer == pl.num_programs(0) - 1, step == pl.num_programs(1) - 1))
    def _():
        y_ref[...] = xs_s[...]


def _stacked_spec(shape):
    nd = len(shape)
    return pl.BlockSpec((None,) + tuple(shape), lambda l, i: (l,) + (0,) * nd)


def _sample_layers(x, win, wout, ng, gq, gk, sink_col, gam, rot_a, rot_b, d8, qdec, kdec, ck, cv, st, dec):
    ntok, D = x.shape
    depth, nb = ck.shape[:2]
    nseq = SAMPLE_SEQS
    assert nb % nseq == 0 and nseq % 2 == 0 and ntok == nb * dec and 2 * dec == 8 and ntok % LANES == 0
    assert LANES % dec == 0 and WINDOW == LANES
    pair_rows = 2 * dec
    nq = A_HEADS * pair_rows
    smem = pl.BlockSpec(memory_space=pltpu.SMEM)
    kv = pl.BlockSpec((None, nseq, A_KV_WIDTH, WINDOW), lambda l, i: (l, i, 0, 0))
    stt = pl.BlockSpec((None, nseq, B_HEADS, B_DK, B_DV), lambda l, i: (l, i, 0, 0, 0))
    tab = _const_spec((ntok, LANES))
    return pl.pallas_call(
        functools.partial(_sample_kernel, nseq=nseq, dec=dec),
        grid=(depth, nb // nseq),
        in_specs=[smem, _const_spec((ntok, D)), _stacked_spec((D, IN_WIDTH)), _stacked_spec((MIX_WIDTH, D)),
                  _stacked_spec((1, D)), _stacked_spec((1, LANES)), _stacked_spec((1, LANES)),
                  tab, tab, tab, tab, tab,
                  _stacked_spec((nq, 1)), _const_spec((B_HEADS, pair_rows, pair_rows)),
                  _const_spec((B_HEADS, pair_rows, LANES)), _const_spec((B_HEADS, pair_rows, LANES)),
                  kv, kv, stt],
        out_specs=[pl.BlockSpec((ntok, D), lambda l, i: (0, 0)), kv, kv, stt],
        out_shape=[jax.ShapeDtypeStruct((ntok, D), F32),
                   jax.ShapeDtypeStruct(ck.shape, F32),
                   jax.ShapeDtypeStruct(cv.shape, F32),
                   jax.ShapeDtypeStruct(st.shape, F32)],
        scratch_shapes=[pltpu.VMEM((ntok, D), F32),
                        pltpu.VMEM((A_HEADS, ntok, LANES), F32),
                        pltpu.VMEM((ntok, LANES), F32),
                        pltpu.VMEM((ntok, LANES), F32),
                        pltpu.VMEM((ntok // LANES, A_KV_WIDTH, LANES), F32),
                        pltpu.VMEM((ntok // LANES, A_KV_WIDTH, LANES), F32),
                        pltpu.VMEM((ntok, B_QK_WIDTH), F32),
                        pltpu.VMEM((ntok, B_QK_WIDTH), F32),
                        pltpu.VMEM((ntok, B_V_WIDTH), F32),
                        pltpu.VMEM((ntok, MIX_WIDTH), F32),
                        pltpu.VMEM((ntok, MIX_WIDTH), F32)],
        compiler_params=pltpu.CompilerParams(dimension_semantics=("arbitrary", "arbitrary"),
                                             vmem_limit_bytes=VMEM_LIMIT),
        name="sample_layers",
    )(gam, x, win, wout, ng, gq, gk, *rot_a, *rot_b, sink_col, d8, qdec, kdec, ck, cv, st)


def _cos_sin(pos_start, n_pos, freq):
    if n_pos % LANES or pos_start:
        ang = (pos_start + jnp.arange(n_pos, dtype=jnp.int32)).astype(F32)[:, None] * freq[None, :]
        return jnp.cos(ang), jnp.sin(ang)
    a_hi = (jnp.arange(n_pos // LANES, dtype=jnp.int32) * LANES).astype(F32)[:, None] * freq[None, :]
    a_lo = jnp.arange(LANES, dtype=jnp.int32).astype(F32)[:, None] * freq[None, :]
    ch, sh = jnp.cos(a_hi)[:, None, :], jnp.sin(a_hi)[:, None, :]
    cl, sl = jnp.cos(a_lo)[None, :, :], jnp.sin(a_lo)[None, :, :]
    return (ch * cl - sh * sl).reshape(n_pos, LANES), (sh * cl + ch * sl).reshape(n_pos, LANES)


def _rot_tables_a(pos_start, n_pos):
    half = A_ROT_DIM // 2
    inv_freq = ROPE_THETA_A ** (-jnp.arange(half, dtype=F32) / half)
    d = jnp.arange(LANES, dtype=jnp.int32) % A_HEAD_DIM
    freq = jnp.where(d < A_ROT_DIM, inv_freq[d % half], 0.0)
    cos, sin = _cos_sin(pos_start, n_pos, freq)
    s1 = jnp.where(d < half, -sin, 0.0)
    s2 = jnp.where((d >= half) & (d < A_ROT_DIM), sin, 0.0)
    return cos, s1, s2


def _rot_tables_b(pos_start, n_pos):
    half = B_DK // 2
    inv_freq = ROPE_THETA_B ** (-jnp.arange(half, dtype=F32) / half)
    cos, sin = _cos_sin(pos_start, n_pos, jnp.tile(inv_freq, 2))
    return cos, jnp.where(jnp.arange(LANES) < half, -sin, sin)


def _log_decay():
    return jnp.log(1.0 - 2.0 ** (-5.0 - jnp.arange(B_HEADS, dtype=F32)))


def _decay_tables(C):
    lg = _log_decay()
    idx = jnp.arange(C, dtype=F32)
    diff = idx[:, None] - idx[None, :]
    dmat = jnp.where(diff >= 0, jnp.exp(lg[:, None, None] * jnp.maximum(diff, 0.0)), 0.0)
    q_dec = jnp.exp(lg[None, :] * (idx[:, None] + 1.0))
    k_dec = jnp.exp(lg[None, :] * (C - 1.0 - idx[:, None]))
    gam = jnp.exp(lg * C)
    return dmat, q_dec, k_dec, gam


def kernel(x_prompt, x_sample, cache_swa_k, cache_swa_v, state_ret, w_in, w_out, norm_g, q_norm_g, k_norm_g, sinks):
    N, T, D = x_prompt.shape
    nb, dec, _ = x_sample.shape
    w_buf = cache_swa_k.shape[2]
    assert w_buf == WINDOW

    rot_a_p, rot_b_p = _rot_tables_a(0, T), _rot_tables_b(0, T)
    rot_a_s = tuple(jnp.tile(a, (nb, 1)) for a in _rot_tables_a(PAST_LEN, dec))
    rot_b_s = tuple(jnp.tile(a, (nb, 1)) for a in _rot_tables_b(PAST_LEN, dec))

    dmat, q_dec, k_dec, gam_p = _decay_tables(RET_CHUNK)
    qdec_p = jnp.broadcast_to(q_dec.T[:, :, None], (B_HEADS, RET_CHUNK, LANES))
    kdec_p = jnp.broadcast_to(k_dec.T[:, :, None], (B_HEADS, RET_CHUNK, LANES))
    dmat4, q_dec4, k_dec4, gam_s = _decay_tables(dec)
    d8 = jnp.kron(jnp.eye(2, dtype=F32)[None], jnp.ones((1, dec, dec), F32)) * jnp.tile(dmat4, (1, 2, 2))
    qdec_s = jnp.broadcast_to(jnp.tile(q_dec4.T, (1, 2))[:, :, None], (B_HEADS, 2 * dec, LANES))
    kdec_s = jnp.broadcast_to(jnp.tile(k_dec4.T, (1, 2))[:, :, None], (B_HEADS, 2 * dec, LANES))

    w_in_b = w_in.astype(BF16)
    w_out_b = w_out.astype(BF16)
    ck = cache_swa_k.transpose(0, 1, 3, 4, 2).reshape(DEPTH, nb, A_KV_WIDTH, w_buf)
    cv = cache_swa_v.transpose(0, 1, 3, 4, 2).reshape(DEPTH, nb, A_KV_WIDTH, w_buf)

    ng = norm_g[:, None, :]
    gq = jnp.tile(q_norm_g, (1, HEADS_PER_TILE))[:, None, :] * (A_HEAD_DIM ** -0.5 * LOG2E)
    gk = jnp.tile(k_norm_g, (1, HEADS_PER_TILE))[:, None, :]
    sinks2 = sinks * LOG2E

    sink_col = jnp.repeat(sinks2, 2 * dec, axis=1)[:, :, None]
    xs, ks, vs, ss = _sample_layers(x_sample.reshape(nb * dec, D), w_in_b, w_out_b, ng, gq, gk, sink_col, gam_s,
                                    rot_a_s, rot_b_s, d8, qdec_s, kdec_s, ck, cv, state_ret, dec)

    xp = x_prompt
    kp_l, vp_l, sp_l = [], [], []
    for l in range(DEPTH):
        xp, kl, vl, S = _prompt_layer(l, xp, w_in_b, w_out_b, ng, gq, gk, sinks2[l], gam_p,
                                      rot_a_p, rot_b_p, dmat, qdec_p, kdec_p)
        kp_l.append(kl)
        vp_l.append(vl)
        sp_l.append(S)

    kv_p = (DEPTH, N, WINDOW, A_KV_HEADS, A_HEAD_DIM)
    kv_t = (DEPTH, nb, A_KV_HEADS, A_HEAD_DIM, w_buf)
    return (xp, xs.reshape(nb, dec, D),
            jnp.stack(kp_l).reshape(kv_p), jnp.stack(vp_l).reshape(kv_p), jnp.stack(sp_l),
            ks.reshape(kv_t).transpose(0, 1, 4, 2, 3), vs.reshape(kv_t).transpose(0, 1, 4, 2, 3), ss)
```

```python
import functools

import jax
import jax.numpy as jnp
from jax import lax
from jax.experimental import pallas as pl
from jax.experimental.pallas import tpu as pltpu

D_MODEL = 1024
DEPTH = 4
PAST_LEN = 8192
A_HEADS = 8
A_KV_HEADS = 2
A_GROUP = A_HEADS // A_KV_HEADS
A_HEAD_DIM = 64
WINDOW = 128
A_ROT_DIM = A_HEAD_DIM // 4
ROPE_THETA_A = 500000.0
B_HEADS = 4
B_DK = 128
B_DV = 128
RET_CHUNK = 128
ROPE_THETA_B = 10000.0
EPS = 1e-6

A_WIDTH = A_HEADS * A_HEAD_DIM
A_KV_WIDTH = A_KV_HEADS * A_HEAD_DIM
B_QK_WIDTH = B_HEADS * B_DK
B_V_WIDTH = B_HEADS * B_DV
MIX_WIDTH = A_WIDTH + B_V_WIDTH
OFF_QA = 0
OFF_KA = OFF_QA + A_WIDTH
OFF_VA = OFF_KA + A_KV_WIDTH
OFF_GA = OFF_VA + A_KV_WIDTH
OFF_QB = OFF_GA + A_WIDTH
OFF_KB = OFF_QB + B_QK_WIDTH
OFF_VB = OFF_KB + B_QK_WIDTH
OFF_GB = OFF_VB + B_V_WIDTH
IN_WIDTH = OFF_GB + B_V_WIDTH

LANES = 128
HEADS_PER_TILE = LANES // A_HEAD_DIM
A_TILES = A_WIDTH // LANES
NEG = -1e30
LOG2E = 1.4426950408889634

PROMPT_BLOCK = 512
SAMPLE_SEQS = 8
VMEM_LIMIT = 56 * 1024 * 1024

F32 = jnp.float32
BF16 = jnp.bfloat16


def _dot(a, b):
    return jnp.dot(a, b, preferred_element_type=F32)


def _dot_nt(a, b):
    return lax.dot_general(a, b, (((1,), (1,)), ((), ())), preferred_element_type=F32)


def _dot_tn(a, b):
    return lax.dot_general(a, b, (((0,), (0,)), ((), ())), preferred_element_type=F32)


def _silu(x):
    return x * (1.0 / (1.0 + jnp.exp(-x)))


def _row_rms(x):
    n = x.shape[-1]
    return x * lax.rsqrt(jnp.sum(x * x, axis=-1, keepdims=True) * (1.0 / n) + EPS)


def _pair_norm_rot(blk, g_row, ca, sa1, sa2, lo_mask):
    sq = blk * blk
    tot = jnp.sum(sq, axis=-1, keepdims=True)
    lo = jnp.sum(jnp.where(lo_mask, sq, 0.0), axis=-1, keepdims=True)
    hi = tot - lo
    inv_lo = lax.rsqrt(lo * (1.0 / A_HEAD_DIM) + EPS)
    inv_hi = lax.rsqrt(hi * (1.0 / A_HEAD_DIM) + EPS)
    xn = blk * jnp.where(lo_mask, inv_lo, inv_hi) * g_row
    half = A_ROT_DIM // 2
    return xn * ca + pltpu.roll(xn, LANES - half, 1) * sa1 + pltpu.roll(xn, half, 1) * sa2


def _rot_b(blk, cb, sb):
    return blk * cb + pltpu.roll(blk, B_DK // 2, 1) * sb


def _softmax_parts(parts, sink):
    mx = sink
    for s in parts:
        mx = jnp.maximum(mx, jnp.max(s, axis=-1, keepdims=True))
    ps = [jnp.exp2(s - mx) for s in parts]
    den = jnp.exp2(sink - mx)
    for p in ps:
        den = den + jnp.sum(p, axis=-1, keepdims=True)
    rinv = 1.0 / den
    return [p * rinv for p in ps]


def _prompt_kernel(sinks_ref, gam_ref, x_ref, xres_ref, win_ref, wout_ref, ng_ref, gq_ref, gk_ref,
                   ca_ref, sa1_ref, sa2_ref, cb_ref, sb_ref, dmat_ref, qdec_ref, kdec_ref,
                   y_ref, klast_ref, vlast_ref, S_ref,
                   q_s, kz_s, vt_s, qb_s, kb_s, kbd_s, vb_s, g_s, mix_s, bias_s, s_s, p_s, es_s,
                   sc_s, kv_s, sb_s, *, tm, nt, last_set):
    g = pl.program_id(0)
    last = pl.num_programs(0) - 1
    nchunk = tm // RET_CHUNK
    C = RET_CHUNK
    t_fin = jnp.maximum(g - 1, 0) % nt

    @pl.when(g == 0)
    def _():
        kj = lax.broadcasted_iota(jnp.int32, (2 * C, C), 0)
        qi = lax.broadcasted_iota(jnp.int32, (2 * C, C), 1)
        bias_s[...] = jnp.where((kj > qi) & (kj <= qi + C), 0.0, NEG)

    @pl.when(t_fin == 0)
    def _():
        S_ref[...] = jnp.zeros_like(S_ref)

    def tile(a, i):
        return a[:, i * LANES:(i + 1) * LANES]

    def body(w, r):
        lo_mask = lax.broadcasted_iota(jnp.int32, (tm, LANES), 1) < A_HEAD_DIM
        ca, sa1, sa2 = ca_ref[...], sa1_ref[...], sa2_ref[...]
        cb, sb = cb_ref[...], sb_ref[...]
        first_bias = jnp.where(t_fin == 0, NEG, 0.0)

        def chunk_rows(c):
            return slice(c * C, (c + 1) * C)

        def window_rows(c):
            return slice(c * C, (c + 2) * C)

        def group_heads(grp):
            return range(grp * A_GROUP, (grp + 1) * A_GROUP)

        def window_cols(c):
            return slice(c * C, (c + 2) * C)

        def scores(c):
            for grp in range(A_KV_HEADS):
                tiles = range(grp * A_GROUP // HEADS_PER_TILE, (grp + 1) * A_GROUP // HEADS_PER_TILE)
                qp = jnp.concatenate([q_s[r, chunk_rows(c), m * LANES:(m + 1) * LANES] for m in tiles], axis=0)
                for half in range(HEADS_PER_TILE):
                    st = _dot_nt(kz_s[r, 2 * grp + half, window_rows(c), :], qp)
                    for i, m in enumerate(tiles):
                        s = st[:, i * C:(i + 1) * C] + bias_s[...]
                        if c == 0:
                            s = jnp.concatenate([s[:C, :] + first_bias, s[C:, :]], axis=0)
                        s_s[c * A_HEADS + m * HEADS_PER_TILE + half] = s

        def softmax(c):
            for grp in range(A_KV_HEADS):
                for i, hd in enumerate(group_heads(grp)):
                    s = s_s[c * A_HEADS + hd]
                    sink = sinks_ref[hd]
                    mx = jnp.maximum(jnp.max(s, axis=0, keepdims=True), sink)
                    p_s[c * A_KV_HEADS + grp, :, i * C:(i + 1) * C] = jnp.exp2(s - mx).astype(BF16)
                    es_s[c * A_KV_HEADS + grp, :, i * C:(i + 1) * C] = jnp.exp2(sink - mx)

        def attend(c):
            for grp in range(A_KV_HEADS):
                ot = _dot(vt_s[r, grp, :, window_cols(c)], p_s[c * A_KV_HEADS + grp])
                den = ot[A_HEAD_DIM:A_HEAD_DIM + 1, :] + es_s[c * A_KV_HEADS + grp]
                o = ot[0:A_HEAD_DIM, :] * (1.0 / den)
                for i in range(A_GROUP // HEADS_PER_TILE):
                    m = grp * A_GROUP // HEADS_PER_TILE + i
                    sl = slice(m * LANES, (m + 1) * LANES)
                    pair = jnp.concatenate([o[:, (HEADS_PER_TILE * i + j) * C:(HEADS_PER_TILE * i + j + 1) * C]
                                            for j in range(HEADS_PER_TILE)], axis=0)
                    mix_s[chunk_rows(c), sl] = (g_s[r, chunk_rows(c), sl] * pair.T).astype(BF16)

        def retain_local(c):
            for hh in range(B_HEADS):
                sl = slice(hh * LANES, (hh + 1) * LANES)
                rows = chunk_rows(c)
                sc_s[c * B_HEADS + hh] = (_dot_nt(qb_s[r, rows, sl], kb_s[r, rows, sl]) * dmat_ref[hh]).astype(BF16)
                kv_s[c * B_HEADS + hh] = _dot_tn(kbd_s[r, rows, sl], vb_s[r, rows, sl])

        def retain_states():
            for hh in range(B_HEADS):
                S = S_ref[0, hh]
                for c in range(nchunk):
                    sb_s[c * B_HEADS + hh] = S.astype(BF16)
                    S = gam_ref[hh] * S + kv_s[c * B_HEADS + hh]
                S_ref[0, hh] = S

        def retain_out(c):
            for hh in range(B_HEADS):
                sl = slice(hh * LANES, (hh + 1) * LANES)
                rows = chunk_rows(c)
                i = c * B_HEADS + hh
                ret = _dot(sc_s[i], vb_s[r, rows, sl]) + _dot(qb_s[r, rows, sl], sb_s[i]) * qdec_ref[hh]
                osl = slice(A_WIDTH + hh * LANES, A_WIDTH + (hh + 1) * LANES)
                mix_s[rows, osl] = (g_s[r, rows, osl] * _row_rms(ret)).astype(BF16)

        def project_attention():
            qa = _dot(h, win_ref[:, OFF_QA:OFF_QA + A_WIDTH])
            for m in range(A_TILES):
                q_s[w, :, m * LANES:(m + 1) * LANES] = _pair_norm_rot(
                    tile(qa, m), gq_ref[...], ca, sa1, sa2, lo_mask).astype(BF16)
            kv = _dot(h, win_ref[:, OFF_KA:OFF_KA + 2 * A_KV_WIDTH])
            kr = _pair_norm_rot(tile(kv, 0), gk_ref[...], ca, sa1, sa2, lo_mask)
            va = tile(kv, 1)
            klast_ref[0] = kr[tm - WINDOW:, :]
            vlast_ref[0] = va[tm - WINDOW:, :]
            sw = pltpu.roll(kr, A_HEAD_DIM, 1)
            kz_s[w, 0, C:, :] = jnp.where(lo_mask, kr, 0.0).astype(BF16)
            kz_s[w, 1, C:, :] = jnp.where(lo_mask, 0.0, sw).astype(BF16)
            kz_s[w, 2, C:, :] = jnp.where(lo_mask, sw, 0.0).astype(BF16)
            kz_s[w, 3, C:, :] = jnp.where(lo_mask, 0.0, kr).astype(BF16)
            vat = va.T
            for grp in range(A_KV_HEADS):
                vt_s[w, grp, 0:A_HEAD_DIM, C:] = vat[grp * A_HEAD_DIM:(grp + 1) * A_HEAD_DIM, :].astype(BF16)
                vt_s[w, grp, A_HEAD_DIM:, C:] = jnp.ones((LANES - A_HEAD_DIM, tm), BF16)
            for i in range(2 * A_KV_HEADS):
                kz_s[w, i, 0:C, :] = jnp.zeros((C, LANES), BF16) if r is None else kz_s[r, i, tm:tm + C, :]
            for grp in range(A_KV_HEADS):
                vt_s[w, grp, :, 0:C] = jnp.zeros((LANES, C), BF16) if r is None else vt_s[r, grp, :, tm:tm + C]

        def project_gate_a():
            ga = _dot(h, win_ref[:, OFF_GA:OFF_GA + A_WIDTH])
            for m in range(A_TILES):
                g_s[w, :, m * LANES:(m + 1) * LANES] = _silu(tile(ga, m))

        def project_qb():
            qb = _dot(h, win_ref[:, OFF_QB:OFF_QB + B_QK_WIDTH])
            for hh in range(B_HEADS):
                qb_s[w, :, hh * LANES:(hh + 1) * LANES] = _rot_b(tile(qb, hh), cb, sb).astype(BF16)

        def project_kb():
            kb = _dot(h, win_ref[:, OFF_KB:OFF_KB + B_QK_WIDTH])
            for hh in range(B_HEADS):
                sl = slice(hh * LANES, (hh + 1) * LANES)
                kbh = _rot_b(tile(kb, hh), cb, sb) * (B_DK ** -0.5)
                kb_s[w, :, sl] = kbh.astype(BF16)
                kbd_s[w, :, sl] = (kbh.reshape(nchunk, C, LANES) * kdec_ref[hh][None]).reshape(tm, LANES).astype(BF16)

        def project_vb_gate_b():
            vb_s[w] = _dot(h, win_ref[:, OFF_VB:OFF_VB + B_V_WIDTH]).astype(BF16)
            gb = _dot(h, win_ref[:, OFF_GB:OFF_GB + B_V_WIDTH])
            for hh in range(B_HEADS):
                g_s[w, :, A_WIDTH + hh * LANES:A_WIDTH + (hh + 1) * LANES] = _silu(tile(gb, hh))

        project, finish = w is not None, r is not None
        if project:
            h = (_row_rms(x_ref[0]) * ng_ref[...]).astype(BF16)
        if finish:
            for c in range(nchunk):
                scores(c)
                retain_local(c)
        if project:
            project_attention()
        if finish:
            for c in range(nchunk):
                softmax(c)
            retain_states()
        if project:
            project_gate_a()
            project_qb()
        if finish:
            for c in range(nchunk):
                attend(c)
                retain_out(c)
        if project:
            project_kb()
            project_vb_gate_b()
        if finish:
            y_ref[0] = xres_ref[0] + _dot(mix_s[...], wout_ref[...])

    steady = jnp.logical_and(g > 0, g < last)

    @pl.when(g == 0)
    def _():
        body(0, None)

    @pl.when(jnp.logical_and(steady, g % 2 == 0))
    def _():
        body(0, 1)

    @pl.when(jnp.logical_and(steady, g % 2 == 1))
    def _():
        body(1, 0)

    @pl.when(g == last)
    def _():
        body(None, last_set)


def _const_spec(shape):
    nd = len(shape)
    return pl.BlockSpec(shape, lambda *_: (0,) * nd, pipeline_mode=pl.Buffered(1))


def _layer_spec(shape, l):
    nd = len(shape)
    return pl.BlockSpec((None,) + tuple(shape), lambda *_: (l,) + (0,) * nd, pipeline_mode=pl.Buffered(1))


def _prompt_layer(l, x, win, wout, ng, gq, gk, sinks, gam, rot_a, rot_b, dmat, qdec, kdec):
    N, T, D = x.shape
    tm = PROMPT_BLOCK
    assert T % tm == 0 and tm % RET_CHUNK == 0 and WINDOW == RET_CHUNK
    nt = T // tm
    nblk = N * nt
    nchunk = tm // RET_CHUNK

    def proj_blk(g):
        return jnp.minimum(g, nblk - 1)

    def fin_blk(g):
        return jnp.maximum(g - 1, 0)

    smem = pl.BlockSpec(memory_space=pltpu.SMEM)
    tab = pl.BlockSpec((tm, LANES), lambda g: (proj_blk(g) % nt, 0))
    x_proj = pl.BlockSpec((1, tm, D), lambda g: (proj_blk(g) // nt, proj_blk(g) % nt, 0))
    x_fin = pl.BlockSpec((1, tm, D), lambda g: (fin_blk(g) // nt, fin_blk(g) % nt, 0))
    kv_out = pl.BlockSpec((1, WINDOW, LANES), lambda g: (proj_blk(g) // nt, 0, 0))

    def two(*shape, dtype):
        return pltpu.VMEM((2,) + shape, dtype)

    return pl.pallas_call(
        functools.partial(_prompt_kernel, tm=tm, nt=nt, last_set=(nblk - 1) % 2),
        grid=(nblk + 1,),
        in_specs=[smem, smem, x_proj, x_fin,
                  _layer_spec((D, IN_WIDTH), l), _layer_spec((MIX_WIDTH, D), l),
                  _layer_spec((1, D), l), _layer_spec((1, LANES), l), _layer_spec((1, LANES), l),
                  tab, tab, tab, tab, tab,
                  _const_spec((B_HEADS, RET_CHUNK, RET_CHUNK)), _const_spec((B_HEADS, RET_CHUNK, LANES)),
                  _const_spec((B_HEADS, RET_CHUNK, LANES))],
        out_specs=[x_fin, kv_out, kv_out,
                   pl.BlockSpec((1, B_HEADS, B_DK, B_DV), lambda g: (fin_blk(g) // nt, 0, 0, 0))],
        out_shape=[jax.ShapeDtypeStruct((N, T, D), F32),
                   jax.ShapeDtypeStruct((N, WINDOW, LANES), F32),
                   jax.ShapeDtypeStruct((N, WINDOW, LANES), F32),
                   jax.ShapeDtypeStruct((N, B_HEADS, B_DK, B_DV), F32)],
        scratch_shapes=[two(tm, A_WIDTH, dtype=BF16),
                        two(2 * A_KV_HEADS, RET_CHUNK + tm, LANES, dtype=BF16),
                        two(A_KV_HEADS, LANES, RET_CHUNK + tm, dtype=BF16),
                        two(tm, B_QK_WIDTH, dtype=BF16),
                        two(tm, B_QK_WIDTH, dtype=BF16),
                        two(tm, B_QK_WIDTH, dtype=BF16),
                        two(tm, B_V_WIDTH, dtype=BF16),
                        two(tm, MIX_WIDTH, dtype=F32),
                        pltpu.VMEM((tm, MIX_WIDTH), BF16),
                        pltpu.VMEM((2 * RET_CHUNK, RET_CHUNK), F32),
                        pltpu.VMEM((nchunk * A_HEADS, 2 * RET_CHUNK, RET_CHUNK), F32),
                        pltpu.VMEM((nchunk * A_KV_HEADS, 2 * RET_CHUNK, A_GROUP * RET_CHUNK), BF16),
                        pltpu.VMEM((nchunk * A_KV_HEADS, 1, A_GROUP * RET_CHUNK), F32),
                        pltpu.VMEM((nchunk * B_HEADS, RET_CHUNK, RET_CHUNK), BF16),
                        pltpu.VMEM((nchunk * B_HEADS, B_DK, B_DV), F32),
                        pltpu.VMEM((nchunk * B_HEADS, B_DK, B_DV), BF16)],
        compiler_params=pltpu.CompilerParams(dimension_semantics=("arbitrary",),
                                             vmem_limit_bytes=VMEM_LIMIT),
        name="prompt_layer",
    )(sinks, gam, x, x, win, wout, ng, gq, gk, *rot_a, *rot_b, dmat, qdec, kdec)


def _sample_kernel(gam_ref, x_ref, win_ref, wout_ref, ng_ref, gq_ref, gk_ref,
                   ca_ref, sa1_ref, sa2_ref, cb_ref, sb_ref, sink_ref, d8_ref, qdec_ref, kdec_ref,
                   ck_ref, cv_ref, st_ref,
                   y_ref, ko_ref, vo_ref, so_ref,
                   xs_s, qz_s, kn_s, vn_s, knt_s, vnt_s, qb_s, kb_s, vb_s, g_s, mix_s, *, nseq, dec):
    layer = pl.program_id(0)
    step = pl.program_id(1)
    ntok = x_ref.shape[0]
    pair_rows = 2 * dec
    npair = nseq // 2

    @pl.when(jnp.logical_and(layer == 0, step == 0))
    def _():
        xs_s[...] = x_ref[...]

    @pl.when(step == 0)
    def _():
        h = (_row_rms(xs_s[...]) * ng_ref[...]).astype(BF16)
        lo_mask = lax.broadcasted_iota(jnp.int32, (ntok, LANES), 1) < A_HEAD_DIM
        ca, sa1, sa2 = ca_ref[...], sa1_ref[...], sa2_ref[...]
        cb, sb = cb_ref[...], sb_ref[...]

        def tile(a, i):
            return a[:, i * LANES:(i + 1) * LANES]

        qa = _dot(h, win_ref[:, OFF_QA:OFF_QA + A_WIDTH])
        for m in range(A_TILES):
            qr = _pair_norm_rot(tile(qa, m), gq_ref[...], ca, sa1, sa2, lo_mask)
            sw = pltpu.roll(qr, A_HEAD_DIM, 1)
            if m < A_TILES // 2:
                qz_s[2 * m] = jnp.where(lo_mask, qr, 0.0)
                qz_s[2 * m + 1] = jnp.where(lo_mask, sw, 0.0)
            else:
                qz_s[2 * m] = jnp.where(lo_mask, 0.0, sw)
                qz_s[2 * m + 1] = jnp.where(lo_mask, 0.0, qr)
        kv = _dot(h, win_ref[:, OFF_KA:OFF_KA + 2 * A_KV_WIDTH])
        kn = _pair_norm_rot(tile(kv, 0), gk_ref[...], ca, sa1, sa2, lo_mask)
        vn = tile(kv, 1)
        kn_s[...] = kn
        vn_s[...] = vn
        for tt in range(ntok // LANES):
            knt_s[tt] = kn[tt * LANES:(tt + 1) * LANES, :].T
            vnt_s[tt] = vn[tt * LANES:(tt + 1) * LANES, :].T
        ga = _dot(h, win_ref[:, OFF_GA:OFF_GA + A_WIDTH])
        for m in range(A_TILES):
            g_s[:, m * LANES:(m + 1) * LANES] = _silu(tile(ga, m))
        qb = _dot(h, win_ref[:, OFF_QB:OFF_QB + B_QK_WIDTH])
        for hh in range(B_HEADS):
            qb_s[:, hh * LANES:(hh + 1) * LANES] = _rot_b(tile(qb, hh), cb, sb)
        kb = _dot(h, win_ref[:, OFF_KB:OFF_KB + B_QK_WIDTH])
        for hh in range(B_HEADS):
            kb_s[:, hh * LANES:(hh + 1) * LANES] = _rot_b(tile(kb, hh), cb, sb) * (B_DK ** -0.5)
        vb_s[...] = _dot(h, win_ref[:, OFF_VB:OFF_VB + B_V_WIDTH])
        gb = _dot(h, win_ref[:, OFF_GB:OFF_GB + B_V_WIDTH])
        for hh in range(B_HEADS):
            g_s[:, A_WIDTH + hh * LANES:A_WIDTH + (hh + 1) * LANES] = _silu(tile(gb, hh))

    nq = A_HEADS * pair_rows
    row = lax.broadcasted_iota(jnp.int32, (nq, LANES), 0)
    slot = lax.broadcasted_iota(jnp.int32, (nq, LANES), 1)
    first_seq = (row % pair_rows) < dec
    cache_mask = slot > (row % dec)
    rown = lax.broadcasted_iota(jnp.int32, (nq, pair_rows), 0)
    coln = lax.broadcasted_iota(jnp.int32, (nq, pair_rows), 1)
    new_mask = ((coln // dec) == ((rown % pair_rows) // dec)) & ((coln % dec) <= (rown % dec))
    lo8 = lax.broadcasted_iota(jnp.int32, (pair_rows, LANES), 1) < A_HEAD_DIM
    r8 = lax.broadcasted_iota(jnp.int32, (pair_rows, LANES), 0)
    keep = lax.broadcasted_iota(jnp.int32, (LANES, WINDOW), 1) < WINDOW - dec
    sink = sink_ref[...]

    def pair_rows_of(p):
        seq0 = step * nseq + 2 * p
        return pl.ds(pl.multiple_of(seq0 * dec, pair_rows), pair_rows)


    stage1 = []
    for p in range(npair):
        b0 = 2 * p
        rows = pair_rows_of(p)
        qz = qz_s[:, rows, :].reshape(nq, LANES).astype(BF16)
        s_c = jnp.where(first_seq, _dot(qz, ck_ref[b0].astype(BF16)), _dot(qz, ck_ref[b0 + 1].astype(BF16)))
        s_n = _dot_nt(qz, kn_s[rows, :].astype(BF16))
        rets = []
        for hh in range(B_HEADS):
            sl = slice(hh * LANES, (hh + 1) * LANES)
            q = qb_s[rows, sl].astype(BF16)
            kf = kb_s[rows, sl]
            kd = kf * kdec_ref[hh]
            v = vb_s[rows, sl].astype(BF16)
            sc = (_dot_nt(q, kf.astype(BF16)) * d8_ref[hh]).astype(BF16)
            crosses = []
            for j in range(2):
                S = st_ref[b0 + j, hh]
                crosses.append(_dot(q, S.astype(BF16)))
                kdj = jnp.where((r8 // dec) == j, kd, 0.0).astype(BF16)
                so_ref[b0 + j, hh] = gam_ref[hh] * S + _dot_tn(kdj, v)
            cross = jnp.where(r8 < dec, crosses[0], crosses[1]) * qdec_ref[hh]
            rets.append((sc, v, cross))
        stage1.append((s_c, s_n, rets))

    probs = []
    for s_c, s_n, _ in stage1:
        p_c, p_n = _softmax_parts([jnp.where(cache_mask, s_c, NEG), jnp.where(new_mask, s_n, NEG)], sink)
        probs.append((p_c.astype(BF16), p_n.astype(BF16)))

    for p in range(npair):
        b0 = 2 * p
        rows = pair_rows_of(p)
        p_c, p_n = probs[p]
        o = jnp.where(first_seq, _dot_nt(p_c, cv_ref[b0].astype(BF16)), _dot_nt(p_c, cv_ref[b0 + 1].astype(BF16)))
        o = o + _dot(p_n, vn_s[rows, :].astype(BF16))
        for m in range(A_TILES):
            oa = o[(2 * m) * pair_rows:(2 * m + 1) * pair_rows, :]
            ob = o[(2 * m + 1) * pair_rows:(2 * m + 2) * pair_rows, :]
            if m < A_TILES // 2:
                blk = jnp.where(lo8, oa, pltpu.roll(ob, A_HEAD_DIM, 1))
            else:
                blk = jnp.where(lo8, pltpu.roll(oa, A_HEAD_DIM, 1), ob)
            sl = slice(m * LANES, (m + 1) * LANES)
            mix_s[rows, sl] = g_s[rows, sl] * blk
        for hh, (sc, v, cross) in enumerate(stage1[p][2]):
            osl = slice(A_WIDTH + hh * LANES, A_WIDTH + (hh + 1) * LANES)
            mix_s[rows, osl] = g_s[rows, osl] * _row_rms(_dot(sc, v) + cross)

    for b in range(nseq):
        tok0 = (step * nseq + b) * dec
        tt = tok0 // LANES
        shift = (WINDOW - dec) - tok0 % LANES
        ko_ref[b] = jnp.where(keep, pltpu.roll(ck_ref[b], WINDOW - dec, 1), pltpu.roll(knt_s[tt], shift, 1))
        vo_ref[b] = jnp.where(keep, pltpu.roll(cv_ref[b], WINDOW - dec, 1), pltpu.roll(vnt_s[tt], shift, 1))

    @pl.when(step == pl.num_programs(1) - 1)
    def _():
        xs_s[...] = xs_s[...] + _dot(mix_s[...].astype(BF16), wout_ref[...])

    @pl.when(jnp.logical_and(layer == pl.num_programs(0) - 1, step == pl.num_programs(1) - 1))
    def _():
        y_ref[...] = xs_s[...]


def _stacked_spec(shape):
    nd = len(shape)
    return pl.BlockSpec((None,) + tuple(shape), lambda l, i: (l,) + (0,) * nd)


def _sample_layers(x, win, wout, ng, gq, gk, sink_col, gam, rot_a, rot_b, d8, qdec, kdec, ck, cv, st, dec):
    ntok, D = x.shape
    depth, nb = ck.shape[:2]
    nseq = SAMPLE_SEQS
    assert nb % nseq == 0 and nseq % 2 == 0 and ntok == nb * dec and 2 * dec == 8 and ntok % LANES == 0
    assert LANES % dec == 0 and WINDOW == LANES
    pair_rows = 2 * dec
    nq = A_HEADS * pair_rows
    smem = pl.BlockSpec(memory_space=pltpu.SMEM)
    kv = pl.BlockSpec((None, nseq, A_KV_WIDTH, WINDOW), lambda l, i: (l, i, 0, 0))
    stt = pl.BlockSpec((None, nseq, B_HEADS, B_DK, B_DV), lambda l, i: (l, i, 0, 0, 0))
    tab = _const_spec((ntok, LANES))
    return pl.pallas_call(
        functools.partial(_sample_kernel, nseq=nseq, dec=dec),
        grid=(depth, nb // nseq),
        in_specs=[smem, _const_spec((ntok, D)), _stacked_spec((D, IN_WIDTH)), _stacked_spec((MIX_WIDTH, D)),
                  _stacked_spec((1, D)), _stacked_spec((1, LANES)), _stacked_spec((1, LANES)),
                  tab, tab, tab, tab, tab,
                  _stacked_spec((nq, 1)), _const_spec((B_HEADS, pair_rows, pair_rows)),
                  _const_spec((B_HEADS, pair_rows, LANES)), _const_spec((B_HEADS, pair_rows, LANES)),
                  kv, kv, stt],
        out_specs=[pl.BlockSpec((ntok, D), lambda l, i: (0, 0)), kv, kv, stt],
        out_shape=[jax.ShapeDtypeStruct((ntok, D), F32),
                   jax.ShapeDtypeStruct(ck.shape, F32),
                   jax.ShapeDtypeStruct(cv.shape, F32),
                   jax.ShapeDtypeStruct(st.shape, F32)],
        scratch_shapes=[pltpu.VMEM((ntok, D), F32),
                        pltpu.VMEM((A_HEADS, ntok, LANES), F32),
                        pltpu.VMEM((ntok, LANES), F32),
                        pltpu.VMEM((ntok, LANES), F32),
                        pltpu.VMEM((ntok // LANES, A_KV_WIDTH, LANES), F32),
                        pltpu.VMEM((ntok // LANES, A_KV_WIDTH, LANES), F32),
                        pltpu.VMEM((ntok, B_QK_WIDTH), F32),
                        pltpu.VMEM((ntok, B_QK_WIDTH), F32),
                        pltpu.VMEM((ntok, B_V_WIDTH), F32),
                        pltpu.VMEM((ntok, MIX_WIDTH), F32),
                        pltpu.VMEM((ntok, MIX_WIDTH), F32)],
        compiler_params=pltpu.CompilerParams(dimension_semantics=("arbitrary", "arbitrary"),
                                             vmem_limit_bytes=VMEM_LIMIT),
        name="sample_layers",
    )(gam, x, win, wout, ng, gq, gk, *rot_a, *rot_b, sink_col, d8, qdec, kdec, ck, cv, st)


def _cos_sin(pos_start, n_pos, freq):
    if n_pos % LANES or pos_start:
        ang = (pos_start + jnp.arange(n_pos, dtype=jnp.int32)).astype(F32)[:, None] * freq[None, :]
        return jnp.cos(ang), jnp.sin(ang)
    a_hi = (jnp.arange(n_pos // LANES, dtype=jnp.int32) * LANES).astype(F32)[:, None] * freq[None, :]
    a_lo = jnp.arange(LANES, dtype=jnp.int32).astype(F32)[:, None] * freq[None, :]
    ch, sh = jnp.cos(a_hi)[:, None, :], jnp.sin(a_hi)[:, None, :]
    cl, sl = jnp.cos(a_lo)[None, :, :], jnp.sin(a_lo)[None, :, :]
    return (ch * cl - sh * sl).reshape(n_pos, LANES), (sh * cl + ch * sl).reshape(n_pos, LANES)


def _rot_tables_a(pos_start, n_pos):
    half = A_ROT_DIM // 2
    inv_freq = ROPE_THETA_A ** (-jnp.arange(half, dtype=F32) / half)
    d = jnp.arange(LANES, dtype=jnp.int32) % A_HEAD_DIM
    freq = jnp.where(d < A_ROT_DIM, inv_freq[d % half], 0.0)
    cos, sin = _cos_sin(pos_start, n_pos, freq)
    s1 = jnp.where(d < half, -sin, 0.0)
    s2 = jnp.where((d >= half) & (d < A_ROT_DIM), sin, 0.0)
    return cos, s1, s2


def _rot_tables_b(pos_start, n_pos):
    half = B_DK // 2
    inv_freq = ROPE_THETA_B ** (-jnp.arange(half, dtype=F32) / half)
    cos, sin = _cos_sin(pos_start, n_pos, jnp.tile(inv_freq, 2))
    return cos, jnp.where(jnp.arange(LANES) < half, -sin, sin)


def _log_decay():
    return jnp.log(1.0 - 2.0 ** (-5.0 - jnp.arange(B_HEADS, dtype=F32)))


def _decay_tables(C):
    lg = _log_decay()
    idx = jnp.arange(C, dtype=F32)
    diff = idx[:, None] - idx[None, :]
    dmat = jnp.where(diff >= 0, jnp.exp(lg[:, None, None] * jnp.maximum(diff, 0.0)), 0.0)
    q_dec = jnp.exp(lg[None, :] * (idx[:, None] + 1.0))
    k_dec = jnp.exp(lg[None, :] * (C - 1.0 - idx[:, None]))
    gam = jnp.exp(lg * C)
    return dmat, q_dec, k_dec, gam


def kernel(x_prompt, x_sample, cache_swa_k, cache_swa_v, state_ret, w_in, w_out, norm_g, q_norm_g, k_norm_g, sinks):
    N, T, D = x_prompt.shape
    nb, dec, _ = x_sample.shape
    w_buf = cache_swa_k.shape[2]
    assert w_buf == WINDOW

    rot_a_p, rot_b_p = _rot_tables_a(0, T), _rot_tables_b(0, T)
    rot_a_s = tuple(jnp.tile(a, (nb, 1)) for a in _rot_tables_a(PAST_LEN, dec))
    rot_b_s = tuple(jnp.tile(a, (nb, 1)) for a in _rot_tables_b(PAST_LEN, dec))

    dmat, q_dec, k_dec, gam_p = _decay_tables(RET_CHUNK)
    qdec_p = jnp.broadcast_to(q_dec.T[:, :, None], (B_HEADS, RET_CHUNK, LANES))
    kdec_p = jnp.broadcast_to(k_dec.T[:, :, None], (B_HEADS, RET_CHUNK, LANES))
    dmat4, q_dec4, k_dec4, gam_s = _decay_tables(dec)
    d8 = jnp.kron(jnp.eye(2, dtype=F32)[None], jnp.ones((1, dec, dec), F32)) * jnp.tile(dmat4, (1, 2, 2))
    qdec_s = jnp.broadcast_to(jnp.tile(q_dec4.T, (1, 2))[:, :, None], (B_HEADS, 2 * dec, LANES))
    kdec_s = jnp.broadcast_to(jnp.tile(k_dec4.T, (1, 2))[:, :, None], (B_HEADS, 2 * dec, LANES))

    w_in_b = w_in.astype(BF16)
    w_out_b = w_out.astype(BF16)
    ck = cache_swa_k.transpose(0, 1, 3, 4, 2).reshape(DEPTH, nb, A_KV_WIDTH, w_buf)
    cv = cache_swa_v.transpose(0, 1, 3, 4, 2).reshape(DEPTH, nb, A_KV_WIDTH, w_buf)

    ng = norm_g[:, None, :]
    gq = jnp.tile(q_norm_g, (1, HEADS_PER_TILE))[:, None, :] * (A_HEAD_DIM ** -0.5 * LOG2E)
    gk = jnp.tile(k_norm_g, (1, HEADS_PER_TILE))[:, None, :]
    sinks2 = sinks * LOG2E

    sink_col = jnp.repeat(sinks2, 2 * dec, axis=1)[:, :, None]
    xs, ks, vs, ss = _sample_layers(x_sample.reshape(nb * dec, D), w_in_b, w_out_b, ng, gq, gk, sink_col, gam_s,
                                    rot_a_s, rot_b_s, d8, qdec_s, kdec_s, ck, cv, state_ret, dec)

    xp = x_prompt
    kp_l, vp_l, sp_l = [], [], []
    for l in range(DEPTH):
        xp, kl, vl, S = _prompt_layer(l, xp, w_in_b, w_out_b, ng, gq, gk, sinks2[l], gam_p,
                                      rot_a_p, rot_b_p, dmat, qdec_p, kdec_p)
        kp_l.append(kl)
        vp_l.append(vl)
        sp_l.append(S)

    kv_p = (DEPTH, N, WINDOW, A_KV_HEADS, A_HEAD_DIM)
    kv_t = (DEPTH, nb, A_KV_HEADS, A_HEAD_DIM, w_buf)
    return (xp, xs.reshape(nb, dec, D),
            jnp.stack(kp_l).reshape(kv_p), jnp.stack(vp_l).reshape(kv_p), jnp.stack(sp_l),
            ks.reshape(kv_t).transpose(0, 1, 4, 2, 3), vs.reshape(kv_t).transpose(0, 1, 4, 2, 3), ss)
```

```python
import functools

import jax
import jax.numpy as jnp
from jax import lax
from jax.experimental import pallas as pl
from jax.experimental.pallas import tpu as pltpu

D_MODEL = 1024
DEPTH = 4
PAST_LEN = 8192
A_HEADS = 8
A_KV_HEADS = 2
A_GROUP = A_HEADS // A_KV_HEADS
A_HEAD_DIM = 64
WINDOW = 128
A_ROT_DIM = A_HEAD_DIM // 4
ROPE_THETA_A = 500000.0
B_HEADS = 4
B_DK = 128
B_DV = 128
RET_CHUNK = 128
ROPE_THETA_B = 10000.0
EPS = 1e-6

A_WIDTH = A_HEADS * A_HEAD_DIM
A_KV_WIDTH = A_KV_HEADS * A_HEAD_DIM
B_QK_WIDTH = B_HEADS * B_DK
B_V_WIDTH = B_HEADS * B_DV
MIX_WIDTH = A_WIDTH + B_V_WIDTH
OFF_QA = 0
OFF_KA = OFF_QA + A_WIDTH
OFF_VA = OFF_KA + A_KV_WIDTH
OFF_GA = OFF_VA + A_KV_WIDTH
OFF_QB = OFF_GA + A_WIDTH
OFF_KB = OFF_QB + B_QK_WIDTH
OFF_VB = OFF_KB + B_QK_WIDTH
OFF_GB = OFF_VB + B_V_WIDTH
IN_WIDTH = OFF_GB + B_V_WIDTH

LANES = 128
HEADS_PER_TILE = LANES // A_HEAD_DIM
A_TILES = A_WIDTH // LANES
NEG = -1e30
LOG2E = 1.4426950408889634

PROMPT_BLOCK = 512
SAMPLE_SEQS = 8
VMEM_LIMIT = 56 * 1024 * 1024

F32 = jnp.float32
BF16 = jnp.bfloat16


def _dot(a, b):
    return jnp.dot(a, b, preferred_element_type=F32)


def _dot_nt(a, b):
    return lax.dot_general(a, b, (((1,), (1,)), ((), ())), preferred_element_type=F32)


def _dot_tn(a, b):
    return lax.dot_general(a, b, (((0,), (0,)), ((), ())), preferred_element_type=F32)


def _silu(x):
    return x * (1.0 / (1.0 + jnp.exp(-x)))


def _row_rms(x):
    n = x.shape[-1]
    return x * lax.rsqrt(jnp.sum(x * x, axis=-1, keepdims=True) * (1.0 / n) + EPS)


def _pair_norm_rot(blk, g_row, ca, sa1, sa2, lo_mask):
    sq = blk * blk
    tot = jnp.sum(sq, axis=-1, keepdims=True)
    lo = jnp.sum(jnp.where(lo_mask, sq, 0.0), axis=-1, keepdims=True)
    hi = tot - lo
    inv_lo = lax.rsqrt(lo * (1.0 / A_HEAD_DIM) + EPS)
    inv_hi = lax.rsqrt(hi * (1.0 / A_HEAD_DIM) + EPS)
    xn = blk * jnp.where(lo_mask, inv_lo, inv_hi) * g_row
    half = A_ROT_DIM // 2
    return xn * ca + pltpu.roll(xn, LANES - half, 1) * sa1 + pltpu.roll(xn, half, 1) * sa2


def _rot_b(blk, cb, sb):
    return blk * cb + pltpu.roll(blk, B_DK // 2, 1) * sb


def _softmax_parts(parts, sink):
    mx = sink
    for s in parts:
        mx = jnp.maximum(mx, jnp.max(s, axis=-1, keepdims=True))
    ps = [jnp.exp2(s - mx) for s in parts]
    den = jnp.exp2(sink - mx)
    for p in ps:
        den = den + jnp.sum(p, axis=-1, keepdims=True)
    rinv = 1.0 / den
    return [p * rinv for p in ps]


def _prompt_kernel(layer_ref, sinks_ref, gam_ref, x_ref, xres_ref, win_ref, wout_ref, ng_ref, gq_ref, gk_ref,
                   ca_ref, sa1_ref, sa2_ref, cb_ref, sb_ref, dmat_ref, qdec_ref, kdec_ref,
                   y_ref, klast_ref, vlast_ref, S_ref,
                   q_s, kz_s, vt_s, qb_s, kb_s, kbd_s, vb_s, g_s, mix_s, bias_s, s_s, p_s, es_s,
                   sc_s, kv_s, sb_s, *, tm, nt, last_set):
    g = pl.program_id(0)
    last = pl.num_programs(0) - 1
    nchunk = tm // RET_CHUNK
    C = RET_CHUNK
    t_fin = jnp.maximum(g - 1, 0) % nt

    @pl.when(g == 0)
    def _():
        kj = lax.broadcasted_iota(jnp.int32, (2 * C, C), 0)
        qi = lax.broadcasted_iota(jnp.int32, (2 * C, C), 1)
        bias_s[...] = jnp.where((kj > qi) & (kj <= qi + C), 0.0, NEG)

    @pl.when(t_fin == 0)
    def _():
        S_ref[...] = jnp.zeros_like(S_ref)

    def tile(a, i):
        return a[:, i * LANES:(i + 1) * LANES]

    def body(w, r):
        lo_mask = lax.broadcasted_iota(jnp.int32, (tm, LANES), 1) < A_HEAD_DIM
        ca, sa1, sa2 = ca_ref[...], sa1_ref[...], sa2_ref[...]
        cb, sb = cb_ref[...], sb_ref[...]
        first_bias = jnp.where(t_fin == 0, NEG, 0.0)

        def chunk_rows(c):
            return slice(c * C, (c + 1) * C)

        def window_rows(c):
            return slice(c * C, (c + 2) * C)

        def group_heads(grp):
            return range(grp * A_GROUP, (grp + 1) * A_GROUP)

        def window_cols(c):
            return slice(c * C, (c + 2) * C)

        def scores(c):
            for grp in range(A_KV_HEADS):
                tiles = range(grp * A_GROUP // HEADS_PER_TILE, (grp + 1) * A_GROUP // HEADS_PER_TILE)
                qp = jnp.concatenate([q_s[r, chunk_rows(c), m * LANES:(m + 1) * LANES] for m in tiles], axis=0)
                for half in range(HEADS_PER_TILE):
                    st = _dot_nt(kz_s[r, 2 * grp + half, window_rows(c), :], qp)
                    for i, m in enumerate(tiles):
                        s = st[:, i * C:(i + 1) * C] + bias_s[...]
                        if c == 0:
                            s = jnp.concatenate([s[:C, :] + first_bias, s[C:, :]], axis=0)
                        s_s[c * A_HEADS + m * HEADS_PER_TILE + half] = s

        def softmax(c):
            for grp in range(A_KV_HEADS):
                for i, hd in enumerate(group_heads(grp)):
                    s = s_s[c * A_HEADS + hd]
                    sink = sinks_ref[hd]
                    mx = jnp.maximum(jnp.max(s, axis=0, keepdims=True), sink)
                    p_s[c * A_KV_HEADS + grp, :, i * C:(i + 1) * C] = jnp.exp2(s - mx).astype(BF16)
                    es_s[c * A_KV_HEADS + grp, :, i * C:(i + 1) * C] = jnp.exp2(sink - mx)

        def attend(c):
            for grp in range(A_KV_HEADS):
                ot = _dot(vt_s[r, grp, :, window_cols(c)], p_s[c * A_KV_HEADS + grp])
                den = ot[A_HEAD_DIM:A_HEAD_DIM + 1, :] + es_s[c * A_KV_HEADS + grp]
                o = ot[0:A_HEAD_DIM, :] * (1.0 / den)
                for i in range(A_GROUP // HEADS_PER_TILE):
                    m = grp * A_GROUP // HEADS_PER_TILE + i
                    sl = slice(m * LANES, (m + 1) * LANES)
                    pair = jnp.concatenate([o[:, (HEADS_PER_TILE * i + j) * C:(HEADS_PER_TILE * i + j + 1) * C]
                                            for j in range(HEADS_PER_TILE)], axis=0)
                    mix_s[chunk_rows(c), sl] = (g_s[r, chunk_rows(c), sl] * pair.T).astype(BF16)

        def retain_local(c):
            for hh in range(B_HEADS):
                sl = slice(hh * LANES, (hh + 1) * LANES)
                rows = chunk_rows(c)
                sc_s[c * B_HEADS + hh] = (_dot_nt(qb_s[r, rows, sl], kb_s[r, rows, sl]) * dmat_ref[hh]).astype(BF16)
                kv_s[c * B_HEADS + hh] = _dot_tn(kbd_s[r, rows, sl], vb_s[r, rows, sl])

        def retain_states():
            for hh in range(B_HEADS):
                S = S_ref[0, hh]
                for c in range(nchunk):
                    sb_s[c * B_HEADS + hh] = S.astype(BF16)
                    S = gam_ref[hh] * S + kv_s[c * B_HEADS + hh]
                S_ref[0, hh] = S

        def retain_out(c):
            for hh in range(B_HEADS):
                sl = slice(hh * LANES, (hh + 1) * LANES)
                rows = chunk_rows(c)
                i = c * B_HEADS + hh
                ret = _dot(sc_s[i], vb_s[r, rows, sl]) + _dot(qb_s[r, rows, sl], sb_s[i]) * qdec_ref[hh]
                osl = slice(A_WIDTH + hh * LANES, A_WIDTH + (hh + 1) * LANES)
                mix_s[rows, osl] = (g_s[r, rows, osl] * _row_rms(ret)).astype(BF16)

        def project_attention():
            qa = _dot(h, win_ref[:, OFF_QA:OFF_QA + A_WIDTH])
            for m in range(A_TILES):
                q_s[w, :, m * LANES:(m + 1) * LANES] = _pair_norm_rot(
                    tile(qa, m), gq_ref[...], ca, sa1, sa2, lo_mask).astype(BF16)
            kv = _dot(h, win_ref[:, OFF_KA:OFF_KA + 2 * A_KV_WIDTH])
            kr = _pair_norm_rot(tile(kv, 0), gk_ref[...], ca, sa1, sa2, lo_mask)
            va = tile(kv, 1)
            klast_ref[0] = kr[tm - WINDOW:, :]
            vlast_ref[0] = va[tm - WINDOW:, :]
            sw = pltpu.roll(kr, A_HEAD_DIM, 1)
            kz_s[w, 0, C:, :] = jnp.where(lo_mask, kr, 0.0).astype(BF16)
            kz_s[w, 1, C:, :] = jnp.where(lo_mask, 0.0, sw).astype(BF16)
            kz_s[w, 2, C:, :] = jnp.where(lo_mask, sw, 0.0).astype(BF16)
            kz_s[w, 3, C:, :] = jnp.where(lo_mask, 0.0, kr).astype(BF16)
            vat = va.T
            for grp in range(A_KV_HEADS):
                vt_s[w, grp, 0:A_HEAD_DIM, C:] = vat[grp * A_HEAD_DIM:(grp + 1) * A_HEAD_DIM, :].astype(BF16)
                vt_s[w, grp, A_HEAD_DIM:, C:] = jnp.ones((LANES - A_HEAD_DIM, tm), BF16)
            for i in range(2 * A_KV_HEADS):
                kz_s[w, i, 0:C, :] = jnp.zeros((C, LANES), BF16) if r is None else kz_s[r, i, tm:tm + C, :]
            for grp in range(A_KV_HEADS):
                vt_s[w, grp, :, 0:C] = jnp.zeros((LANES, C), BF16) if r is None else vt_s[r, grp, :, tm:tm + C]

        def project_gate_a():
            ga = _dot(h, win_ref[:, OFF_GA:OFF_GA + A_WIDTH])
            for m in range(A_TILES):
                g_s[w, :, m * LANES:(m + 1) * LANES] = _silu(tile(ga, m))

        def project_qb():
            qb = _dot(h, win_ref[:, OFF_QB:OFF_QB + B_QK_WIDTH])
            for hh in range(B_HEADS):
                qb_s[w, :, hh * LANES:(hh + 1) * LANES] = _rot_b(tile(qb, hh), cb, sb).astype(BF16)

        def project_kb():
            kb = _dot(h, win_ref[:, OFF_KB:OFF_KB + B_QK_WIDTH])
            for hh in range(B_HEADS):
                sl = slice(hh * LANES, (hh + 1) * LANES)
                kbh = _rot_b(tile(kb, hh), cb, sb) * (B_DK ** -0.5)
                kb_s[w, :, sl] = kbh.astype(BF16)
                kbd_s[w, :, sl] = (kbh.reshape(nchunk, C, LANES) * kdec_ref[hh][None]).reshape(tm, LANES).astype(BF16)

        def project_vb_gate_b():
            vb_s[w] = _dot(h, win_ref[:, OFF_VB:OFF_VB + B_V_WIDTH]).astype(BF16)
            gb = _dot(h, win_ref[:, OFF_GB:OFF_GB + B_V_WIDTH])
            for hh in range(B_HEADS):
                g_s[w, :, A_WIDTH + hh * LANES:A_WIDTH + (hh + 1) * LANES] = _silu(tile(gb, hh))

        project, finish = w is not None, r is not None
        if project:
            h = (_row_rms(x_ref[0]) * ng_ref[...]).astype(BF16)
        if finish:
            for c in range(nchunk):
                scores(c)
                retain_local(c)
        if project:
            project_attention()
        if finish:
            for c in range(nchunk):
                softmax(c)
            retain_states()
        if project:
            project_gate_a()
            project_qb()
        if finish:
            for c in range(nchunk):
                attend(c)
                retain_out(c)
        if project:
            project_kb()
            project_vb_gate_b()
        if finish:
            y_ref[0] = xres_ref[0] + _dot(mix_s[...], wout_ref[...])

    steady = jnp.logical_and(g > 0, g < last)

    @pl.when(g == 0)
    def _():
        body(0, None)

    @pl.when(jnp.logical_and(steady, g % 2 == 0))
    def _():
        body(0, 1)

    @pl.when(jnp.logical_and(steady, g % 2 == 1))
    def _():
        body(1, 0)

    @pl.when(g == last)
    def _():
        body(None, last_set)


def _const_spec(shape):
    nd = len(shape)
    return pl.BlockSpec(shape, lambda *_: (0,) * nd, pipeline_mode=pl.Buffered(1))


def _layer_spec(shape):
    nd = len(shape)
    return pl.BlockSpec((None,) + tuple(shape), lambda g, layer: (layer[0],) + (0,) * nd,
                        pipeline_mode=pl.Buffered(1))


def _prompt_layer(l, x, win, wout, ng, gq, gk, sinks, gam, rot_a, rot_b, dmat, qdec, kdec):
    N, T, D = x.shape
    tm = PROMPT_BLOCK
    assert T % tm == 0 and tm % RET_CHUNK == 0 and WINDOW == RET_CHUNK
    nt = T // tm
    nblk = N * nt
    nchunk = tm // RET_CHUNK

    def proj_blk(g):
        return jnp.minimum(g, nblk - 1)

    def fin_blk(g):
        return jnp.maximum(g - 1, 0)

    smem = pl.BlockSpec(memory_space=pltpu.SMEM)
    tab = pl.BlockSpec((tm, LANES), lambda g, layer: (proj_blk(g) % nt, 0))
    x_proj = pl.BlockSpec((1, tm, D), lambda g, layer: (proj_blk(g) // nt, proj_blk(g) % nt, 0))
    x_fin = pl.BlockSpec((1, tm, D), lambda g, layer: (fin_blk(g) // nt, fin_blk(g) % nt, 0))
    kv_out = pl.BlockSpec((1, WINDOW, LANES), lambda g, layer: (proj_blk(g) // nt, 0, 0))

    def two(*shape, dtype):
        return pltpu.VMEM((2,) + shape, dtype)

    grid_spec = pltpu.PrefetchScalarGridSpec(
        num_scalar_prefetch=1,
        grid=(nblk + 1,),
        in_specs=[smem, smem, x_proj, x_fin,
                  _layer_spec((D, IN_WIDTH)), _layer_spec((MIX_WIDTH, D)),
                  _layer_spec((1, D)), _layer_spec((1, LANES)), _layer_spec((1, LANES)),
                  tab, tab, tab, tab, tab,
                  _const_spec((B_HEADS, RET_CHUNK, RET_CHUNK)), _const_spec((B_HEADS, RET_CHUNK, LANES)),
                  _const_spec((B_HEADS, RET_CHUNK, LANES))],
        out_specs=[x_fin, kv_out, kv_out,
                   pl.BlockSpec((1, B_HEADS, B_DK, B_DV), lambda g, layer: (fin_blk(g) // nt, 0, 0, 0))],
        scratch_shapes=[two(tm, A_WIDTH, dtype=BF16),
                        two(2 * A_KV_HEADS, RET_CHUNK + tm, LANES, dtype=BF16),
                        two(A_KV_HEADS, LANES, RET_CHUNK + tm, dtype=BF16),
                        two(tm, B_QK_WIDTH, dtype=BF16),
                        two(tm, B_QK_WIDTH, dtype=BF16),
                        two(tm, B_QK_WIDTH, dtype=BF16),
                        two(tm, B_V_WIDTH, dtype=BF16),
                        two(tm, MIX_WIDTH, dtype=F32),
                        pltpu.VMEM((tm, MIX_WIDTH), BF16),
                        pltpu.VMEM((2 * RET_CHUNK, RET_CHUNK), F32),
                        pltpu.VMEM((nchunk * A_HEADS, 2 * RET_CHUNK, RET_CHUNK), F32),
                        pltpu.VMEM((nchunk * A_KV_HEADS, 2 * RET_CHUNK, A_GROUP * RET_CHUNK), BF16),
                        pltpu.VMEM((nchunk * A_KV_HEADS, 1, A_GROUP * RET_CHUNK), F32),
                        pltpu.VMEM((nchunk * B_HEADS, RET_CHUNK, RET_CHUNK), BF16),
                        pltpu.VMEM((nchunk * B_HEADS, B_DK, B_DV), F32),
                        pltpu.VMEM((nchunk * B_HEADS, B_DK, B_DV), BF16)])
    return pl.pallas_call(
        functools.partial(_prompt_kernel, tm=tm, nt=nt, last_set=(nblk - 1) % 2),
        grid_spec=grid_spec,
        out_shape=[jax.ShapeDtypeStruct((N, T, D), F32),
                   jax.ShapeDtypeStruct((N, WINDOW, LANES), F32),
                   jax.ShapeDtypeStruct((N, WINDOW, LANES), F32),
                   jax.ShapeDtypeStruct((N, B_HEADS, B_DK, B_DV), F32)],
        compiler_params=pltpu.CompilerParams(dimension_semantics=("arbitrary",),
                                             vmem_limit_bytes=VMEM_LIMIT),
        name="prompt_layer",
    )(jnp.full((1,), l, jnp.int32), sinks, gam, x, x, win, wout, ng, gq, gk, *rot_a, *rot_b, dmat, qdec, kdec)


def _sample_kernel(gam_ref, x_ref, win_ref, wout_ref, ng_ref, gq_ref, gk_ref,
                   ca_ref, sa1_ref, sa2_ref, cb_ref, sb_ref, sink_ref, d8_ref, qdec_ref, kdec_ref,
                   ck_ref, cv_ref, st_ref,
                   y_ref, ko_ref, vo_ref, so_ref,
                   xs_s, qz_s, kn_s, vn_s, knt_s, vnt_s, qb_s, kb_s, vb_s, g_s, mix_s, *, nseq, dec):
    layer = pl.program_id(0)
    step = pl.program_id(1)
    ntok = x_ref.shape[0]
    pair_rows = 2 * dec
    npair = nseq // 2

    @pl.when(jnp.logical_and(layer == 0, step == 0))
    def _():
        xs_s[...] = x_ref[...]

    @pl.when(step == 0)
    def _():
        h = (_row_rms(xs_s[...]) * ng_ref[...]).astype(BF16)
        lo_mask = lax.broadcasted_iota(jnp.int32, (ntok, LANES), 1) < A_HEAD_DIM
        ca, sa1, sa2 = ca_ref[...], sa1_ref[...], sa2_ref[...]
        cb, sb = cb_ref[...], sb_ref[...]

        def tile(a, i):
            return a[:, i * LANES:(i + 1) * LANES]

        qa = _dot(h, win_ref[:, OFF_QA:OFF_QA + A_WIDTH])
        for m in range(A_TILES):
            qr = _pair_norm_rot(tile(qa, m), gq_ref[...], ca, sa1, sa2, lo_mask)
            sw = pltpu.roll(qr, A_HEAD_DIM, 1)
            if m < A_TILES // 2:
                qz_s[2 * m] = jnp.where(lo_mask, qr, 0.0)
                qz_s[2 * m + 1] = jnp.where(lo_mask, sw, 0.0)
            else:
                qz_s[2 * m] = jnp.where(lo_mask, 0.0, sw)
                qz_s[2 * m + 1] = jnp.where(lo_mask, 0.0, qr)
        kv = _dot(h, win_ref[:, OFF_KA:OFF_KA + 2 * A_KV_WIDTH])
        kn = _pair_norm_rot(tile(kv, 0), gk_ref[...], ca, sa1, sa2, lo_mask)
        vn = tile(kv, 1)
        kn_s[...] = kn
        vn_s[...] = vn
        for tt in range(ntok // LANES):
            knt_s[tt] = kn[tt * LANES:(tt + 1) * LANES, :].T
            vnt_s[tt] = vn[tt * LANES:(tt + 1) * LANES, :].T
        ga = _dot(h, win_ref[:, OFF_GA:OFF_GA + A_WIDTH])
        for m in range(A_TILES):
            g_s[:, m * LANES:(m + 1) * LANES] = _silu(tile(ga, m))
        qb = _dot(h, win_ref[:, OFF_QB:OFF_QB + B_QK_WIDTH])
        for hh in range(B_HEADS):
            qb_s[:, hh * LANES:(hh + 1) * LANES] = _rot_b(tile(qb, hh), cb, sb)
        kb = _dot(h, win_ref[:, OFF_KB:OFF_KB + B_QK_WIDTH])
        for hh in range(B_HEADS):
            kb_s[:, hh * LANES:(hh + 1) * LANES] = _rot_b(tile(kb, hh), cb, sb) * (B_DK ** -0.5)
        vb_s[...] = _dot(h, win_ref[:, OFF_VB:OFF_VB + B_V_WIDTH])
        gb = _dot(h, win_ref[:, OFF_GB:OFF_GB + B_V_WIDTH])
        for hh in range(B_HEADS):
            g_s[:, A_WIDTH + hh * LANES:A_WIDTH + (hh + 1) * LANES] = _silu(tile(gb, hh))

    nq = A_HEADS * pair_rows
    row = lax.broadcasted_iota(jnp.int32, (nq, LANES), 0)
    slot = lax.broadcasted_iota(jnp.int32, (nq, LANES), 1)
    first_seq = (row % pair_rows) < dec
    cache_mask = slot > (row % dec)
    rown = lax.broadcasted_iota(jnp.int32, (nq, pair_rows), 0)
    coln = lax.broadcasted_iota(jnp.int32, (nq, pair_rows), 1)
    new_mask = ((coln // dec) == ((rown % pair_rows) // dec)) & ((coln % dec) <= (rown % dec))
    lo8 = lax.broadcasted_iota(jnp.int32, (pair_rows, LANES), 1) < A_HEAD_DIM
    r8 = lax.broadcasted_iota(jnp.int32, (pair_rows, LANES), 0)
    keep = lax.broadcasted_iota(jnp.int32, (LANES, WINDOW), 1) < WINDOW - dec
    sink = sink_ref[...]

    def pair_rows_of(p):
        seq0 = step * nseq + 2 * p
        return pl.ds(pl.multiple_of(seq0 * dec, pair_rows), pair_rows)


    stage1 = []
    for p in range(npair):
        b0 = 2 * p
        rows = pair_rows_of(p)
        qz = qz_s[:, rows, :].reshape(nq, LANES).astype(BF16)
        s_c = jnp.where(first_seq, _dot(qz, ck_ref[b0].astype(BF16)), _dot(qz, ck_ref[b0 + 1].astype(BF16)))
        s_n = _dot_nt(qz, kn_s[rows, :].astype(BF16))
        rets = []
        for hh in range(B_HEADS):
            sl = slice(hh * LANES, (hh + 1) * LANES)
            q = qb_s[rows, sl].astype(BF16)
            kf = kb_s[rows, sl]
            kd = kf * kdec_ref[hh]
            v = vb_s[rows, sl].astype(BF16)
            sc = (_dot_nt(q, kf.astype(BF16)) * d8_ref[hh]).astype(BF16)
            crosses = []
            for j in range(2):
                S = st_ref[b0 + j, hh]
                crosses.append(_dot(q, S.astype(BF16)))
                kdj = jnp.where((r8 // dec) == j, kd, 0.0).astype(BF16)
                so_ref[b0 + j, hh] = gam_ref[hh] * S + _dot_tn(kdj, v)
            cross = jnp.where(r8 < dec, crosses[0], crosses[1]) * qdec_ref[hh]
            rets.append((sc, v, cross))
        stage1.append((s_c, s_n, rets))

    probs = []
    for s_c, s_n, _ in stage1:
        p_c, p_n = _softmax_parts([jnp.where(cache_mask, s_c, NEG), jnp.where(new_mask, s_n, NEG)], sink)
        probs.append((p_c.astype(BF16), p_n.astype(BF16)))

    for p in range(npair):
        b0 = 2 * p
        rows = pair_rows_of(p)
        p_c, p_n = probs[p]
        o = jnp.where(first_seq, _dot_nt(p_c, cv_ref[b0].astype(BF16)), _dot_nt(p_c, cv_ref[b0 + 1].astype(BF16)))
        o = o + _dot(p_n, vn_s[rows, :].astype(BF16))
        for m in range(A_TILES):
            oa = o[(2 * m) * pair_rows:(2 * m + 1) * pair_rows, :]
            ob = o[(2 * m + 1) * pair_rows:(2 * m + 2) * pair_rows, :]
            if m < A_TILES // 2:
                blk = jnp.where(lo8, oa, pltpu.roll(ob, A_HEAD_DIM, 1))
            else:
                blk = jnp.where(lo8, pltpu.roll(oa, A_HEAD_DIM, 1), ob)
            sl = slice(m * LANES, (m + 1) * LANES)
            mix_s[rows, sl] = g_s[rows, sl] * blk
        for hh, (sc, v, cross) in enumerate(stage1[p][2]):
            osl = slice(A_WIDTH + hh * LANES, A_WIDTH + (hh + 1) * LANES)
            mix_s[rows, osl] = g_s[rows, osl] * _row_rms(_dot(sc, v) + cross)

    for b in range(nseq):
        tok0 = (step * nseq + b) * dec
        tt = tok0 // LANES
        shift = (WINDOW - dec) - tok0 % LANES
        ko_ref[b] = jnp.where(keep, pltpu.roll(ck_ref[b], WINDOW - dec, 1), pltpu.roll(knt_s[tt], shift, 1))
        vo_ref[b] = jnp.where(keep, pltpu.roll(cv_ref[b], WINDOW - dec, 1), pltpu.roll(vnt_s[tt], shift, 1))

    @pl.when(step == pl.num_programs(1) - 1)
    def _():
        xs_s[...] = xs_s[...] + _dot(mix_s[...].astype(BF16), wout_ref[...])

    @pl.when(jnp.logical_and(layer == pl.num_programs(0) - 1, step == pl.num_programs(1) - 1))
    def _():
        y_ref[...] = xs_s[...]


def _stacked_spec(shape):
    nd = len(shape)
    return pl.BlockSpec((None,) + tuple(shape), lambda l, i: (l,) + (0,) * nd)


def _sample_layers(x, win, wout, ng, gq, gk, sink_col, gam, rot_a, rot_b, d8, qdec, kdec, ck, cv, st, dec):
    ntok, D = x.shape
    depth, nb = ck.shape[:2]
    nseq = SAMPLE_SEQS
    assert nb % nseq == 0 and nseq % 2 == 0 and ntok == nb * dec and 2 * dec == 8 and ntok % LANES == 0
    assert LANES % dec == 0 and WINDOW == LANES
    pair_rows = 2 * dec
    nq = A_HEADS * pair_rows
    smem = pl.BlockSpec(memory_space=pltpu.SMEM)
    kv = pl.BlockSpec((None, nseq, A_KV_WIDTH, WINDOW), lambda l, i: (l, i, 0, 0))
    stt = pl.BlockSpec((None, nseq, B_HEADS, B_DK, B_DV), lambda l, i: (l, i, 0, 0, 0))
    tab = _const_spec((ntok, LANES))
    return pl.pallas_call(
        functools.partial(_sample_kernel, nseq=nseq, dec=dec),
        grid=(depth, nb // nseq),
        in_specs=[smem, _const_spec((ntok, D)), _stacked_spec((D, IN_WIDTH)), _stacked_spec((MIX_WIDTH, D)),
                  _stacked_spec((1, D)), _stacked_spec((1, LANES)), _stacked_spec((1, LANES)),
                  tab, tab, tab, tab, tab,
                  _stacked_spec((nq, 1)), _const_spec((B_HEADS, pair_rows, pair_rows)),
                  _const_spec((B_HEADS, pair_rows, LANES)), _const_spec((B_HEADS, pair_rows, LANES)),
                  kv, kv, stt],
        out_specs=[pl.BlockSpec((ntok, D), lambda l, i: (0, 0)), kv, kv, stt],
        out_shape=[jax.ShapeDtypeStruct((ntok, D), F32),
                   jax.ShapeDtypeStruct(ck.shape, F32),
                   jax.ShapeDtypeStruct(cv.shape, F32),
                   jax.ShapeDtypeStruct(st.shape, F32)],
        scratch_shapes=[pltpu.VMEM((ntok, D), F32),
                        pltpu.VMEM((A_HEADS, ntok, LANES), F32),
                        pltpu.VMEM((ntok, LANES), F32),
                        pltpu.VMEM((ntok, LANES), F32),
                        pltpu.VMEM((ntok // LANES, A_KV_WIDTH, LANES), F32),
                        pltpu.VMEM((ntok // LANES, A_KV_WIDTH, LANES), F32),
                        pltpu.VMEM((ntok, B_QK_WIDTH), F32),
                        pltpu.VMEM((ntok, B_QK_WIDTH), F32),
                        pltpu.VMEM((ntok, B_V_WIDTH), F32),
                        pltpu.VMEM((ntok, MIX_WIDTH), F32),
                        pltpu.VMEM((ntok, MIX_WIDTH), F32)],
        compiler_params=pltpu.CompilerParams(dimension_semantics=("arbitrary", "arbitrary"),
                                             vmem_limit_bytes=VMEM_LIMIT),
        name="sample_layers",
    )(gam, x, win, wout, ng, gq, gk, *rot_a, *rot_b, sink_col, d8, qdec, kdec, ck, cv, st)


def _cos_sin(pos_start, n_pos, freq):
    if n_pos % LANES or pos_start:
        ang = (pos_start + jnp.arange(n_pos, dtype=jnp.int32)).astype(F32)[:, None] * freq[None, :]
        return jnp.cos(ang), jnp.sin(ang)
    a_hi = (jnp.arange(n_pos // LANES, dtype=jnp.int32) * LANES).astype(F32)[:, None] * freq[None, :]
    a_lo = jnp.arange(LANES, dtype=jnp.int32).astype(F32)[:, None] * freq[None, :]
    ch, sh = jnp.cos(a_hi)[:, None, :], jnp.sin(a_hi)[:, None, :]
    cl, sl = jnp.cos(a_lo)[None, :, :], jnp.sin(a_lo)[None, :, :]
    return (ch * cl - sh * sl).reshape(n_pos, LANES), (sh * cl + ch * sl).reshape(n_pos, LANES)


def _rot_tables_a(pos_start, n_pos):
    half = A_ROT_DIM // 2
    inv_freq = ROPE_THETA_A ** (-jnp.arange(half, dtype=F32) / half)
    d = jnp.arange(LANES, dtype=jnp.int32) % A_HEAD_DIM
    freq = jnp.where(d < A_ROT_DIM, inv_freq[d % half], 0.0)
    cos, sin = _cos_sin(pos_start, n_pos, freq)
    s1 = jnp.where(d < half, -sin, 0.0)
    s2 = jnp.where((d >= half) & (d < A_ROT_DIM), sin, 0.0)
    return cos, s1, s2


def _rot_tables_b(pos_start, n_pos):
    half = B_DK // 2
    inv_freq = ROPE_THETA_B ** (-jnp.arange(half, dtype=F32) / half)
    cos, sin = _cos_sin(pos_start, n_pos, jnp.tile(inv_freq, 2))
    return cos, jnp.where(jnp.arange(LANES) < half, -sin, sin)


def _log_decay():
    return jnp.log(1.0 - 2.0 ** (-5.0 - jnp.arange(B_HEADS, dtype=F32)))


def _decay_tables(C):
    lg = _log_decay()
    idx = jnp.arange(C, dtype=F32)
    diff = idx[:, None] - idx[None, :]
    dmat = jnp.where(diff >= 0, jnp.exp(lg[:, None, None] * jnp.maximum(diff, 0.0)), 0.0)
    q_dec = jnp.exp(lg[None, :] * (idx[:, None] + 1.0))
    k_dec = jnp.exp(lg[None, :] * (C - 1.0 - idx[:, None]))
    gam = jnp.exp(lg * C)
    return dmat, q_dec, k_dec, gam


def kernel(x_prompt, x_sample, cache_swa_k, cache_swa_v, state_ret, w_in, w_out, norm_g, q_norm_g, k_norm_g, sinks):
    N, T, D = x_prompt.shape
    nb, dec, _ = x_sample.shape
    w_buf = cache_swa_k.shape[2]
    assert w_buf == WINDOW

    rot_a_p, rot_b_p = _rot_tables_a(0, T), _rot_tables_b(0, T)
    rot_a_s = tuple(jnp.tile(a, (nb, 1)) for a in _rot_tables_a(PAST_LEN, dec))
    rot_b_s = tuple(jnp.tile(a, (nb, 1)) for a in _rot_tables_b(PAST_LEN, dec))

    dmat, q_dec, k_dec, gam_p = _decay_tables(RET_CHUNK)
    qdec_p = jnp.broadcast_to(q_dec.T[:, :, None], (B_HEADS, RET_CHUNK, LANES))
    kdec_p = jnp.broadcast_to(k_dec.T[:, :, None], (B_HEADS, RET_CHUNK, LANES))
    dmat4, q_dec4, k_dec4, gam_s = _decay_tables(dec)
    d8 = jnp.kron(jnp.eye(2, dtype=F32)[None], jnp.ones((1, dec, dec), F32)) * jnp.tile(dmat4, (1, 2, 2))
    qdec_s = jnp.broadcast_to(jnp.tile(q_dec4.T, (1, 2))[:, :, None], (B_HEADS, 2 * dec, LANES))
    kdec_s = jnp.broadcast_to(jnp.tile(k_dec4.T, (1, 2))[:, :, None], (B_HEADS, 2 * dec, LANES))

    w_in_b = w_in.astype(BF16)
    w_out_b = w_out.astype(BF16)
    ck = cache_swa_k.transpose(0, 1, 3, 4, 2).reshape(DEPTH, nb, A_KV_WIDTH, w_buf)
    cv = cache_swa_v.transpose(0, 1, 3, 4, 2).reshape(DEPTH, nb, A_KV_WIDTH, w_buf)

    ng = norm_g[:, None, :]
    gq = jnp.tile(q_norm_g, (1, HEADS_PER_TILE))[:, None, :] * (A_HEAD_DIM ** -0.5 * LOG2E)
    gk = jnp.tile(k_norm_g, (1, HEADS_PER_TILE))[:, None, :]
    sinks2 = sinks * LOG2E

    sink_col = jnp.repeat(sinks2, 2 * dec, axis=1)[:, :, None]
    xs, ks, vs, ss = _sample_layers(x_sample.reshape(nb * dec, D), w_in_b, w_out_b, ng, gq, gk, sink_col, gam_s,
                                    rot_a_s, rot_b_s, d8, qdec_s, kdec_s, ck, cv, state_ret, dec)

    xp = x_prompt
    kp_l, vp_l, sp_l = [], [], []
    for l in range(DEPTH):
        xp, kl, vl, S = _prompt_layer(l, xp, w_in_b, w_out_b, ng, gq, gk, sinks2[l], gam_p,
                                      rot_a_p, rot_b_p, dmat, qdec_p, kdec_p)
        kp_l.append(kl)
        vp_l.append(vl)
        sp_l.append(S)

    kv_p = (DEPTH, N, WINDOW, A_KV_HEADS, A_HEAD_DIM)
    kv_t = (DEPTH, nb, A_KV_HEADS, A_HEAD_DIM, w_buf)
    return (xp, xs.reshape(nb, dec, D),
            jnp.stack(kp_l).reshape(kv_p), jnp.stack(vp_l).reshape(kv_p), jnp.stack(sp_l),
            ks.reshape(kv_t).transpose(0, 1, 4, 2, 3), vs.reshape(kv_t).transpose(0, 1, 4, 2, 3), ss)
```

```python
import functools

import jax
import jax.numpy as jnp
from jax import lax
from jax.experimental import pallas as pl
from jax.experimental.pallas import tpu as pltpu

D_MODEL = 1024
DEPTH = 4
PAST_LEN = 8192
A_HEADS = 8
A_KV_HEADS = 2
A_GROUP = A_HEADS // A_KV_HEADS
A_HEAD_DIM = 64
WINDOW = 128
A_ROT_DIM = A_HEAD_DIM // 4
ROPE_THETA_A = 500000.0
B_HEADS = 4
B_DK = 128
B_DV = 128
RET_CHUNK = 128
ROPE_THETA_B = 10000.0
EPS = 1e-6

A_WIDTH = A_HEADS * A_HEAD_DIM
A_KV_WIDTH = A_KV_HEADS * A_HEAD_DIM
B_QK_WIDTH = B_HEADS * B_DK
B_V_WIDTH = B_HEADS * B_DV
MIX_WIDTH = A_WIDTH + B_V_WIDTH
OFF_QA = 0
OFF_KA = OFF_QA + A_WIDTH
OFF_VA = OFF_KA + A_KV_WIDTH
OFF_GA = OFF_VA + A_KV_WIDTH
OFF_QB = OFF_GA + A_WIDTH
OFF_KB = OFF_QB + B_QK_WIDTH
OFF_VB = OFF_KB + B_QK_WIDTH
OFF_GB = OFF_VB + B_V_WIDTH
IN_WIDTH = OFF_GB + B_V_WIDTH

LANES = 128
HEADS_PER_TILE = LANES // A_HEAD_DIM
A_TILES = A_WIDTH // LANES
NEG = -1e30
LOG2E = 1.4426950408889634

PROMPT_BLOCK = 512
SAMPLE_SEQS = 8
VMEM_LIMIT = 56 * 1024 * 1024

F32 = jnp.float32
BF16 = jnp.bfloat16


def _dot(a, b):
    return jnp.dot(a, b, preferred_element_type=F32)


def _dot_nt(a, b):
    return lax.dot_general(a, b, (((1,), (1,)), ((), ())), preferred_element_type=F32)


def _dot_tn(a, b):
    return lax.dot_general(a, b, (((0,), (0,)), ((), ())), preferred_element_type=F32)


def _silu(x):
    return x * (1.0 / (1.0 + jnp.exp(-x)))


def _row_rms(x):
    n = x.shape[-1]
    return x * lax.rsqrt(jnp.sum(x * x, axis=-1, keepdims=True) * (1.0 / n) + EPS)


def _pair_norm_rot(blk, g_row, ca, sa1, sa2, lo_mask):
    sq = blk * blk
    tot = jnp.sum(sq, axis=-1, keepdims=True)
    lo = jnp.sum(jnp.where(lo_mask, sq, 0.0), axis=-1, keepdims=True)
    hi = tot - lo
    inv_lo = lax.rsqrt(lo * (1.0 / A_HEAD_DIM) + EPS)
    inv_hi = lax.rsqrt(hi * (1.0 / A_HEAD_DIM) + EPS)
    xn = blk * jnp.where(lo_mask, inv_lo, inv_hi) * g_row
    half = A_ROT_DIM // 2
    return xn * ca + pltpu.roll(xn, LANES - half, 1) * sa1 + pltpu.roll(xn, half, 1) * sa2


def _rot_b(blk, cb, sb):
    return blk * cb + pltpu.roll(blk, B_DK // 2, 1) * sb


def _softmax_parts(parts, sink):
    mx = sink
    for s in parts:
        mx = jnp.maximum(mx, jnp.max(s, axis=-1, keepdims=True))
    ps = [jnp.exp2(s - mx) for s in parts]
    den = jnp.exp2(sink - mx)
    for p in ps:
        den = den + jnp.sum(p, axis=-1, keepdims=True)
    rinv = 1.0 / den
    return [p * rinv for p in ps]


def _prompt_kernel(sinks_ref, gam_ref, x_ref, xres_ref, win_ref, wout_ref, ng_ref, gq_ref, gk_ref,
                   ca_ref, sa1_ref, sa2_ref, cb_ref, sb_ref, dmat_ref, qdec_ref, kdec_ref,
                   y_ref, klast_ref, vlast_ref, S_ref,
                   q_s, kz_s, vt_s, qb_s, kb_s, kbd_s, vb_s, g_s, mix_s, bias_s, s_s, p_s, es_s,
                   sc_s, kv_s, sb_s, *, tm, nt):
    g = pl.program_id(0)
    nchunk = tm // RET_CHUNK
    C = RET_CHUNK
    t_fin = jnp.maximum(g - 1, 0) % nt

    @pl.when(g == 0)
    def _():
        kj = lax.broadcasted_iota(jnp.int32, (2 * C, C), 0)
        qi = lax.broadcasted_iota(jnp.int32, (2 * C, C), 1)
        bias_s[...] = jnp.where((kj > qi) & (kj <= qi + C), 0.0, NEG)
        for ref in (q_s, kz_s, vt_s, qb_s, kb_s, kbd_s, vb_s, g_s):
            ref[1] = jnp.zeros(ref.shape[1:], ref.dtype)

    @pl.when(t_fin == 0)
    def _():
        S_ref[...] = jnp.zeros_like(S_ref)

    def tile(a, i):
        return a[:, i * LANES:(i + 1) * LANES]

    def body(w, r):
        lo_mask = lax.broadcasted_iota(jnp.int32, (tm, LANES), 1) < A_HEAD_DIM
        ca, sa1, sa2 = ca_ref[...], sa1_ref[...], sa2_ref[...]
        cb, sb = cb_ref[...], sb_ref[...]
        first_bias = jnp.where(t_fin == 0, NEG, 0.0)

        def chunk_rows(c):
            return slice(c * C, (c + 1) * C)

        def window_rows(c):
            return slice(c * C, (c + 2) * C)

        def group_heads(grp):
            return range(grp * A_GROUP, (grp + 1) * A_GROUP)

        def window_cols(c):
            return slice(c * C, (c + 2) * C)

        def scores(c):
            for grp in range(A_KV_HEADS):
                tiles = range(grp * A_GROUP // HEADS_PER_TILE, (grp + 1) * A_GROUP // HEADS_PER_TILE)
                qp = jnp.concatenate([q_s[r, chunk_rows(c), m * LANES:(m + 1) * LANES] for m in tiles], axis=0)
                for half in range(HEADS_PER_TILE):
                    st = _dot_nt(kz_s[r, 2 * grp + half, window_rows(c), :], qp)
                    for i, m in enumerate(tiles):
                        s = st[:, i * C:(i + 1) * C] + bias_s[...]
                        if c == 0:
                            s = jnp.concatenate([s[:C, :] + first_bias, s[C:, :]], axis=0)
                        s_s[c * A_HEADS + m * HEADS_PER_TILE + half] = s

        def softmax(c):
            for grp in range(A_KV_HEADS):
                for i, hd in enumerate(group_heads(grp)):
                    s = s_s[c * A_HEADS + hd]
                    sink = sinks_ref[hd]
                    mx = jnp.maximum(jnp.max(s, axis=0, keepdims=True), sink)
                    p_s[c * A_KV_HEADS + grp, :, i * C:(i + 1) * C] = jnp.exp2(s - mx).astype(BF16)
                    es_s[c * A_KV_HEADS + grp, :, i * C:(i + 1) * C] = jnp.exp2(sink - mx)

        def attend(c):
            for grp in range(A_KV_HEADS):
                ot = _dot(vt_s[r, grp, :, window_cols(c)], p_s[c * A_KV_HEADS + grp])
                den = ot[A_HEAD_DIM:A_HEAD_DIM + 1, :] + es_s[c * A_KV_HEADS + grp]
                o = ot[0:A_HEAD_DIM, :] * (1.0 / den)
                for i in range(A_GROUP // HEADS_PER_TILE):
                    m = grp * A_GROUP // HEADS_PER_TILE + i
                    sl = slice(m * LANES, (m + 1) * LANES)
                    pair = jnp.concatenate([o[:, (HEADS_PER_TILE * i + j) * C:(HEADS_PER_TILE * i + j + 1) * C]
                                            for j in range(HEADS_PER_TILE)], axis=0)
                    mix_s[chunk_rows(c), sl] = (g_s[r, chunk_rows(c), sl] * pair.T).astype(BF16)

        def retain_local(c):
            for hh in range(B_HEADS):
                sl = slice(hh * LANES, (hh + 1) * LANES)
                rows = chunk_rows(c)
                sc_s[c * B_HEADS + hh] = (_dot_nt(qb_s[r, rows, sl], kb_s[r, rows, sl]) * dmat_ref[hh]).astype(BF16)
                kv_s[c * B_HEADS + hh] = _dot_tn(kbd_s[r, rows, sl], vb_s[r, rows, sl])

        def retain_states():
            for hh in range(B_HEADS):
                S = S_ref[0, hh]
                for c in range(nchunk):
                    sb_s[c * B_HEADS + hh] = S.astype(BF16)
                    S = gam_ref[hh] * S + kv_s[c * B_HEADS + hh]
                S_ref[0, hh] = S

        def retain_out(c):
            for hh in range(B_HEADS):
                sl = slice(hh * LANES, (hh + 1) * LANES)
                rows = chunk_rows(c)
                i = c * B_HEADS + hh
                ret = _dot(sc_s[i], vb_s[r, rows, sl]) + _dot(qb_s[r, rows, sl], sb_s[i]) * qdec_ref[hh]
                osl = slice(A_WIDTH + hh * LANES, A_WIDTH + (hh + 1) * LANES)
                mix_s[rows, osl] = (g_s[r, rows, osl] * _row_rms(ret)).astype(BF16)

        def project_attention():
            qa = _dot(h, win_ref[:, OFF_QA:OFF_QA + A_WIDTH])
            for m in range(A_TILES):
                q_s[w, :, m * LANES:(m + 1) * LANES] = _pair_norm_rot(
                    tile(qa, m), gq_ref[...], ca, sa1, sa2, lo_mask).astype(BF16)
            kv = _dot(h, win_ref[:, OFF_KA:OFF_KA + 2 * A_KV_WIDTH])
            kr = _pair_norm_rot(tile(kv, 0), gk_ref[...], ca, sa1, sa2, lo_mask)
            va = tile(kv, 1)
            klast_ref[0] = kr[tm - WINDOW:, :]
            vlast_ref[0] = va[tm - WINDOW:, :]
            sw = pltpu.roll(kr, A_HEAD_DIM, 1)
            kz_s[w, 0, C:, :] = jnp.where(lo_mask, kr, 0.0).astype(BF16)
            kz_s[w, 1, C:, :] = jnp.where(lo_mask, 0.0, sw).astype(BF16)
            kz_s[w, 2, C:, :] = jnp.where(lo_mask, sw, 0.0).astype(BF16)
            kz_s[w, 3, C:, :] = jnp.where(lo_mask, 0.0, kr).astype(BF16)
            vat = va.T
            for grp in range(A_KV_HEADS):
                vt_s[w, grp, 0:A_HEAD_DIM, C:] = vat[grp * A_HEAD_DIM:(grp + 1) * A_HEAD_DIM, :].astype(BF16)
                vt_s[w, grp, A_HEAD_DIM:, C:] = jnp.ones((LANES - A_HEAD_DIM, tm), BF16)
            for i in range(2 * A_KV_HEADS):
                kz_s[w, i, 0:C, :] = kz_s[r, i, tm:tm + C, :]
            for grp in range(A_KV_HEADS):
                vt_s[w, grp, :, 0:C] = vt_s[r, grp, :, tm:tm + C]

        def project_gate_a():
            ga = _dot(h, win_ref[:, OFF_GA:OFF_GA + A_WIDTH])
            for m in range(A_TILES):
                g_s[w, :, m * LANES:(m + 1) * LANES] = _silu(tile(ga, m))

        def project_qb():
            qb = _dot(h, win_ref[:, OFF_QB:OFF_QB + B_QK_WIDTH])
            for hh in range(B_HEADS):
                qb_s[w, :, hh * LANES:(hh + 1) * LANES] = _rot_b(tile(qb, hh), cb, sb).astype(BF16)

        def project_kb():
            kb = _dot(h, win_ref[:, OFF_KB:OFF_KB + B_QK_WIDTH])
            for hh in range(B_HEADS):
                sl = slice(hh * LANES, (hh + 1) * LANES)
                kbh = _rot_b(tile(kb, hh), cb, sb) * (B_DK ** -0.5)
                kb_s[w, :, sl] = kbh.astype(BF16)
                kbd_s[w, :, sl] = (kbh.reshape(nchunk, C, LANES) * kdec_ref[hh][None]).reshape(tm, LANES).astype(BF16)

        def project_vb_gate_b():
            vb_s[w] = _dot(h, win_ref[:, OFF_VB:OFF_VB + B_V_WIDTH]).astype(BF16)
            gb = _dot(h, win_ref[:, OFF_GB:OFF_GB + B_V_WIDTH])
            for hh in range(B_HEADS):
                g_s[w, :, A_WIDTH + hh * LANES:A_WIDTH + (hh + 1) * LANES] = _silu(tile(gb, hh))

        h = (_row_rms(x_ref[0]) * ng_ref[...]).astype(BF16)
        for c in range(nchunk):
            scores(c)
            retain_local(c)
        project_attention()
        for c in range(nchunk):
            softmax(c)
        retain_states()
        project_gate_a()
        project_qb()
        for c in range(nchunk):
            attend(c)
            retain_out(c)
        project_kb()
        project_vb_gate_b()
        y_ref[0] = xres_ref[0] + _dot(mix_s[...], wout_ref[...])

    @pl.when(g % 2 == 0)
    def _():
        body(0, 1)

    @pl.when(g % 2 == 1)
    def _():
        body(1, 0)


def _const_spec(shape):
    nd = len(shape)
    return pl.BlockSpec(shape, lambda *_: (0,) * nd, pipeline_mode=pl.Buffered(1))


def _layer_spec(shape, l):
    nd = len(shape)
    return pl.BlockSpec((None,) + tuple(shape), lambda *_: (l,) + (0,) * nd, pipeline_mode=pl.Buffered(1))


def _prompt_layer(l, x, win, wout, ng, gq, gk, sinks, gam, rot_a, rot_b, dmat, qdec, kdec):
    N, T, D = x.shape
    tm = PROMPT_BLOCK
    assert T % tm == 0 and tm % RET_CHUNK == 0 and WINDOW == RET_CHUNK
    nt = T // tm
    nblk = N * nt
    nchunk = tm // RET_CHUNK

    def proj_blk(g):
        return jnp.minimum(g, nblk - 1)

    def fin_blk(g):
        return jnp.maximum(g - 1, 0)

    smem = pl.BlockSpec(memory_space=pltpu.SMEM)
    tab = pl.BlockSpec((tm, LANES), lambda g: (proj_blk(g) % nt, 0))
    x_proj = pl.BlockSpec((1, tm, D), lambda g: (proj_blk(g) // nt, proj_blk(g) % nt, 0))
    x_fin = pl.BlockSpec((1, tm, D), lambda g: (fin_blk(g) // nt, fin_blk(g) % nt, 0))
    kv_out = pl.BlockSpec((1, WINDOW, LANES), lambda g: (proj_blk(g) // nt, 0, 0))

    def two(*shape, dtype):
        return pltpu.VMEM((2,) + shape, dtype)

    return pl.pallas_call(
        functools.partial(_prompt_kernel, tm=tm, nt=nt),
        grid=(nblk + 1,),
        in_specs=[smem, smem, x_proj, x_fin,
                  _layer_spec((D, IN_WIDTH), l), _layer_spec((MIX_WIDTH, D), l),
                  _layer_spec((1, D), l), _layer_spec((1, LANES), l), _layer_spec((1, LANES), l),
                  tab, tab, tab, tab, tab,
                  _const_spec((B_HEADS, RET_CHUNK, RET_CHUNK)), _const_spec((B_HEADS, RET_CHUNK, LANES)),
                  _const_spec((B_HEADS, RET_CHUNK, LANES))],
        out_specs=[x_fin, kv_out, kv_out,
                   pl.BlockSpec((1, B_HEADS, B_DK, B_DV), lambda g: (fin_blk(g) // nt, 0, 0, 0))],
        out_shape=[jax.ShapeDtypeStruct((N, T, D), F32),
                   jax.ShapeDtypeStruct((N, WINDOW, LANES), F32),
                   jax.ShapeDtypeStruct((N, WINDOW, LANES), F32),
                   jax.ShapeDtypeStruct((N, B_HEADS, B_DK, B_DV), F32)],
        scratch_shapes=[two(tm, A_WIDTH, dtype=BF16),
                        two(2 * A_KV_HEADS, RET_CHUNK + tm, LANES, dtype=BF16),
                        two(A_KV_HEADS, LANES, RET_CHUNK + tm, dtype=BF16),
                        two(tm, B_QK_WIDTH, dtype=BF16),
                        two(tm, B_QK_WIDTH, dtype=BF16),
                        two(tm, B_QK_WIDTH, dtype=BF16),
                        two(tm, B_V_WIDTH, dtype=BF16),
                        two(tm, MIX_WIDTH, dtype=F32),
                        pltpu.VMEM((tm, MIX_WIDTH), BF16),
                        pltpu.VMEM((2 * RET_CHUNK, RET_CHUNK), F32),
                        pltpu.VMEM((nchunk * A_HEADS, 2 * RET_CHUNK, RET_CHUNK), F32),
                        pltpu.VMEM((nchunk * A_KV_HEADS, 2 * RET_CHUNK, A_GROUP * RET_CHUNK), BF16),
                        pltpu.VMEM((nchunk * A_KV_HEADS, 1, A_GROUP * RET_CHUNK), F32),
                        pltpu.VMEM((nchunk * B_HEADS, RET_CHUNK, RET_CHUNK), BF16),
                        pltpu.VMEM((nchunk * B_HEADS, B_DK, B_DV), F32),
                        pltpu.VMEM((nchunk * B_HEADS, B_DK, B_DV), BF16)],
        compiler_params=pltpu.CompilerParams(dimension_semantics=("arbitrary",),
                                             vmem_limit_bytes=VMEM_LIMIT),
        name="prompt_layer",
    )(sinks, gam, x, x, win, wout, ng, gq, gk, *rot_a, *rot_b, dmat, qdec, kdec)


def _sample_kernel(gam_ref, x_ref, win_ref, wout_ref, ng_ref, gq_ref, gk_ref,
                   ca_ref, sa1_ref, sa2_ref, cb_ref, sb_ref, sink_ref, d8_ref, qdec_ref, kdec_ref,
                   ck_ref, cv_ref, st_ref,
                   y_ref, ko_ref, vo_ref, so_ref,
                   xs_s, qz_s, kn_s, vn_s, knt_s, vnt_s, qb_s, kb_s, vb_s, g_s, mix_s, *, nseq, dec):
    layer = pl.program_id(0)
    step = pl.program_id(1)
    ntok = x_ref.shape[0]
    pair_rows = 2 * dec
    npair = nseq // 2

    @pl.when(jnp.logical_and(layer == 0, step == 0))
    def _():
        xs_s[...] = x_ref[...]

    @pl.when(step == 0)
    def _():
        h = (_row_rms(xs_s[...]) * ng_ref[...]).astype(BF16)
        lo_mask = lax.broadcasted_iota(jnp.int32, (ntok, LANES), 1) < A_HEAD_DIM
        ca, sa1, sa2 = ca_ref[...], sa1_ref[...], sa2_ref[...]
        cb, sb = cb_ref[...], sb_ref[...]

        def tile(a, i):
            return a[:, i * LANES:(i + 1) * LANES]

        qa = _dot(h, win_ref[:, OFF_QA:OFF_QA + A_WIDTH])
        for m in range(A_TILES):
            qr = _pair_norm_rot(tile(qa, m), gq_ref[...], ca, sa1, sa2, lo_mask)
            sw = pltpu.roll(qr, A_HEAD_DIM, 1)
            if m < A_TILES // 2:
                qz_s[2 * m] = jnp.where(lo_mask, qr, 0.0)
                qz_s[2 * m + 1] = jnp.where(lo_mask, sw, 0.0)
            else:
                qz_s[2 * m] = jnp.where(lo_mask, 0.0, sw)
                qz_s[2 * m + 1] = jnp.where(lo_mask, 0.0, qr)
        kv = _dot(h, win_ref[:, OFF_KA:OFF_KA + 2 * A_KV_WIDTH])
        kn = _pair_norm_rot(tile(kv, 0), gk_ref[...], ca, sa1, sa2, lo_mask)
        vn = tile(kv, 1)
        kn_s[...] = kn
        vn_s[...] = vn
        for tt in range(ntok // LANES):
            knt_s[tt] = kn[tt * LANES:(tt + 1) * LANES, :].T
            vnt_s[tt] = vn[tt * LANES:(tt + 1) * LANES, :].T
        ga = _dot(h, win_ref[:, OFF_GA:OFF_GA + A_WIDTH])
        for m in range(A_TILES):
            g_s[:, m * LANES:(m + 1) * LANES] = _silu(tile(ga, m))
        qb = _dot(h, win_ref[:, OFF_QB:OFF_QB + B_QK_WIDTH])
        for hh in range(B_HEADS):
            qb_s[:, hh * LANES:(hh + 1) * LANES] = _rot_b(tile(qb, hh), cb, sb)
        kb = _dot(h, win_ref[:, OFF_KB:OFF_KB + B_QK_WIDTH])
        for hh in range(B_HEADS):
            kb_s[:, hh * LANES:(hh + 1) * LANES] = _rot_b(tile(kb, hh), cb, sb) * (B_DK ** -0.5)
        vb_s[...] = _dot(h, win_ref[:, OFF_VB:OFF_VB + B_V_WIDTH])
        gb = _dot(h, win_ref[:, OFF_GB:OFF_GB + B_V_WIDTH])
        for hh in range(B_HEADS):
            g_s[:, A_WIDTH + hh * LANES:A_WIDTH + (hh + 1) * LANES] = _silu(tile(gb, hh))

    nq = A_HEADS * pair_rows
    row = lax.broadcasted_iota(jnp.int32, (nq, LANES), 0)
    slot = lax.broadcasted_iota(jnp.int32, (nq, LANES), 1)
    first_seq = (row % pair_rows) < dec
    cache_mask = slot > (row % dec)
    rown = lax.broadcasted_iota(jnp.int32, (nq, pair_rows), 0)
    coln = lax.broadcasted_iota(jnp.int32, (nq, pair_rows), 1)
    new_mask = ((coln // dec) == ((rown % pair_rows) // dec)) & ((coln % dec) <= (rown % dec))
    lo8 = lax.broadcasted_iota(jnp.int32, (pair_rows, LANES), 1) < A_HEAD_DIM
    r8 = lax.broadcasted_iota(jnp.int32, (pair_rows, LANES), 0)
    keep = lax.broadcasted_iota(jnp.int32, (LANES, WINDOW), 1) < WINDOW - dec
    sink = sink_ref[...]

    def pair_rows_of(p):
        seq0 = step * nseq + 2 * p
        return pl.ds(pl.multiple_of(seq0 * dec, pair_rows), pair_rows)


    stage1 = []
    for p in range(npair):
        b0 = 2 * p
        rows = pair_rows_of(p)
        qz = qz_s[:, rows, :].reshape(nq, LANES).astype(BF16)
        s_c = jnp.where(first_seq, _dot(qz, ck_ref[b0].astype(BF16)), _dot(qz, ck_ref[b0 + 1].astype(BF16)))
        s_n = _dot_nt(qz, kn_s[rows, :].astype(BF16))
        rets = []
        for hh in range(B_HEADS):
            sl = slice(hh * LANES, (hh + 1) * LANES)
            q = qb_s[rows, sl].astype(BF16)
            kf = kb_s[rows, sl]
            kd = kf * kdec_ref[hh]
            v = vb_s[rows, sl].astype(BF16)
            sc = (_dot_nt(q, kf.astype(BF16)) * d8_ref[hh]).astype(BF16)
            crosses = []
            for j in range(2):
                S = st_ref[b0 + j, hh]
                crosses.append(_dot(q, S.astype(BF16)))
                kdj = jnp.where((r8 // dec) == j, kd, 0.0).astype(BF16)
                so_ref[b0 + j, hh] = gam_ref[hh] * S + _dot_tn(kdj, v)
            cross = jnp.where(r8 < dec, crosses[0], crosses[1]) * qdec_ref[hh]
            rets.append((sc, v, cross))
        stage1.append((s_c, s_n, rets))

    probs = []
    for s_c, s_n, _ in stage1:
        p_c, p_n = _softmax_parts([jnp.where(cache_mask, s_c, NEG), jnp.where(new_mask, s_n, NEG)], sink)
        probs.append((p_c.astype(BF16), p_n.astype(BF16)))

    for p in range(npair):
        b0 = 2 * p
        rows = pair_rows_of(p)
        p_c, p_n = probs[p]
        o = jnp.where(first_seq, _dot_nt(p_c, cv_ref[b0].astype(BF16)), _dot_nt(p_c, cv_ref[b0 + 1].astype(BF16)))
        o = o + _dot(p_n, vn_s[rows, :].astype(BF16))
        for m in range(A_TILES):
            oa = o[(2 * m) * pair_rows:(2 * m + 1) * pair_rows, :]
            ob = o[(2 * m + 1) * pair_rows:(2 * m + 2) * pair_rows, :]
            if m < A_TILES // 2:
                blk = jnp.where(lo8, oa, pltpu.roll(ob, A_HEAD_DIM, 1))
            else:
                blk = jnp.where(lo8, pltpu.roll(oa, A_HEAD_DIM, 1), ob)
            sl = slice(m * LANES, (m + 1) * LANES)
            mix_s[rows, sl] = g_s[rows, sl] * blk
        for hh, (sc, v, cross) in enumerate(stage1[p][2]):
            osl = slice(A_WIDTH + hh * LANES, A_WIDTH + (hh + 1) * LANES)
            mix_s[rows, osl] = g_s[rows, osl] * _row_rms(_dot(sc, v) + cross)

    for b in range(nseq):
        tok0 = (step * nseq + b) * dec
        tt = tok0 // LANES
        shift = (WINDOW - dec) - tok0 % LANES
        ko_ref[b] = jnp.where(keep, pltpu.roll(ck_ref[b], WINDOW - dec, 1), pltpu.roll(knt_s[tt], shift, 1))
        vo_ref[b] = jnp.where(keep, pltpu.roll(cv_ref[b], WINDOW - dec, 1), pltpu.roll(vnt_s[tt], shift, 1))

    @pl.when(step == pl.num_programs(1) - 1)
    def _():
        xs_s[...] = xs_s[...] + _dot(mix_s[...].astype(BF16), wout_ref[...])

    @pl.when(jnp.logical_and(layer == pl.num_programs(0) - 1, step == pl.num_programs(1) - 1))
    def _():
        y_ref[...] = xs_s[...]


def _stacked_spec(shape):
    nd = len(shape)
    return pl.BlockSpec((None,) + tuple(shape), lambda l, i: (l,) + (0,) * nd)


def _sample_layers(x, win, wout, ng, gq, gk, sink_col, gam, rot_a, rot_b, d8, qdec, kdec, ck, cv, st, dec):
    ntok, D = x.shape
    depth, nb = ck.shape[:2]
    nseq = SAMPLE_SEQS
    assert nb % nseq == 0 and nseq % 2 == 0 and ntok == nb * dec and 2 * dec == 8 and ntok % LANES == 0
    assert LANES % dec == 0 and WINDOW == LANES
    pair_rows = 2 * dec
    nq = A_HEADS * pair_rows
    smem = pl.BlockSpec(memory_space=pltpu.SMEM)
    kv = pl.BlockSpec((None, nseq, A_KV_WIDTH, WINDOW), lambda l, i: (l, i, 0, 0))
    stt = pl.BlockSpec((None, nseq, B_HEADS, B_DK, B_DV), lambda l, i: (l, i, 0, 0, 0))
    tab = _const_spec((ntok, LANES))
    return pl.pallas_call(
        functools.partial(_sample_kernel, nseq=nseq, dec=dec),
        grid=(depth, nb // nseq),
        in_specs=[smem, _const_spec((ntok, D)), _stacked_spec((D, IN_WIDTH)), _stacked_spec((MIX_WIDTH, D)),
                  _stacked_spec((1, D)), _stacked_spec((1, LANES)), _stacked_spec((1, LANES)),
                  tab, tab, tab, tab, tab,
                  _stacked_spec((nq, 1)), _const_spec((B_HEADS, pair_rows, pair_rows)),
                  _const_spec((B_HEADS, pair_rows, LANES)), _const_spec((B_HEADS, pair_rows, LANES)),
                  kv, kv, stt],
        out_specs=[pl.BlockSpec((ntok, D), lambda l, i: (0, 0)), kv, kv, stt],
        out_shape=[jax.ShapeDtypeStruct((ntok, D), F32),
                   jax.ShapeDtypeStruct(ck.shape, F32),
                   jax.ShapeDtypeStruct(cv.shape, F32),
                   jax.ShapeDtypeStruct(st.shape, F32)],
        scratch_shapes=[pltpu.VMEM((ntok, D), F32),
                        pltpu.VMEM((A_HEADS, ntok, LANES), F32),
                        pltpu.VMEM((ntok, LANES), F32),
                        pltpu.VMEM((ntok, LANES), F32),
                        pltpu.VMEM((ntok // LANES, A_KV_WIDTH, LANES), F32),
                        pltpu.VMEM((ntok // LANES, A_KV_WIDTH, LANES), F32),
                        pltpu.VMEM((ntok, B_QK_WIDTH), F32),
                        pltpu.VMEM((ntok, B_QK_WIDTH), F32),
                        pltpu.VMEM((ntok, B_V_WIDTH), F32),
                        pltpu.VMEM((ntok, MIX_WIDTH), F32),
                        pltpu.VMEM((ntok, MIX_WIDTH), F32)],
        compiler_params=pltpu.CompilerParams(dimension_semantics=("arbitrary", "arbitrary"),
                                             vmem_limit_bytes=VMEM_LIMIT),
        name="sample_layers",
    )(gam, x, win, wout, ng, gq, gk, *rot_a, *rot_b, sink_col, d8, qdec, kdec, ck, cv, st)


def _cos_sin(pos_start, n_pos, freq):
    if n_pos % LANES or pos_start:
        ang = (pos_start + jnp.arange(n_pos, dtype=jnp.int32)).astype(F32)[:, None] * freq[None, :]
        return jnp.cos(ang), jnp.sin(ang)
    a_hi = (jnp.arange(n_pos // LANES, dtype=jnp.int32) * LANES).astype(F32)[:, None] * freq[None, :]
    a_lo = jnp.arange(LANES, dtype=jnp.int32).astype(F32)[:, None] * freq[None, :]
    ch, sh = jnp.cos(a_hi)[:, None, :], jnp.sin(a_hi)[:, None, :]
    cl, sl = jnp.cos(a_lo)[None, :, :], jnp.sin(a_lo)[None, :, :]
    return (ch * cl - sh * sl).reshape(n_pos, LANES), (sh * cl + ch * sl).reshape(n_pos, LANES)


def _rot_tables_a(pos_start, n_pos):
    half = A_ROT_DIM // 2
    inv_freq = ROPE_THETA_A ** (-jnp.arange(half, dtype=F32) / half)
    d = jnp.arange(LANES, dtype=jnp.int32) % A_HEAD_DIM
    freq = jnp.where(d < A_ROT_DIM, inv_freq[d % half], 0.0)
    cos, sin = _cos_sin(pos_start, n_pos, freq)
    s1 = jnp.where(d < half, -sin, 0.0)
    s2 = jnp.where((d >= half) & (d < A_ROT_DIM), sin, 0.0)
    return cos, s1, s2


def _rot_tables_b(pos_start, n_pos):
    half = B_DK // 2
    inv_freq = ROPE_THETA_B ** (-jnp.arange(half, dtype=F32) / half)
    cos, sin = _cos_sin(pos_start, n_pos, jnp.tile(inv_freq, 2))
    return cos, jnp.where(jnp.arange(LANES) < half, -sin, sin)


def _log_decay():
    return jnp.log(1.0 - 2.0 ** (-5.0 - jnp.arange(B_HEADS, dtype=F32)))


def _decay_tables(C):
    lg = _log_decay()
    idx = jnp.arange(C, dtype=F32)
    diff = idx[:, None] - idx[None, :]
    dmat = jnp.where(diff >= 0, jnp.exp(lg[:, None, None] * jnp.maximum(diff, 0.0)), 0.0)
    q_dec = jnp.exp(lg[None, :] * (idx[:, None] + 1.0))
    k_dec = jnp.exp(lg[None, :] * (C - 1.0 - idx[:, None]))
    gam = jnp.exp(lg * C)
    return dmat, q_dec, k_dec, gam


def kernel(x_prompt, x_sample, cache_swa_k, cache_swa_v, state_ret, w_in, w_out, norm_g, q_norm_g, k_norm_g, sinks):
    N, T, D = x_prompt.shape
    nb, dec, _ = x_sample.shape
    w_buf = cache_swa_k.shape[2]
    assert w_buf == WINDOW

    rot_a_p, rot_b_p = _rot_tables_a(0, T), _rot_tables_b(0, T)
    rot_a_s = tuple(jnp.tile(a, (nb, 1)) for a in _rot_tables_a(PAST_LEN, dec))
    rot_b_s = tuple(jnp.tile(a, (nb, 1)) for a in _rot_tables_b(PAST_LEN, dec))

    dmat, q_dec, k_dec, gam_p = _decay_tables(RET_CHUNK)
    qdec_p = jnp.broadcast_to(q_dec.T[:, :, None], (B_HEADS, RET_CHUNK, LANES))
    kdec_p = jnp.broadcast_to(k_dec.T[:, :, None], (B_HEADS, RET_CHUNK, LANES))
    dmat4, q_dec4, k_dec4, gam_s = _decay_tables(dec)
    d8 = jnp.kron(jnp.eye(2, dtype=F32)[None], jnp.ones((1, dec, dec), F32)) * jnp.tile(dmat4, (1, 2, 2))
    qdec_s = jnp.broadcast_to(jnp.tile(q_dec4.T, (1, 2))[:, :, None], (B_HEADS, 2 * dec, LANES))
    kdec_s = jnp.broadcast_to(jnp.tile(k_dec4.T, (1, 2))[:, :, None], (B_HEADS, 2 * dec, LANES))

    w_in_b = w_in.astype(BF16)
    w_out_b = w_out.astype(BF16)
    ck = cache_swa_k.transpose(0, 1, 3, 4, 2).reshape(DEPTH, nb, A_KV_WIDTH, w_buf)
    cv = cache_swa_v.transpose(0, 1, 3, 4, 2).reshape(DEPTH, nb, A_KV_WIDTH, w_buf)

    ng = norm_g[:, None, :]
    gq = jnp.tile(q_norm_g, (1, HEADS_PER_TILE))[:, None, :] * (A_HEAD_DIM ** -0.5 * LOG2E)
    gk = jnp.tile(k_norm_g, (1, HEADS_PER_TILE))[:, None, :]
    sinks2 = sinks * LOG2E

    sink_col = jnp.repeat(sinks2, 2 * dec, axis=1)[:, :, None]
    xs, ks, vs, ss = _sample_layers(x_sample.reshape(nb * dec, D), w_in_b, w_out_b, ng, gq, gk, sink_col, gam_s,
                                    rot_a_s, rot_b_s, d8, qdec_s, kdec_s, ck, cv, state_ret, dec)

    xp = x_prompt
    kp_l, vp_l, sp_l = [], [], []
    for l in range(DEPTH):
        xp, kl, vl, S = _prompt_layer(l, xp, w_in_b, w_out_b, ng, gq, gk, sinks2[l], gam_p,
                                      rot_a_p, rot_b_p, dmat, qdec_p, kdec_p)
        kp_l.append(kl)
        vp_l.append(vl)
        sp_l.append(S)

    kv_p = (DEPTH, N, WINDOW, A_KV_HEADS, A_HEAD_DIM)
    kv_t = (DEPTH, nb, A_KV_HEADS, A_HEAD_DIM, w_buf)
    return (xp, xs.reshape(nb, dec, D),
            jnp.stack(kp_l).reshape(kv_p), jnp.stack(vp_l).reshape(kv_p), jnp.stack(sp_l),
            ks.reshape(kv_t).transpose(0, 1, 4, 2, 3), vs.reshape(kv_t).transpose(0, 1, 4, 2, 3), ss)
```

```python
import functools

import jax
import jax.numpy as jnp
from jax import lax
from jax.experimental import pallas as pl
from jax.experimental.pallas import tpu as pltpu

D_MODEL = 1024
DEPTH = 4
PAST_LEN = 8192
A_HEADS = 8
A_KV_HEADS = 2
A_GROUP = A_HEADS // A_KV_HEADS
A_HEAD_DIM = 64
WINDOW = 128
A_ROT_DIM = A_HEAD_DIM // 4
ROPE_THETA_A = 500000.0
B_HEADS = 4
B_DK = 128
B_DV = 128
RET_CHUNK = 128
ROPE_THETA_B = 10000.0
EPS = 1e-6

A_WIDTH = A_HEADS * A_HEAD_DIM
A_KV_WIDTH = A_KV_HEADS * A_HEAD_DIM
B_QK_WIDTH = B_HEADS * B_DK
B_V_WIDTH = B_HEADS * B_DV
MIX_WIDTH = A_WIDTH + B_V_WIDTH
OFF_QA = 0
OFF_KA = OFF_QA + A_WIDTH
OFF_VA = OFF_KA + A_KV_WIDTH
OFF_GA = OFF_VA + A_KV_WIDTH
OFF_QB = OFF_GA + A_WIDTH
OFF_KB = OFF_QB + B_QK_WIDTH
OFF_VB = OFF_KB + B_QK_WIDTH
OFF_GB = OFF_VB + B_V_WIDTH
IN_WIDTH = OFF_GB + B_V_WIDTH

LANES = 128
HEADS_PER_TILE = LANES // A_HEAD_DIM
A_TILES = A_WIDTH // LANES
NEG = -1e30
LOG2E = 1.4426950408889634

PROMPT_BLOCK = 512
SAMPLE_SEQS = 8
VMEM_LIMIT = 56 * 1024 * 1024

F32 = jnp.float32
BF16 = jnp.bfloat16


def _dot(a, b):
    return jnp.dot(a, b, preferred_element_type=F32)


def _dot_nt(a, b):
    return lax.dot_general(a, b, (((1,), (1,)), ((), ())), preferred_element_type=F32)


def _dot_tn(a, b):
    return lax.dot_general(a, b, (((0,), (0,)), ((), ())), preferred_element_type=F32)


def _silu(x):
    return x * (1.0 / (1.0 + jnp.exp(-x)))


def _row_rms(x):
    n = x.shape[-1]
    return x * lax.rsqrt(jnp.sum(x * x, axis=-1, keepdims=True) * (1.0 / n) + EPS)


def _pair_norm_rot(blk, g_row, ca, sa1, sa2, lo_mask):
    sq = blk * blk
    tot = jnp.sum(sq, axis=-1, keepdims=True)
    lo = jnp.sum(jnp.where(lo_mask, sq, 0.0), axis=-1, keepdims=True)
    hi = tot - lo
    inv_lo = lax.rsqrt(lo * (1.0 / A_HEAD_DIM) + EPS)
    inv_hi = lax.rsqrt(hi * (1.0 / A_HEAD_DIM) + EPS)
    xn = blk * jnp.where(lo_mask, inv_lo, inv_hi) * g_row
    half = A_ROT_DIM // 2
    return xn * ca + pltpu.roll(xn, LANES - half, 1) * sa1 + pltpu.roll(xn, half, 1) * sa2


def _rot_b(blk, cb, sb):
    return blk * cb + pltpu.roll(blk, B_DK // 2, 1) * sb


def _softmax_parts(parts, sink):
    mx = sink
    for s in parts:
        mx = jnp.maximum(mx, jnp.max(s, axis=-1, keepdims=True))
    ps = [jnp.exp2(s - mx) for s in parts]
    den = jnp.exp2(sink - mx)
    for p in ps:
        den = den + jnp.sum(p, axis=-1, keepdims=True)
    rinv = 1.0 / den
    return [p * rinv for p in ps]


def _prompt_kernel(sinks_ref, gam_ref, x_ref, win_ref, wout_ref, ng_ref, gq_ref, gk_ref,
                   ca_ref, sa1_ref, sa2_ref, cb_ref, sb_ref, dmat_ref, qdec_ref, kdec_ref,
                   y_ref, klast_ref, vlast_ref, S_ref,
                   q_s, kz_s, vt_s, qb_s, kb_s, kbd_s, vb_s, g_s, mix_s, bias_s, s_s, p_s, es_s,
                   sc_s, kv_s, sb_s, xkeep_s, *, tm, nt):
    g = pl.program_id(0)
    nchunk = tm // RET_CHUNK
    C = RET_CHUNK
    t_fin = jnp.maximum(g - 1, 0) % nt

    @pl.when(g == 0)
    def _():
        kj = lax.broadcasted_iota(jnp.int32, (2 * C, C), 0)
        qi = lax.broadcasted_iota(jnp.int32, (2 * C, C), 1)
        bias_s[...] = jnp.where((kj > qi) & (kj <= qi + C), 0.0, NEG)
        for ref in (q_s, kz_s, vt_s, qb_s, kb_s, kbd_s, vb_s, g_s):
            ref[1] = jnp.zeros(ref.shape[1:], ref.dtype)
        xkeep_s[...] = x_ref[0]

    @pl.when(t_fin == 0)
    def _():
        S_ref[...] = jnp.zeros_like(S_ref)

    def tile(a, i):
        return a[:, i * LANES:(i + 1) * LANES]

    def body(w, r):
        lo_mask = lax.broadcasted_iota(jnp.int32, (tm, LANES), 1) < A_HEAD_DIM
        ca, sa1, sa2 = ca_ref[...], sa1_ref[...], sa2_ref[...]
        cb, sb = cb_ref[...], sb_ref[...]
        first_bias = jnp.where(t_fin == 0, NEG, 0.0)

        def chunk_rows(c):
            return slice(c * C, (c + 1) * C)

        def window_rows(c):
            return slice(c * C, (c + 2) * C)

        def group_heads(grp):
            return range(grp * A_GROUP, (grp + 1) * A_GROUP)

        def window_cols(c):
            return slice(c * C, (c + 2) * C)

        def scores(c):
            for grp in range(A_KV_HEADS):
                tiles = range(grp * A_GROUP // HEADS_PER_TILE, (grp + 1) * A_GROUP // HEADS_PER_TILE)
                qp = jnp.concatenate([q_s[r, chunk_rows(c), m * LANES:(m + 1) * LANES] for m in tiles], axis=0)
                for half in range(HEADS_PER_TILE):
                    st = _dot_nt(kz_s[r, 2 * grp + half, window_rows(c), :], qp)
                    for i, m in enumerate(tiles):
                        s = st[:, i * C:(i + 1) * C] + bias_s[...]
                        if c == 0:
                            s = jnp.concatenate([s[:C, :] + first_bias, s[C:, :]], axis=0)
                        s_s[c * A_HEADS + m * HEADS_PER_TILE + half] = s

        def softmax(c):
            for grp in range(A_KV_HEADS):
                for i, hd in enumerate(group_heads(grp)):
                    s = s_s[c * A_HEADS + hd]
                    sink = sinks_ref[hd]
                    mx = jnp.maximum(jnp.max(s, axis=0, keepdims=True), sink)
                    p_s[c * A_KV_HEADS + grp, :, i * C:(i + 1) * C] = jnp.exp2(s - mx).astype(BF16)
                    es_s[c * A_KV_HEADS + grp, :, i * C:(i + 1) * C] = jnp.exp2(sink - mx)

        def attend(c):
            for grp in range(A_KV_HEADS):
                ot = _dot(vt_s[r, grp, :, window_cols(c)], p_s[c * A_KV_HEADS + grp])
                den = ot[A_HEAD_DIM:A_HEAD_DIM + 1, :] + es_s[c * A_KV_HEADS + grp]
                o = ot[0:A_HEAD_DIM, :] * (1.0 / den)
                for i in range(A_GROUP // HEADS_PER_TILE):
                    m = grp * A_GROUP // HEADS_PER_TILE + i
                    sl = slice(m * LANES, (m + 1) * LANES)
                    pair = jnp.concatenate([o[:, (HEADS_PER_TILE * i + j) * C:(HEADS_PER_TILE * i + j + 1) * C]
                                            for j in range(HEADS_PER_TILE)], axis=0)
                    mix_s[chunk_rows(c), sl] = (g_s[r, chunk_rows(c), sl] * pair.T).astype(BF16)

        def retain_local(c):
            for hh in range(B_HEADS):
                sl = slice(hh * LANES, (hh + 1) * LANES)
                rows = chunk_rows(c)
                sc_s[c * B_HEADS + hh] = (_dot_nt(qb_s[r, rows, sl], kb_s[r, rows, sl]) * dmat_ref[hh]).astype(BF16)
                kv_s[c * B_HEADS + hh] = _dot_tn(kbd_s[r, rows, sl], vb_s[r, rows, sl])

        def retain_states():
            for hh in range(B_HEADS):
                S = S_ref[0, hh]
                for c in range(nchunk):
                    sb_s[c * B_HEADS + hh] = S.astype(BF16)
                    S = gam_ref[hh] * S + kv_s[c * B_HEADS + hh]
                S_ref[0, hh] = S

        def retain_out(c):
            for hh in range(B_HEADS):
                sl = slice(hh * LANES, (hh + 1) * LANES)
                rows = chunk_rows(c)
                i = c * B_HEADS + hh
                ret = _dot(sc_s[i], vb_s[r, rows, sl]) + _dot(qb_s[r, rows, sl], sb_s[i]) * qdec_ref[hh]
                osl = slice(A_WIDTH + hh * LANES, A_WIDTH + (hh + 1) * LANES)
                mix_s[rows, osl] = (g_s[r, rows, osl] * _row_rms(ret)).astype(BF16)

        def project_attention():
            qa = _dot(h, win_ref[:, OFF_QA:OFF_QA + A_WIDTH])
            for m in range(A_TILES):
                q_s[w, :, m * LANES:(m + 1) * LANES] = _pair_norm_rot(
                    tile(qa, m), gq_ref[...], ca, sa1, sa2, lo_mask).astype(BF16)
            kv = _dot(h, win_ref[:, OFF_KA:OFF_KA + 2 * A_KV_WIDTH])
            kr = _pair_norm_rot(tile(kv, 0), gk_ref[...], ca, sa1, sa2, lo_mask)
            va = tile(kv, 1)
            klast_ref[0] = kr[tm - WINDOW:, :]
            vlast_ref[0] = va[tm - WINDOW:, :]
            sw = pltpu.roll(kr, A_HEAD_DIM, 1)
            kz_s[w, 0, C:, :] = jnp.where(lo_mask, kr, 0.0).astype(BF16)
            kz_s[w, 1, C:, :] = jnp.where(lo_mask, 0.0, sw).astype(BF16)
            kz_s[w, 2, C:, :] = jnp.where(lo_mask, sw, 0.0).astype(BF16)
            kz_s[w, 3, C:, :] = jnp.where(lo_mask, 0.0, kr).astype(BF16)
            vat = va.T
            for grp in range(A_KV_HEADS):
                vt_s[w, grp, 0:A_HEAD_DIM, C:] = vat[grp * A_HEAD_DIM:(grp + 1) * A_HEAD_DIM, :].astype(BF16)
                vt_s[w, grp, A_HEAD_DIM:, C:] = jnp.ones((LANES - A_HEAD_DIM, tm), BF16)
            for i in range(2 * A_KV_HEADS):
                kz_s[w, i, 0:C, :] = kz_s[r, i, tm:tm + C, :]
            for grp in range(A_KV_HEADS):
                vt_s[w, grp, :, 0:C] = vt_s[r, grp, :, tm:tm + C]

        def project_gate_a():
            ga = _dot(h, win_ref[:, OFF_GA:OFF_GA + A_WIDTH])
            for m in range(A_TILES):
                g_s[w, :, m * LANES:(m + 1) * LANES] = _silu(tile(ga, m))

        def project_qb():
            qb = _dot(h, win_ref[:, OFF_QB:OFF_QB + B_QK_WIDTH])
            for hh in range(B_HEADS):
                qb_s[w, :, hh * LANES:(hh + 1) * LANES] = _rot_b(tile(qb, hh), cb, sb).astype(BF16)

        def project_kb():
            kb = _dot(h, win_ref[:, OFF_KB:OFF_KB + B_QK_WIDTH])
            for hh in range(B_HEADS):
                sl = slice(hh * LANES, (hh + 1) * LANES)
                kbh = _rot_b(tile(kb, hh), cb, sb) * (B_DK ** -0.5)
                kb_s[w, :, sl] = kbh.astype(BF16)
                kbd_s[w, :, sl] = (kbh.reshape(nchunk, C, LANES) * kdec_ref[hh][None]).reshape(tm, LANES).astype(BF16)

        def project_vb_gate_b():
            vb_s[w] = _dot(h, win_ref[:, OFF_VB:OFF_VB + B_V_WIDTH]).astype(BF16)
            gb = _dot(h, win_ref[:, OFF_GB:OFF_GB + B_V_WIDTH])
            for hh in range(B_HEADS):
                g_s[w, :, A_WIDTH + hh * LANES:A_WIDTH + (hh + 1) * LANES] = _silu(tile(gb, hh))

        h = (_row_rms(x_ref[0]) * ng_ref[...]).astype(BF16)
        for c in range(nchunk):
            scores(c)
            retain_local(c)
        project_attention()
        for c in range(nchunk):
            softmax(c)
        retain_states()
        project_gate_a()
        project_qb()
        for c in range(nchunk):
            attend(c)
            retain_out(c)
        project_kb()
        project_vb_gate_b()
        y_ref[0] = xkeep_s[...] + _dot(mix_s[...], wout_ref[...])
        xkeep_s[...] = x_ref[0]

    @pl.when(g % 2 == 0)
    def _():
        body(0, 1)

    @pl.when(g % 2 == 1)
    def _():
        body(1, 0)


def _const_spec(shape):
    nd = len(shape)
    return pl.BlockSpec(shape, lambda *_: (0,) * nd, pipeline_mode=pl.Buffered(1))


def _layer_spec(shape, l):
    nd = len(shape)
    return pl.BlockSpec((None,) + tuple(shape), lambda *_: (l,) + (0,) * nd, pipeline_mode=pl.Buffered(1))


def _prompt_layer(l, x, win, wout, ng, gq, gk, sinks, gam, rot_a, rot_b, dmat, qdec, kdec):
    N, T, D = x.shape
    tm = PROMPT_BLOCK
    assert T % tm == 0 and tm % RET_CHUNK == 0 and WINDOW == RET_CHUNK
    nt = T // tm
    nblk = N * nt
    nchunk = tm // RET_CHUNK

    def proj_blk(g):
        return jnp.minimum(g, nblk - 1)

    def fin_blk(g):
        return jnp.maximum(g - 1, 0)

    smem = pl.BlockSpec(memory_space=pltpu.SMEM)
    tab = pl.BlockSpec((tm, LANES), lambda g: (proj_blk(g) % nt, 0))
    x_proj = pl.BlockSpec((1, tm, D), lambda g: (proj_blk(g) // nt, proj_blk(g) % nt, 0))
    x_fin = pl.BlockSpec((1, tm, D), lambda g: (fin_blk(g) // nt, fin_blk(g) % nt, 0))
    kv_out = pl.BlockSpec((1, WINDOW, LANES), lambda g: (proj_blk(g) // nt, 0, 0))

    def two(*shape, dtype):
        return pltpu.VMEM((2,) + shape, dtype)

    return pl.pallas_call(
        functools.partial(_prompt_kernel, tm=tm, nt=nt),
        grid=(nblk + 1,),
        in_specs=[smem, smem, x_proj,
                  _layer_spec((D, IN_WIDTH), l), _layer_spec((MIX_WIDTH, D), l),
                  _layer_spec((1, D), l), _layer_spec((1, LANES), l), _layer_spec((1, LANES), l),
                  tab, tab, tab, tab, tab,
                  _const_spec((B_HEADS, RET_CHUNK, RET_CHUNK)), _const_spec((B_HEADS, RET_CHUNK, LANES)),
                  _const_spec((B_HEADS, RET_CHUNK, LANES))],
        out_specs=[x_fin, kv_out, kv_out,
                   pl.BlockSpec((1, B_HEADS, B_DK, B_DV), lambda g: (fin_blk(g) // nt, 0, 0, 0))],
        out_shape=[jax.ShapeDtypeStruct((N, T, D), F32),
                   jax.ShapeDtypeStruct((N, WINDOW, LANES), F32),
                   jax.ShapeDtypeStruct((N, WINDOW, LANES), F32),
                   jax.ShapeDtypeStruct((N, B_HEADS, B_DK, B_DV), F32)],
        scratch_shapes=[two(tm, A_WIDTH, dtype=BF16),
                        two(2 * A_KV_HEADS, RET_CHUNK + tm, LANES, dtype=BF16),
                        two(A_KV_HEADS, LANES, RET_CHUNK + tm, dtype=BF16),
                        two(tm, B_QK_WIDTH, dtype=BF16),
                        two(tm, B_QK_WIDTH, dtype=BF16),
                        two(tm, B_QK_WIDTH, dtype=BF16),
                        two(tm, B_V_WIDTH, dtype=BF16),
                        two(tm, MIX_WIDTH, dtype=F32),
                        pltpu.VMEM((tm, MIX_WIDTH), BF16),
                        pltpu.VMEM((2 * RET_CHUNK, RET_CHUNK), F32),
                        pltpu.VMEM((nchunk * A_HEADS, 2 * RET_CHUNK, RET_CHUNK), F32),
                        pltpu.VMEM((nchunk * A_KV_HEADS, 2 * RET_CHUNK, A_GROUP * RET_CHUNK), BF16),
                        pltpu.VMEM((nchunk * A_KV_HEADS, 1, A_GROUP * RET_CHUNK), F32),
                        pltpu.VMEM((nchunk * B_HEADS, RET_CHUNK, RET_CHUNK), BF16),
                        pltpu.VMEM((nchunk * B_HEADS, B_DK, B_DV), F32),
                        pltpu.VMEM((nchunk * B_HEADS, B_DK, B_DV), BF16),
                        pltpu.VMEM((tm, D), F32)],
        compiler_params=pltpu.CompilerParams(dimension_semantics=("arbitrary",),
                                             vmem_limit_bytes=VMEM_LIMIT),
        name="prompt_layer",
    )(sinks, gam, x, win, wout, ng, gq, gk, *rot_a, *rot_b, dmat, qdec, kdec)


def _sample_kernel(gam_ref, x_ref, win_ref, wout_ref, ng_ref, gq_ref, gk_ref,
                   ca_ref, sa1_ref, sa2_ref, cb_ref, sb_ref, sink_ref, d8_ref, qdec_ref, kdec_ref,
                   ck_ref, cv_ref, st_ref,
                   y_ref, ko_ref, vo_ref, so_ref,
                   xs_s, qz_s, kn_s, vn_s, knt_s, vnt_s, qb_s, kb_s, vb_s, g_s, mix_s, *, nseq, dec):
    layer = pl.program_id(0)
    step = pl.program_id(1)
    ntok = x_ref.shape[0]
    pair_rows = 2 * dec
    npair = nseq // 2

    @pl.when(jnp.logical_and(layer == 0, step == 0))
    def _():
        xs_s[...] = x_ref[...]

    @pl.when(step == 0)
    def _():
        h = (_row_rms(xs_s[...]) * ng_ref[...]).astype(BF16)
        lo_mask = lax.broadcasted_iota(jnp.int32, (ntok, LANES), 1) < A_HEAD_DIM
        ca, sa1, sa2 = ca_ref[...], sa1_ref[...], sa2_ref[...]
        cb, sb = cb_ref[...], sb_ref[...]

        def tile(a, i):
            return a[:, i * LANES:(i + 1) * LANES]

        qa = _dot(h, win_ref[:, OFF_QA:OFF_QA + A_WIDTH])
        for m in range(A_TILES):
            qr = _pair_norm_rot(tile(qa, m), gq_ref[...], ca, sa1, sa2, lo_mask)
            sw = pltpu.roll(qr, A_HEAD_DIM, 1)
            if m < A_TILES // 2:
                qz_s[2 * m] = jnp.where(lo_mask, qr, 0.0)
                qz_s[2 * m + 1] = jnp.where(lo_mask, sw, 0.0)
            else:
                qz_s[2 * m] = jnp.where(lo_mask, 0.0, sw)
                qz_s[2 * m + 1] = jnp.where(lo_mask, 0.0, qr)
        kv = _dot(h, win_ref[:, OFF_KA:OFF_KA + 2 * A_KV_WIDTH])
        kn = _pair_norm_rot(tile(kv, 0), gk_ref[...], ca, sa1, sa2, lo_mask)
        vn = tile(kv, 1)
        kn_s[...] = kn
        vn_s[...] = vn
        for tt in range(ntok // LANES):
            knt_s[tt] = kn[tt * LANES:(tt + 1) * LANES, :].T
            vnt_s[tt] = vn[tt * LANES:(tt + 1) * LANES, :].T
        ga = _dot(h, win_ref[:, OFF_GA:OFF_GA + A_WIDTH])
        for m in range(A_TILES):
            g_s[:, m * LANES:(m + 1) * LANES] = _silu(tile(ga, m))
        qb = _dot(h, win_ref[:, OFF_QB:OFF_QB + B_QK_WIDTH])
        for hh in range(B_HEADS):
            qb_s[:, hh * LANES:(hh + 1) * LANES] = _rot_b(tile(qb, hh), cb, sb)
        kb = _dot(h, win_ref[:, OFF_KB:OFF_KB + B_QK_WIDTH])
        for hh in range(B_HEADS):
            kb_s[:, hh * LANES:(hh + 1) * LANES] = _rot_b(tile(kb, hh), cb, sb) * (B_DK ** -0.5)
        vb_s[...] = _dot(h, win_ref[:, OFF_VB:OFF_VB + B_V_WIDTH])
        gb = _dot(h, win_ref[:, OFF_GB:OFF_GB + B_V_WIDTH])
        for hh in range(B_HEADS):
            g_s[:, A_WIDTH + hh * LANES:A_WIDTH + (hh + 1) * LANES] = _silu(tile(gb, hh))

    nq = A_HEADS * pair_rows
    row = lax.broadcasted_iota(jnp.int32, (nq, LANES), 0)
    slot = lax.broadcasted_iota(jnp.int32, (nq, LANES), 1)
    first_seq = (row % pair_rows) < dec
    cache_mask = slot > (row % dec)
    rown = lax.broadcasted_iota(jnp.int32, (nq, pair_rows), 0)
    coln = lax.broadcasted_iota(jnp.int32, (nq, pair_rows), 1)
    new_mask = ((coln // dec) == ((rown % pair_rows) // dec)) & ((coln % dec) <= (rown % dec))
    lo8 = lax.broadcasted_iota(jnp.int32, (pair_rows, LANES), 1) < A_HEAD_DIM
    r8 = lax.broadcasted_iota(jnp.int32, (pair_rows, LANES), 0)
    keep = lax.broadcasted_iota(jnp.int32, (LANES, WINDOW), 1) < WINDOW - dec
    sink = sink_ref[...]

    def pair_rows_of(p):
        seq0 = step * nseq + 2 * p
        return pl.ds(pl.multiple_of(seq0 * dec, pair_rows), pair_rows)


    stage1 = []
    for p in range(npair):
        b0 = 2 * p
        rows = pair_rows_of(p)
        qz = qz_s[:, rows, :].reshape(nq, LANES).astype(BF16)
        s_c = jnp.where(first_seq, _dot(qz, ck_ref[b0].astype(BF16)), _dot(qz, ck_ref[b0 + 1].astype(BF16)))
        s_n = _dot_nt(qz, kn_s[rows, :].astype(BF16))
        rets = []
        for hh in range(B_HEADS):
            sl = slice(hh * LANES, (hh + 1) * LANES)
            q = qb_s[rows, sl].astype(BF16)
            kf = kb_s[rows, sl]
            kd = kf * kdec_ref[hh]
            v = vb_s[rows, sl].astype(BF16)
            sc = (_dot_nt(q, kf.astype(BF16)) * d8_ref[hh]).astype(BF16)
            crosses = []
            for j in range(2):
                S = st_ref[b0 + j, hh]
                crosses.append(_dot(q, S.astype(BF16)))
                kdj = jnp.where((r8 // dec) == j, kd, 0.0).astype(BF16)
                so_ref[b0 + j, hh] = gam_ref[hh] * S + _dot_tn(kdj, v)
            cross = jnp.where(r8 < dec, crosses[0], crosses[1]) * qdec_ref[hh]
            rets.append((sc, v, cross))
        stage1.append((s_c, s_n, rets))

    probs = []
    for s_c, s_n, _ in stage1:
        p_c, p_n = _softmax_parts([jnp.where(cache_mask, s_c, NEG), jnp.where(new_mask, s_n, NEG)], sink)
        probs.append((p_c.astype(BF16), p_n.astype(BF16)))

    for p in range(npair):
        b0 = 2 * p
        rows = pair_rows_of(p)
        p_c, p_n = probs[p]
        o = jnp.where(first_seq, _dot_nt(p_c, cv_ref[b0].astype(BF16)), _dot_nt(p_c, cv_ref[b0 + 1].astype(BF16)))
        o = o + _dot(p_n, vn_s[rows, :].astype(BF16))
        for m in range(A_TILES):
            oa = o[(2 * m) * pair_rows:(2 * m + 1) * pair_rows, :]
            ob = o[(2 * m + 1) * pair_rows:(2 * m + 2) * pair_rows, :]
            if m < A_TILES // 2:
                blk = jnp.where(lo8, oa, pltpu.roll(ob, A_HEAD_DIM, 1))
            else:
                blk = jnp.where(lo8, pltpu.roll(oa, A_HEAD_DIM, 1), ob)
            sl = slice(m * LANES, (m + 1) * LANES)
            mix_s[rows, sl] = g_s[rows, sl] * blk
        for hh, (sc, v, cross) in enumerate(stage1[p][2]):
            osl = slice(A_WIDTH + hh * LANES, A_WIDTH + (hh + 1) * LANES)
            mix_s[rows, osl] = g_s[rows, osl] * _row_rms(_dot(sc, v) + cross)

    for b in range(nseq):
        tok0 = (step * nseq + b) * dec
        tt = tok0 // LANES
        shift = (WINDOW - dec) - tok0 % LANES
        ko_ref[b] = jnp.where(keep, pltpu.roll(ck_ref[b], WINDOW - dec, 1), pltpu.roll(knt_s[tt], shift, 1))
        vo_ref[b] = jnp.where(keep, pltpu.roll(cv_ref[b], WINDOW - dec, 1), pltpu.roll(vnt_s[tt], shift, 1))

    @pl.when(step == pl.num_programs(1) - 1)
    def _():
        xs_s[...] = xs_s[...] + _dot(mix_s[...].astype(BF16), wout_ref[...])

    @pl.when(jnp.logical_and(layer == pl.num_programs(0) - 1, step == pl.num_programs(1) - 1))
    def _():
        y_ref[...] = xs_s[...]


def _stacked_spec(shape):
    nd = len(shape)
    return pl.BlockSpec((None,) + tuple(shape), lambda l, i: (l,) + (0,) * nd)


def _sample_layers(x, win, wout, ng, gq, gk, sink_col, gam, rot_a, rot_b, d8, qdec, kdec, ck, cv, st, dec):
    ntok, D = x.shape
    depth, nb = ck.shape[:2]
    nseq = SAMPLE_SEQS
    assert nb % nseq == 0 and nseq % 2 == 0 and ntok == nb * dec and 2 * dec == 8 and ntok % LANES == 0
    assert LANES % dec == 0 and WINDOW == LANES
    pair_rows = 2 * dec
    nq = A_HEADS * pair_rows
    smem = pl.BlockSpec(memory_space=pltpu.SMEM)
    kv = pl.BlockSpec((None, nseq, A_KV_WIDTH, WINDOW), lambda l, i: (l, i, 0, 0))
    stt = pl.BlockSpec((None, nseq, B_HEADS, B_DK, B_DV), lambda l, i: (l, i, 0, 0, 0))
    tab = _const_spec((ntok, LANES))
    return pl.pallas_call(
        functools.partial(_sample_kernel, nseq=nseq, dec=dec),
        grid=(depth, nb // nseq),
        in_specs=[smem, _const_spec((ntok, D)), _stacked_spec((D, IN_WIDTH)), _stacked_spec((MIX_WIDTH, D)),
                  _stacked_spec((1, D)), _stacked_spec((1, LANES)), _stacked_spec((1, LANES)),
                  tab, tab, tab, tab, tab,
                  _stacked_spec((nq, 1)), _const_spec((B_HEADS, pair_rows, pair_rows)),
                  _const_spec((B_HEADS, pair_rows, LANES)), _const_spec((B_HEADS, pair_rows, LANES)),
                  kv, kv, stt],
        out_specs=[pl.BlockSpec((ntok, D), lambda l, i: (0, 0)), kv, kv, stt],
        out_shape=[jax.ShapeDtypeStruct((ntok, D), F32),
                   jax.ShapeDtypeStruct(ck.shape, F32),
                   jax.ShapeDtypeStruct(cv.shape, F32),
                   jax.ShapeDtypeStruct(st.shape, F32)],
        scratch_shapes=[pltpu.VMEM((ntok, D), F32),
                        pltpu.VMEM((A_HEADS, ntok, LANES), F32),
                        pltpu.VMEM((ntok, LANES), F32),
                        pltpu.VMEM((ntok, LANES), F32),
                        pltpu.VMEM((ntok // LANES, A_KV_WIDTH, LANES), F32),
                        pltpu.VMEM((ntok // LANES, A_KV_WIDTH, LANES), F32),
                        pltpu.VMEM((ntok, B_QK_WIDTH), F32),
                        pltpu.VMEM((ntok, B_QK_WIDTH), F32),
                        pltpu.VMEM((ntok, B_V_WIDTH), F32),
                        pltpu.VMEM((ntok, MIX_WIDTH), F32),
                        pltpu.VMEM((ntok, MIX_WIDTH), F32)],
        compiler_params=pltpu.CompilerParams(dimension_semantics=("arbitrary", "arbitrary"),
                                             vmem_limit_bytes=VMEM_LIMIT),
        name="sample_layers",
    )(gam, x, win, wout, ng, gq, gk, *rot_a, *rot_b, sink_col, d8, qdec, kdec, ck, cv, st)


def _cos_sin(pos_start, n_pos, freq):
    if n_pos % LANES or pos_start:
        ang = (pos_start + jnp.arange(n_pos, dtype=jnp.int32)).astype(F32)[:, None] * freq[None, :]
        return jnp.cos(ang), jnp.sin(ang)
    a_hi = (jnp.arange(n_pos // LANES, dtype=jnp.int32) * LANES).astype(F32)[:, None] * freq[None, :]
    a_lo = jnp.arange(LANES, dtype=jnp.int32).astype(F32)[:, None] * freq[None, :]
    ch, sh = jnp.cos(a_hi)[:, None, :], jnp.sin(a_hi)[:, None, :]
    cl, sl = jnp.cos(a_lo)[None, :, :], jnp.sin(a_lo)[None, :, :]
    return (ch * cl - sh * sl).reshape(n_pos, LANES), (sh * cl + ch * sl).reshape(n_pos, LANES)


def _rot_tables_a(pos_start, n_pos):
    half = A_ROT_DIM // 2
    inv_freq = ROPE_THETA_A ** (-jnp.arange(half, dtype=F32) / half)
    d = jnp.arange(LANES, dtype=jnp.int32) % A_HEAD_DIM
    freq = jnp.where(d < A_ROT_DIM, inv_freq[d % half], 0.0)
    cos, sin = _cos_sin(pos_start, n_pos, freq)
    s1 = jnp.where(d < half, -sin, 0.0)
    s2 = jnp.where((d >= half) & (d < A_ROT_DIM), sin, 0.0)
    return cos, s1, s2


def _rot_tables_b(pos_start, n_pos):
    half = B_DK // 2
    inv_freq = ROPE_THETA_B ** (-jnp.arange(half, dtype=F32) / half)
    cos, sin = _cos_sin(pos_start, n_pos, jnp.tile(inv_freq, 2))
    return cos, jnp.where(jnp.arange(LANES) < half, -sin, sin)


def _log_decay():
    return jnp.log(1.0 - 2.0 ** (-5.0 - jnp.arange(B_HEADS, dtype=F32)))


def _decay_tables(C):
    lg = _log_decay()
    idx = jnp.arange(C, dtype=F32)
    diff = idx[:, None] - idx[None, :]
    dmat = jnp.where(diff >= 0, jnp.exp(lg[:, None, None] * jnp.maximum(diff, 0.0)), 0.0)
    q_dec = jnp.exp(lg[None, :] * (idx[:, None] + 1.0))
    k_dec = jnp.exp(lg[None, :] * (C - 1.0 - idx[:, None]))
    gam = jnp.exp(lg * C)
    return dmat, q_dec, k_dec, gam


def kernel(x_prompt, x_sample, cache_swa_k, cache_swa_v, state_ret, w_in, w_out, norm_g, q_norm_g, k_norm_g, sinks):
    N, T, D = x_prompt.shape
    nb, dec, _ = x_sample.shape
    w_buf = cache_swa_k.shape[2]
    assert w_buf == WINDOW

    rot_a_p, rot_b_p = _rot_tables_a(0, T), _rot_tables_b(0, T)
    rot_a_s = tuple(jnp.tile(a, (nb, 1)) for a in _rot_tables_a(PAST_LEN, dec))
    rot_b_s = tuple(jnp.tile(a, (nb, 1)) for a in _rot_tables_b(PAST_LEN, dec))

    dmat, q_dec, k_dec, gam_p = _decay_tables(RET_CHUNK)
    qdec_p = jnp.broadcast_to(q_dec.T[:, :, None], (B_HEADS, RET_CHUNK, LANES))
    kdec_p = jnp.broadcast_to(k_dec.T[:, :, None], (B_HEADS, RET_CHUNK, LANES))
    dmat4, q_dec4, k_dec4, gam_s = _decay_tables(dec)
    d8 = jnp.kron(jnp.eye(2, dtype=F32)[None], jnp.ones((1, dec, dec), F32)) * jnp.tile(dmat4, (1, 2, 2))
    qdec_s = jnp.broadcast_to(jnp.tile(q_dec4.T, (1, 2))[:, :, None], (B_HEADS, 2 * dec, LANES))
    kdec_s = jnp.broadcast_to(jnp.tile(k_dec4.T, (1, 2))[:, :, None], (B_HEADS, 2 * dec, LANES))

    w_in_b = w_in.astype(BF16)
    w_out_b = w_out.astype(BF16)
    ck = cache_swa_k.transpose(0, 1, 3, 4, 2).reshape(DEPTH, nb, A_KV_WIDTH, w_buf)
    cv = cache_swa_v.transpose(0, 1, 3, 4, 2).reshape(DEPTH, nb, A_KV_WIDTH, w_buf)

    ng = norm_g[:, None, :]
    gq = jnp.tile(q_norm_g, (1, HEADS_PER_TILE))[:, None, :] * (A_HEAD_DIM ** -0.5 * LOG2E)
    gk = jnp.tile(k_norm_g, (1, HEADS_PER_TILE))[:, None, :]
    sinks2 = sinks * LOG2E

    sink_col = jnp.repeat(sinks2, 2 * dec, axis=1)[:, :, None]
    xs, ks, vs, ss = _sample_layers(x_sample.reshape(nb * dec, D), w_in_b, w_out_b, ng, gq, gk, sink_col, gam_s,
                                    rot_a_s, rot_b_s, d8, qdec_s, kdec_s, ck, cv, state_ret, dec)

    xp = x_prompt
    kp_l, vp_l, sp_l = [], [], []
    for l in range(DEPTH):
        xp, kl, vl, S = _prompt_layer(l, xp, w_in_b, w_out_b, ng, gq, gk, sinks2[l], gam_p,
                                      rot_a_p, rot_b_p, dmat, qdec_p, kdec_p)
        kp_l.append(kl)
        vp_l.append(vl)
        sp_l.append(S)

    kv_p = (DEPTH, N, WINDOW, A_KV_HEADS, A_HEAD_DIM)
    kv_t = (DEPTH, nb, A_KV_HEADS, A_HEAD_DIM, w_buf)
    return (xp, xs.reshape(nb, dec, D),
            jnp.stack(kp_l).reshape(kv_p), jnp.stack(vp_l).reshape(kv_p), jnp.stack(sp_l),
            ks.reshape(kv_t).transpose(0, 1, 4, 2, 3), vs.reshape(kv_t).transpose(0, 1, 4, 2, 3), ss)
```

```python
import functools

import jax
import jax.numpy as jnp
from jax import lax
from jax.experimental import pallas as pl
from jax.experimental.pallas import tpu as pltpu

D_MODEL = 1024
DEPTH = 4
PAST_LEN = 8192
A_HEADS = 8
A_KV_HEADS = 2
A_GROUP = A_HEADS // A_KV_HEADS
A_HEAD_DIM = 64
WINDOW = 128
A_ROT_DIM = A_HEAD_DIM // 4
ROPE_THETA_A = 500000.0
B_HEADS = 4
B_DK = 128
B_DV = 128
RET_CHUNK = 128
ROPE_THETA_B = 10000.0
EPS = 1e-6

A_WIDTH = A_HEADS * A_HEAD_DIM
A_KV_WIDTH = A_KV_HEADS * A_HEAD_DIM
B_QK_WIDTH = B_HEADS * B_DK
B_V_WIDTH = B_HEADS * B_DV
MIX_WIDTH = A_WIDTH + B_V_WIDTH
OFF_QA = 0
OFF_KA = OFF_QA + A_WIDTH
OFF_VA = OFF_KA + A_KV_WIDTH
OFF_GA = OFF_VA + A_KV_WIDTH
OFF_QB = OFF_GA + A_WIDTH
OFF_KB = OFF_QB + B_QK_WIDTH
OFF_VB = OFF_KB + B_QK_WIDTH
OFF_GB = OFF_VB + B_V_WIDTH
IN_WIDTH = OFF_GB + B_V_WIDTH

LANES = 128
HEADS_PER_TILE = LANES // A_HEAD_DIM
A_TILES = A_WIDTH // LANES
NEG = -1e30
LOG2E = 1.4426950408889634

PROMPT_BLOCK = 512
SAMPLE_SEQS = 16
VMEM_LIMIT = 56 * 1024 * 1024

F32 = jnp.float32
BF16 = jnp.bfloat16


def _dot(a, b):
    return jnp.dot(a, b, preferred_element_type=F32)


def _dot_nt(a, b):
    return lax.dot_general(a, b, (((1,), (1,)), ((), ())), preferred_element_type=F32)


def _dot_tn(a, b):
    return lax.dot_general(a, b, (((0,), (0,)), ((), ())), preferred_element_type=F32)


def _silu(x):
    return x * (1.0 / (1.0 + jnp.exp(-x)))


def _row_rms(x):
    n = x.shape[-1]
    return x * lax.rsqrt(jnp.sum(x * x, axis=-1, keepdims=True) * (1.0 / n) + EPS)


def _pair_norm_rot(blk, g_row, ca, sa1, sa2, lo_mask):
    sq = blk * blk
    tot = jnp.sum(sq, axis=-1, keepdims=True)
    lo = jnp.sum(jnp.where(lo_mask, sq, 0.0), axis=-1, keepdims=True)
    hi = tot - lo
    inv_lo = lax.rsqrt(lo * (1.0 / A_HEAD_DIM) + EPS)
    inv_hi = lax.rsqrt(hi * (1.0 / A_HEAD_DIM) + EPS)
    xn = blk * jnp.where(lo_mask, inv_lo, inv_hi) * g_row
    half = A_ROT_DIM // 2
    return xn * ca + pltpu.roll(xn, LANES - half, 1) * sa1 + pltpu.roll(xn, half, 1) * sa2


def _rot_b(blk, cb, sb):
    return blk * cb + pltpu.roll(blk, B_DK // 2, 1) * sb


def _softmax_parts(parts, sink):
    mx = sink
    for s in parts:
        mx = jnp.maximum(mx, jnp.max(s, axis=-1, keepdims=True))
    ps = [jnp.exp2(s - mx) for s in parts]
    den = jnp.exp2(sink - mx)
    for p in ps:
        den = den + jnp.sum(p, axis=-1, keepdims=True)
    rinv = 1.0 / den
    return [p * rinv for p in ps]


def _prompt_kernel(sinks_ref, gam_ref, x_ref, xres_ref, win_ref, wout_ref, ng_ref, gq_ref, gk_ref,
                   ca_ref, sa1_ref, sa2_ref, cb_ref, sb_ref, dmat_ref, qdec_ref, kdec_ref,
                   y_ref, klast_ref, vlast_ref, S_ref,
                   q_s, kz_s, vt_s, qb_s, kb_s, kbd_s, vb_s, g_s, mix_s, bias_s, s_s, p_s, es_s,
                   sc_s, kv_s, sb_s, *, tm, nt):
    g = pl.program_id(0)
    nchunk = tm // RET_CHUNK
    C = RET_CHUNK
    t_fin = jnp.maximum(g - 1, 0) % nt

    @pl.when(g == 0)
    def _():
        kj = lax.broadcasted_iota(jnp.int32, (2 * C, C), 0)
        qi = lax.broadcasted_iota(jnp.int32, (2 * C, C), 1)
        bias_s[...] = jnp.where((kj > qi) & (kj <= qi + C), 0.0, NEG)
        for ref in (q_s, kz_s, vt_s, qb_s, kb_s, kbd_s, vb_s, g_s):
            ref[1] = jnp.zeros(ref.shape[1:], ref.dtype)

    @pl.when(t_fin == 0)
    def _():
        S_ref[...] = jnp.zeros_like(S_ref)

    def tile(a, i):
        return a[:, i * LANES:(i + 1) * LANES]

    def body(w, r):
        lo_mask = lax.broadcasted_iota(jnp.int32, (tm, LANES), 1) < A_HEAD_DIM
        ca, sa1, sa2 = ca_ref[...], sa1_ref[...], sa2_ref[...]
        cb, sb = cb_ref[...], sb_ref[...]
        first_bias = jnp.where(t_fin == 0, NEG, 0.0)

        def chunk_rows(c):
            return slice(c * C, (c + 1) * C)

        def window_rows(c):
            return slice(c * C, (c + 2) * C)

        def group_heads(grp):
            return range(grp * A_GROUP, (grp + 1) * A_GROUP)

        def window_cols(c):
            return slice(c * C, (c + 2) * C)

        def scores(c):
            for grp in range(A_KV_HEADS):
                tiles = range(grp * A_GROUP // HEADS_PER_TILE, (grp + 1) * A_GROUP // HEADS_PER_TILE)
                qp = jnp.concatenate([q_s[r, chunk_rows(c), m * LANES:(m + 1) * LANES] for m in tiles], axis=0)
                for half in range(HEADS_PER_TILE):
                    st = _dot_nt(kz_s[r, 2 * grp + half, window_rows(c), :], qp)
                    for i, m in enumerate(tiles):
                        s = st[:, i * C:(i + 1) * C] + bias_s[...]
                        if c == 0:
                            s = jnp.concatenate([s[:C, :] + first_bias, s[C:, :]], axis=0)
                        s_s[c * A_HEADS + m * HEADS_PER_TILE + half] = s

        def softmax(c):
            for grp in range(A_KV_HEADS):
                for i, hd in enumerate(group_heads(grp)):
                    s = s_s[c * A_HEADS + hd]
                    sink = sinks_ref[hd]
                    mx = jnp.maximum(jnp.max(s, axis=0, keepdims=True), sink)
                    p_s[c * A_KV_HEADS + grp, :, i * C:(i + 1) * C] = jnp.exp2(s - mx).astype(BF16)
                    es_s[c * A_KV_HEADS + grp, :, i * C:(i + 1) * C] = jnp.exp2(sink - mx)

        def attend(c):
            for grp in range(A_KV_HEADS):
                ot = _dot(vt_s[r, grp, :, window_cols(c)], p_s[c * A_KV_HEADS + grp])
                den = ot[A_HEAD_DIM:A_HEAD_DIM + 1, :] + es_s[c * A_KV_HEADS + grp]
                o = ot[0:A_HEAD_DIM, :] * (1.0 / den)
                for i in range(A_GROUP // HEADS_PER_TILE):
                    m = grp * A_GROUP // HEADS_PER_TILE + i
                    sl = slice(m * LANES, (m + 1) * LANES)
                    pair = jnp.concatenate([o[:, (HEADS_PER_TILE * i + j) * C:(HEADS_PER_TILE * i + j + 1) * C]
                                            for j in range(HEADS_PER_TILE)], axis=0)
                    mix_s[chunk_rows(c), sl] = (g_s[r, chunk_rows(c), sl] * pair.T).astype(BF16)

        def retain_local(c):
            for hh in range(B_HEADS):
                sl = slice(hh * LANES, (hh + 1) * LANES)
                rows = chunk_rows(c)
                sc_s[c * B_HEADS + hh] = (_dot_nt(qb_s[r, rows, sl], kb_s[r, rows, sl]) * dmat_ref[hh]).astype(BF16)
                kv_s[c * B_HEADS + hh] = _dot_tn(kbd_s[r, rows, sl], vb_s[r, rows, sl])

        def retain_states():
            for hh in range(B_HEADS):
                S = S_ref[0, hh]
                for c in range(nchunk):
                    sb_s[c * B_HEADS + hh] = S.astype(BF16)
                    S = gam_ref[hh] * S + kv_s[c * B_HEADS + hh]
                S_ref[0, hh] = S

        def retain_out(c):
            for hh in range(B_HEADS):
                sl = slice(hh * LANES, (hh + 1) * LANES)
                rows = chunk_rows(c)
                i = c * B_HEADS + hh
                ret = _dot(sc_s[i], vb_s[r, rows, sl]) + _dot(qb_s[r, rows, sl], sb_s[i]) * qdec_ref[hh]
                osl = slice(A_WIDTH + hh * LANES, A_WIDTH + (hh + 1) * LANES)
                mix_s[rows, osl] = (g_s[r, rows, osl] * _row_rms(ret)).astype(BF16)

        def project_attention():
            qa = _dot(h, win_ref[:, OFF_QA:OFF_QA + A_WIDTH])
            for m in range(A_TILES):
                q_s[w, :, m * LANES:(m + 1) * LANES] = _pair_norm_rot(
                    tile(qa, m), gq_ref[...], ca, sa1, sa2, lo_mask).astype(BF16)
            kv = _dot(h, win_ref[:, OFF_KA:OFF_KA + 2 * A_KV_WIDTH])
            kr = _pair_norm_rot(tile(kv, 0), gk_ref[...], ca, sa1, sa2, lo_mask)
            va = tile(kv, 1)
            klast_ref[0] = kr[tm - WINDOW:, :]
            vlast_ref[0] = va[tm - WINDOW:, :]
            sw = pltpu.roll(kr, A_HEAD_DIM, 1)
            kz_s[w, 0, C:, :] = jnp.where(lo_mask, kr, 0.0).astype(BF16)
            kz_s[w, 1, C:, :] = jnp.where(lo_mask, 0.0, sw).astype(BF16)
            kz_s[w, 2, C:, :] = jnp.where(lo_mask, sw, 0.0).astype(BF16)
            kz_s[w, 3, C:, :] = jnp.where(lo_mask, 0.0, kr).astype(BF16)
            vat = va.T
            for grp in range(A_KV_HEADS):
                vt_s[w, grp, 0:A_HEAD_DIM, C:] = vat[grp * A_HEAD_DIM:(grp + 1) * A_HEAD_DIM, :].astype(BF16)
                vt_s[w, grp, A_HEAD_DIM:, C:] = jnp.ones((LANES - A_HEAD_DIM, tm), BF16)
            for i in range(2 * A_KV_HEADS):
                kz_s[w, i, 0:C, :] = kz_s[r, i, tm:tm + C, :]
            for grp in range(A_KV_HEADS):
                vt_s[w, grp, :, 0:C] = vt_s[r, grp, :, tm:tm + C]

        def project_gate_a():
            ga = _dot(h, win_ref[:, OFF_GA:OFF_GA + A_WIDTH])
            for m in range(A_TILES):
                g_s[w, :, m * LANES:(m + 1) * LANES] = _silu(tile(ga, m))

        def project_qb():
            qb = _dot(h, win_ref[:, OFF_QB:OFF_QB + B_QK_WIDTH])
            for hh in range(B_HEADS):
                qb_s[w, :, hh * LANES:(hh + 1) * LANES] = _rot_b(tile(qb, hh), cb, sb).astype(BF16)

        def project_kb():
            kb = _dot(h, win_ref[:, OFF_KB:OFF_KB + B_QK_WIDTH])
            for hh in range(B_HEADS):
                sl = slice(hh * LANES, (hh + 1) * LANES)
                kbh = _rot_b(tile(kb, hh), cb, sb) * (B_DK ** -0.5)
                kb_s[w, :, sl] = kbh.astype(BF16)
                kbd_s[w, :, sl] = (kbh.reshape(nchunk, C, LANES) * kdec_ref[hh][None]).reshape(tm, LANES).astype(BF16)

        def project_vb_gate_b():
            vb_s[w] = _dot(h, win_ref[:, OFF_VB:OFF_VB + B_V_WIDTH]).astype(BF16)
            gb = _dot(h, win_ref[:, OFF_GB:OFF_GB + B_V_WIDTH])
            for hh in range(B_HEADS):
                g_s[w, :, A_WIDTH + hh * LANES:A_WIDTH + (hh + 1) * LANES] = _silu(tile(gb, hh))

        h = (_row_rms(x_ref[0]) * ng_ref[...]).astype(BF16)
        for c in range(nchunk):
            scores(c)
            retain_local(c)
        project_attention()
        for c in range(nchunk):
            softmax(c)
        retain_states()
        project_gate_a()
        project_qb()
        for c in range(nchunk):
            attend(c)
            retain_out(c)
        project_kb()
        project_vb_gate_b()
        y_ref[0] = xres_ref[0] + _dot(mix_s[...], wout_ref[...])

    @pl.when(g % 2 == 0)
    def _():
        body(0, 1)

    @pl.when(g % 2 == 1)
    def _():
        body(1, 0)


def _const_spec(shape):
    nd = len(shape)
    return pl.BlockSpec(shape, lambda *_: (0,) * nd, pipeline_mode=pl.Buffered(1))


def _layer_spec(shape, l):
    nd = len(shape)
    return pl.BlockSpec((None,) + tuple(shape), lambda *_: (l,) + (0,) * nd, pipeline_mode=pl.Buffered(1))


def _prompt_layer(l, x, win, wout, ng, gq, gk, sinks, gam, rot_a, rot_b, dmat, qdec, kdec):
    N, T, D = x.shape
    tm = PROMPT_BLOCK
    assert T % tm == 0 and tm % RET_CHUNK == 0 and WINDOW == RET_CHUNK
    nt = T // tm
    nblk = N * nt
    nchunk = tm // RET_CHUNK

    def proj_blk(g):
        return jnp.minimum(g, nblk - 1)

    def fin_blk(g):
        return jnp.maximum(g - 1, 0)

    smem = pl.BlockSpec(memory_space=pltpu.SMEM)
    tab = pl.BlockSpec((tm, LANES), lambda g: (proj_blk(g) % nt, 0))
    x_proj = pl.BlockSpec((1, tm, D), lambda g: (proj_blk(g) // nt, proj_blk(g) % nt, 0))
    x_fin = pl.BlockSpec((1, tm, D), lambda g: (fin_blk(g) // nt, fin_blk(g) % nt, 0))
    kv_out = pl.BlockSpec((1, WINDOW, LANES), lambda g: (proj_blk(g) // nt, 0, 0))

    def two(*shape, dtype):
        return pltpu.VMEM((2,) + shape, dtype)

    return pl.pallas_call(
        functools.partial(_prompt_kernel, tm=tm, nt=nt),
        grid=(nblk + 1,),
        in_specs=[smem, smem, x_proj, x_fin,
                  _layer_spec((D, IN_WIDTH), l), _layer_spec((MIX_WIDTH, D), l),
                  _layer_spec((1, D), l), _layer_spec((1, LANES), l), _layer_spec((1, LANES), l),
                  tab, tab, tab, tab, tab,
                  _const_spec((B_HEADS, RET_CHUNK, RET_CHUNK)), _const_spec((B_HEADS, RET_CHUNK, LANES)),
                  _const_spec((B_HEADS, RET_CHUNK, LANES))],
        out_specs=[x_fin, kv_out, kv_out,
                   pl.BlockSpec((1, B_HEADS, B_DK, B_DV), lambda g: (fin_blk(g) // nt, 0, 0, 0))],
        out_shape=[jax.ShapeDtypeStruct((N, T, D), F32),
                   jax.ShapeDtypeStruct((N, WINDOW, LANES), F32),
                   jax.ShapeDtypeStruct((N, WINDOW, LANES), F32),
                   jax.ShapeDtypeStruct((N, B_HEADS, B_DK, B_DV), F32)],
        scratch_shapes=[two(tm, A_WIDTH, dtype=BF16),
                        two(2 * A_KV_HEADS, RET_CHUNK + tm, LANES, dtype=BF16),
                        two(A_KV_HEADS, LANES, RET_CHUNK + tm, dtype=BF16),
                        two(tm, B_QK_WIDTH, dtype=BF16),
                        two(tm, B_QK_WIDTH, dtype=BF16),
                        two(tm, B_QK_WIDTH, dtype=BF16),
                        two(tm, B_V_WIDTH, dtype=BF16),
                        two(tm, MIX_WIDTH, dtype=F32),
                        pltpu.VMEM((tm, MIX_WIDTH), BF16),
                        pltpu.VMEM((2 * RET_CHUNK, RET_CHUNK), F32),
                        pltpu.VMEM((nchunk * A_HEADS, 2 * RET_CHUNK, RET_CHUNK), F32),
                        pltpu.VMEM((nchunk * A_KV_HEADS, 2 * RET_CHUNK, A_GROUP * RET_CHUNK), BF16),
                        pltpu.VMEM((nchunk * A_KV_HEADS, 1, A_GROUP * RET_CHUNK), F32),
                        pltpu.VMEM((nchunk * B_HEADS, RET_CHUNK, RET_CHUNK), BF16),
                        pltpu.VMEM((nchunk * B_HEADS, B_DK, B_DV), F32),
                        pltpu.VMEM((nchunk * B_HEADS, B_DK, B_DV), BF16)],
        compiler_params=pltpu.CompilerParams(dimension_semantics=("arbitrary",),
                                             vmem_limit_bytes=VMEM_LIMIT),
        name="prompt_layer",
    )(sinks, gam, x, x, win, wout, ng, gq, gk, *rot_a, *rot_b, dmat, qdec, kdec)


def _sample_kernel(gam_ref, x_ref, win_ref, wout_ref, ng_ref, gq_ref, gk_ref,
                   ca_ref, sa1_ref, sa2_ref, cb_ref, sb_ref, sink_ref, d8_ref, qdec_ref, kdec_ref,
                   ck_ref, cv_ref, st_ref,
                   y_ref, ko_ref, vo_ref, so_ref,
                   xs_s, qz_s, kn_s, vn_s, knt_s, vnt_s, qb_s, kb_s, vb_s, g_s, mix_s, *, nseq, dec):
    layer = pl.program_id(0)
    step = pl.program_id(1)
    ntok = x_ref.shape[0]
    pair_rows = 2 * dec
    npair = nseq // 2

    @pl.when(jnp.logical_and(layer == 0, step == 0))
    def _():
        xs_s[...] = x_ref[...]

    @pl.when(step == 0)
    def _():
        h = (_row_rms(xs_s[...]) * ng_ref[...]).astype(BF16)
        lo_mask = lax.broadcasted_iota(jnp.int32, (ntok, LANES), 1) < A_HEAD_DIM
        ca, sa1, sa2 = ca_ref[...], sa1_ref[...], sa2_ref[...]
        cb, sb = cb_ref[...], sb_ref[...]

        def tile(a, i):
            return a[:, i * LANES:(i + 1) * LANES]

        qa = _dot(h, win_ref[:, OFF_QA:OFF_QA + A_WIDTH])
        for m in range(A_TILES):
            qr = _pair_norm_rot(tile(qa, m), gq_ref[...], ca, sa1, sa2, lo_mask)
            sw = pltpu.roll(qr, A_HEAD_DIM, 1)
            if m < A_TILES // 2:
                qz_s[2 * m] = jnp.where(lo_mask, qr, 0.0)
                qz_s[2 * m + 1] = jnp.where(lo_mask, sw, 0.0)
            else:
                qz_s[2 * m] = jnp.where(lo_mask, 0.0, sw)
                qz_s[2 * m + 1] = jnp.where(lo_mask, 0.0, qr)
        kv = _dot(h, win_ref[:, OFF_KA:OFF_KA + 2 * A_KV_WIDTH])
        kn = _pair_norm_rot(tile(kv, 0), gk_ref[...], ca, sa1, sa2, lo_mask)
        vn = tile(kv, 1)
        kn_s[...] = kn
        vn_s[...] = vn
        for tt in range(ntok // LANES):
            knt_s[tt] = kn[tt * LANES:(tt + 1) * LANES, :].T
            vnt_s[tt] = vn[tt * LANES:(tt + 1) * LANES, :].T
        ga = _dot(h, win_ref[:, OFF_GA:OFF_GA + A_WIDTH])
        for m in range(A_TILES):
            g_s[:, m * LANES:(m + 1) * LANES] = _silu(tile(ga, m))
        qb = _dot(h, win_ref[:, OFF_QB:OFF_QB + B_QK_WIDTH])
        for hh in range(B_HEADS):
            qb_s[:, hh * LANES:(hh + 1) * LANES] = _rot_b(tile(qb, hh), cb, sb)
        kb = _dot(h, win_ref[:, OFF_KB:OFF_KB + B_QK_WIDTH])
        for hh in range(B_HEADS):
            kb_s[:, hh * LANES:(hh + 1) * LANES] = _rot_b(tile(kb, hh), cb, sb) * (B_DK ** -0.5)
        vb_s[...] = _dot(h, win_ref[:, OFF_VB:OFF_VB + B_V_WIDTH])
        gb = _dot(h, win_ref[:, OFF_GB:OFF_GB + B_V_WIDTH])
        for hh in range(B_HEADS):
            g_s[:, A_WIDTH + hh * LANES:A_WIDTH + (hh + 1) * LANES] = _silu(tile(gb, hh))

    nq = A_HEADS * pair_rows
    row = lax.broadcasted_iota(jnp.int32, (nq, LANES), 0)
    slot = lax.broadcasted_iota(jnp.int32, (nq, LANES), 1)
    first_seq = (row % pair_rows) < dec
    cache_mask = slot > (row % dec)
    rown = lax.broadcasted_iota(jnp.int32, (nq, pair_rows), 0)
    coln = lax.broadcasted_iota(jnp.int32, (nq, pair_rows), 1)
    new_mask = ((coln // dec) == ((rown % pair_rows) // dec)) & ((coln % dec) <= (rown % dec))
    lo8 = lax.broadcasted_iota(jnp.int32, (pair_rows, LANES), 1) < A_HEAD_DIM
    r8 = lax.broadcasted_iota(jnp.int32, (pair_rows, LANES), 0)
    keep = lax.broadcasted_iota(jnp.int32, (LANES, WINDOW), 1) < WINDOW - dec
    sink = sink_ref[...]

    def pair_rows_of(p):
        seq0 = step * nseq + 2 * p
        return pl.ds(pl.multiple_of(seq0 * dec, pair_rows), pair_rows)


    stage1 = []
    for p in range(npair):
        b0 = 2 * p
        rows = pair_rows_of(p)
        qz = qz_s[:, rows, :].reshape(nq, LANES).astype(BF16)
        s_c = jnp.where(first_seq, _dot(qz, ck_ref[b0].astype(BF16)), _dot(qz, ck_ref[b0 + 1].astype(BF16)))
        s_n = _dot_nt(qz, kn_s[rows, :].astype(BF16))
        rets = []
        for hh in range(B_HEADS):
            sl = slice(hh * LANES, (hh + 1) * LANES)
            q = qb_s[rows, sl].astype(BF16)
            kf = kb_s[rows, sl]
            kd = kf * kdec_ref[hh]
            v = vb_s[rows, sl].astype(BF16)
            sc = (_dot_nt(q, kf.astype(BF16)) * d8_ref[hh]).astype(BF16)
            crosses = []
            for j in range(2):
                S = st_ref[b0 + j, hh]
                crosses.append(_dot(q, S.astype(BF16)))
                kdj = jnp.where((r8 // dec) == j, kd, 0.0).astype(BF16)
                so_ref[b0 + j, hh] = gam_ref[hh] * S + _dot_tn(kdj, v)
            cross = jnp.where(r8 < dec, crosses[0], crosses[1]) * qdec_ref[hh]
            rets.append((sc, v, cross))
        stage1.append((s_c, s_n, rets))

    probs = []
    for s_c, s_n, _ in stage1:
        p_c, p_n = _softmax_parts([jnp.where(cache_mask, s_c, NEG), jnp.where(new_mask, s_n, NEG)], sink)
        probs.append((p_c.astype(BF16), p_n.astype(BF16)))

    for p in range(npair):
        b0 = 2 * p
        rows = pair_rows_of(p)
        p_c, p_n = probs[p]
        o = jnp.where(first_seq, _dot_nt(p_c, cv_ref[b0].astype(BF16)), _dot_nt(p_c, cv_ref[b0 + 1].astype(BF16)))
        o = o + _dot(p_n, vn_s[rows, :].astype(BF16))
        for m in range(A_TILES):
            oa = o[(2 * m) * pair_rows:(2 * m + 1) * pair_rows, :]
            ob = o[(2 * m + 1) * pair_rows:(2 * m + 2) * pair_rows, :]
            if m < A_TILES // 2:
                blk = jnp.where(lo8, oa, pltpu.roll(ob, A_HEAD_DIM, 1))
            else:
                blk = jnp.where(lo8, pltpu.roll(oa, A_HEAD_DIM, 1), ob)
            sl = slice(m * LANES, (m + 1) * LANES)
            mix_s[rows, sl] = g_s[rows, sl] * blk
        for hh, (sc, v, cross) in enumerate(stage1[p][2]):
            osl = slice(A_WIDTH + hh * LANES, A_WIDTH + (hh + 1) * LANES)
            mix_s[rows, osl] = g_s[rows, osl] * _row_rms(_dot(sc, v) + cross)

    for b in range(nseq):
        tok0 = (step * nseq + b) * dec
        tt = tok0 // LANES
        shift = (WINDOW - dec) - tok0 % LANES
        ko_ref[b] = jnp.where(keep, pltpu.roll(ck_ref[b], WINDOW - dec, 1), pltpu.roll(knt_s[tt], shift, 1))
        vo_ref[b] = jnp.where(keep, pltpu.roll(cv_ref[b], WINDOW - dec, 1), pltpu.roll(vnt_s[tt], shift, 1))

    @pl.when(step == pl.num_programs(1) - 1)
    def _():
        xs_s[...] = xs_s[...] + _dot(mix_s[...].astype(BF16), wout_ref[...])

    @pl.when(jnp.logical_and(layer == pl.num_programs(0) - 1, step == pl.num_programs(1) - 1))
    def _():
        y_ref[...] = xs_s[...]


def _stacked_spec(shape):
    nd = len(shape)
    return pl.BlockSpec((None,) + tuple(shape), lambda l, i: (l,) + (0,) * nd, pipeline_mode=pl.Buffered(1))


def _sample_layers(x, win, wout, ng, gq, gk, sink_col, gam, rot_a, rot_b, d8, qdec, kdec, ck, cv, st, dec):
    ntok, D = x.shape
    depth, nb = ck.shape[:2]
    nseq = SAMPLE_SEQS
    assert nb % nseq == 0 and nseq % 2 == 0 and ntok == nb * dec and 2 * dec == 8 and ntok % LANES == 0
    assert LANES % dec == 0 and WINDOW == LANES
    pair_rows = 2 * dec
    nq = A_HEADS * pair_rows
    smem = pl.BlockSpec(memory_space=pltpu.SMEM)
    kv = pl.BlockSpec((None, nseq, A_KV_WIDTH, WINDOW), lambda l, i: (l, i, 0, 0))
    stt = pl.BlockSpec((None, nseq, B_HEADS, B_DK, B_DV), lambda l, i: (l, i, 0, 0, 0))
    tab = _const_spec((ntok, LANES))
    return pl.pallas_call(
        functools.partial(_sample_kernel, nseq=nseq, dec=dec),
        grid=(depth, nb // nseq),
        in_specs=[smem, _const_spec((ntok, D)), _stacked_spec((D, IN_WIDTH)), _stacked_spec((MIX_WIDTH, D)),
                  _stacked_spec((1, D)), _stacked_spec((1, LANES)), _stacked_spec((1, LANES)),
                  tab, tab, tab, tab, tab,
                  _stacked_spec((nq, 1)), _const_spec((B_HEADS, pair_rows, pair_rows)),
                  _const_spec((B_HEADS, pair_rows, LANES)), _const_spec((B_HEADS, pair_rows, LANES)),
                  kv, kv, stt],
        out_specs=[pl.BlockSpec((ntok, D), lambda l, i: (0, 0)), kv, kv, stt],
        out_shape=[jax.ShapeDtypeStruct((ntok, D), F32),
                   jax.ShapeDtypeStruct(ck.shape, F32),
                   jax.ShapeDtypeStruct(cv.shape, F32),
                   jax.ShapeDtypeStruct(st.shape, F32)],
        scratch_shapes=[pltpu.VMEM((ntok, D), F32),
                        pltpu.VMEM((A_HEADS, ntok, LANES), F32),
                        pltpu.VMEM((ntok, LANES), F32),
                        pltpu.VMEM((ntok, LANES), F32),
                        pltpu.VMEM((ntok // LANES, A_KV_WIDTH, LANES), F32),
                        pltpu.VMEM((ntok // LANES, A_KV_WIDTH, LANES), F32),
                        pltpu.VMEM((ntok, B_QK_WIDTH), F32),
                        pltpu.VMEM((ntok, B_QK_WIDTH), F32),
                        pltpu.VMEM((ntok, B_V_WIDTH), F32),
                        pltpu.VMEM((ntok, MIX_WIDTH), F32),
                        pltpu.VMEM((ntok, MIX_WIDTH), F32)],
        compiler_params=pltpu.CompilerParams(dimension_semantics=("arbitrary", "arbitrary"),
                                             vmem_limit_bytes=VMEM_LIMIT),
        name="sample_layers",
    )(gam, x, win, wout, ng, gq, gk, *rot_a, *rot_b, sink_col, d8, qdec, kdec, ck, cv, st)


def _cos_sin(pos_start, n_pos, freq):
    if n_pos % LANES or pos_start:
        ang = (pos_start + jnp.arange(n_pos, dtype=jnp.int32)).astype(F32)[:, None] * freq[None, :]
        return jnp.cos(ang), jnp.sin(ang)
    a_hi = (jnp.arange(n_pos // LANES, dtype=jnp.int32) * LANES).astype(F32)[:, None] * freq[None, :]
    a_lo = jnp.arange(LANES, dtype=jnp.int32).astype(F32)[:, None] * freq[None, :]
    ch, sh = jnp.cos(a_hi)[:, None, :], jnp.sin(a_hi)[:, None, :]
    cl, sl = jnp.cos(a_lo)[None, :, :], jnp.sin(a_lo)[None, :, :]
    return (ch * cl - sh * sl).reshape(n_pos, LANES), (sh * cl + ch * sl).reshape(n_pos, LANES)


def _rot_tables_a(pos_start, n_pos):
    half = A_ROT_DIM // 2
    inv_freq = ROPE_THETA_A ** (-jnp.arange(half, dtype=F32) / half)
    d = jnp.arange(LANES, dtype=jnp.int32) % A_HEAD_DIM
    freq = jnp.where(d < A_ROT_DIM, inv_freq[d % half], 0.0)
    cos, sin = _cos_sin(pos_start, n_pos, freq)
    s1 = jnp.where(d < half, -sin, 0.0)
    s2 = jnp.where((d >= half) & (d < A_ROT_DIM), sin, 0.0)
    return cos, s1, s2


def _rot_tables_b(pos_start, n_pos):
    half = B_DK // 2
    inv_freq = ROPE_THETA_B ** (-jnp.arange(half, dtype=F32) / half)
    cos, sin = _cos_sin(pos_start, n_pos, jnp.tile(inv_freq, 2))
    return cos, jnp.where(jnp.arange(LANES) < half, -sin, sin)


def _log_decay():
    return jnp.log(1.0 - 2.0 ** (-5.0 - jnp.arange(B_HEADS, dtype=F32)))


def _decay_tables(C):
    lg = _log_decay()
    idx = jnp.arange(C, dtype=F32)
    diff = idx[:, None] - idx[None, :]
    dmat = jnp.where(diff >= 0, jnp.exp(lg[:, None, None] * jnp.maximum(diff, 0.0)), 0.0)
    q_dec = jnp.exp(lg[None, :] * (idx[:, None] + 1.0))
    k_dec = jnp.exp(lg[None, :] * (C - 1.0 - idx[:, None]))
    gam = jnp.exp(lg * C)
    return dmat, q_dec, k_dec, gam


def kernel(x_prompt, x_sample, cache_swa_k, cache_swa_v, state_ret, w_in, w_out, norm_g, q_norm_g, k_norm_g, sinks):
    N, T, D = x_prompt.shape
    nb, dec, _ = x_sample.shape
    w_buf = cache_swa_k.shape[2]
    assert w_buf == WINDOW

    rot_a_p, rot_b_p = _rot_tables_a(0, T), _rot_tables_b(0, T)
    rot_a_s = tuple(jnp.tile(a, (nb, 1)) for a in _rot_tables_a(PAST_LEN, dec))
    rot_b_s = tuple(jnp.tile(a, (nb, 1)) for a in _rot_tables_b(PAST_LEN, dec))

    dmat, q_dec, k_dec, gam_p = _decay_tables(RET_CHUNK)
    qdec_p = jnp.broadcast_to(q_dec.T[:, :, None], (B_HEADS, RET_CHUNK, LANES))
    kdec_p = jnp.broadcast_to(k_dec.T[:, :, None], (B_HEADS, RET_CHUNK, LANES))
    dmat4, q_dec4, k_dec4, gam_s = _decay_tables(dec)
    d8 = jnp.kron(jnp.eye(2, dtype=F32)[None], jnp.ones((1, dec, dec), F32)) * jnp.tile(dmat4, (1, 2, 2))
    qdec_s = jnp.broadcast_to(jnp.tile(q_dec4.T, (1, 2))[:, :, None], (B_HEADS, 2 * dec, LANES))
    kdec_s = jnp.broadcast_to(jnp.tile(k_dec4.T, (1, 2))[:, :, None], (B_HEADS, 2 * dec, LANES))

    w_in_b = w_in.astype(BF16)
    w_out_b = w_out.astype(BF16)
    ck = cache_swa_k.transpose(0, 1, 3, 4, 2).reshape(DEPTH, nb, A_KV_WIDTH, w_buf)
    cv = cache_swa_v.transpose(0, 1, 3, 4, 2).reshape(DEPTH, nb, A_KV_WIDTH, w_buf)

    ng = norm_g[:, None, :]
    gq = jnp.tile(q_norm_g, (1, HEADS_PER_TILE))[:, None, :] * (A_HEAD_DIM ** -0.5 * LOG2E)
    gk = jnp.tile(k_norm_g, (1, HEADS_PER_TILE))[:, None, :]
    sinks2 = sinks * LOG2E

    sink_col = jnp.repeat(sinks2, 2 * dec, axis=1)[:, :, None]
    xs, ks, vs, ss = _sample_layers(x_sample.reshape(nb * dec, D), w_in_b, w_out_b, ng, gq, gk, sink_col, gam_s,
                                    rot_a_s, rot_b_s, d8, qdec_s, kdec_s, ck, cv, state_ret, dec)

    xp = x_prompt
    kp_l, vp_l, sp_l = [], [], []
    for l in range(DEPTH):
        xp, kl, vl, S = _prompt_layer(l, xp, w_in_b, w_out_b, ng, gq, gk, sinks2[l], gam_p,
                                      rot_a_p, rot_b_p, dmat, qdec_p, kdec_p)
        kp_l.append(kl)
        vp_l.append(vl)
        sp_l.append(S)

    kv_p = (DEPTH, N, WINDOW, A_KV_HEADS, A_HEAD_DIM)
    kv_t = (DEPTH, nb, A_KV_HEADS, A_HEAD_DIM, w_buf)
    return (xp, xs.reshape(nb, dec, D),
            jnp.stack(kp_l).reshape(kv_p), jnp.stack(vp_l).reshape(kv_p), jnp.stack(sp_l),
            ks.reshape(kv_t).transpose(0, 1, 4, 2, 3), vs.reshape(kv_t).transpose(0, 1, 4, 2, 3), ss)
```
